```python
import math
import jax, jax.numpy as jnp
from jax import lax
import numpy as np

D_MODEL = 1024
BATCH = 2
SEQ = 8192
DEPTH = 4
DEC_BATCH = 128
DEC_SEQ = 8
PAST_LEN = 8192
PAGE_SIZE = 128

N_MIXERS = 3
N_SSD_LAYERS = (DEPTH + 2) // N_MIXERS
N_S5_LAYERS = (DEPTH + 1) // N_MIXERS
N_MLA_LAYERS = DEPTH // N_MIXERS
RMS_EPS = 1e-6

SSD_EXPAND = 2
SSD_D_INNER = SSD_EXPAND * D_MODEL
SSD_HEAD_DIM = 64
SSD_N_HEADS = SSD_D_INNER // SSD_HEAD_DIM
SSD_N_GROUPS = 4
SSD_HEADS_PER_GROUP = SSD_N_HEADS // SSD_N_GROUPS
SSD_D_STATE = 128
SSD_CONV = 4
SSD_CONV_DIM = SSD_D_INNER + 2 * SSD_N_GROUPS * SSD_D_STATE
SSD_IN_DIM = SSD_D_INNER + SSD_CONV_DIM + SSD_N_HEADS
SSD_CHUNK = 256

S5_GROUP = 16
S5_N_GROUPS = D_MODEL // S5_GROUP
S5_STATE = 64

MLA_HEADS = 8
MLA_Q_LORA = 384
MLA_KV_LORA = 256
MLA_NOPE = 128
MLA_ROPE = 64
MLA_V = 128
MLA_SCALE = 1.0 / math.sqrt(MLA_NOPE + MLA_ROPE)
ROPE_THETA = 10000.0
Q_BLOCK = 128

FFN_DIM = ((8 * D_MODEL // 3 + 127) // 128) * 128
FFN_CONV = 3

kernel_name = "hybrid_ssd_s5_mla_convffn_step"

F32 = jnp.float32


def rms_norm(x, w):
    xf = x.astype(F32)
    y = xf * lax.rsqrt(jnp.mean(xf * xf, axis=-1, keepdims=True) + RMS_EPS)
    return (y * w.astype(F32)).astype(x.dtype)


def causal_dwconv(x, prev, w, b):
    k = w.shape[0]
    length = x.shape[1]
    xp = jnp.concatenate([prev.astype(x.dtype), x], axis=1)
    y = sum((xp[:, j:j + length] * w[j] for j in range(k)), b)
    return y, xp[:, length:]


def ssd_chunked_scan(xdt, da, bm, cm, h0):
    bsz, length = xdt.shape[:2]
    q = min(SSD_CHUNK, length)
    nc = -(-length // q)
    pad = nc * q - length
    if pad:
        padw = lambda a: jnp.pad(a, [(0, 0), (0, pad)] + [(0, 0)] * (a.ndim - 2))
        xdt, da, bm, cm = padw(xdt), padw(da), padw(bm), padw(cm)
    chunk = lambda a: a.reshape((bsz, nc, q) + a.shape[2:])
    xdt, da, bm, cm = chunk(xdt), chunk(da), chunk(bm), chunk(cm)
    cs = jnp.cumsum(da, axis=2)
    causal = jnp.tril(jnp.ones((q, q), bool))[None, None, :, :, None, None]
    seg = cs[:, :, :, None] - cs[:, :, None, :]
    decay = jnp.exp(jnp.where(causal, seg, -jnp.inf))
    cb = jnp.einsum("bclgn,bcsgn->bclsg", cm, bm)
    y_diag = jnp.einsum("bclsg,bclsgr,bcsgrp->bclgrp", cb, decay, xdt)
    to_end = jnp.exp(cs[:, :, -1:] - cs)
    chunk_states = jnp.einsum("bclgn,bclgr,bclgrp->bcgrpn", bm, to_end, xdt)

    def carry_step(h, inp):
        tot, st = inp
        return h * jnp.exp(tot)[..., None, None] + st, h

    h_final, h_enter = lax.scan(carry_step, h0, (jnp.moveaxis(cs[:, :, -1], 1, 0), jnp.moveaxis(chunk_states, 1, 0)))
    h_enter = jnp.moveaxis(h_enter, 0, 1)
    y_off = jnp.einsum("bclgn,bcgrpn,bclgr->bclgrp", cm, h_enter, jnp.exp(cs))
    y = (y_diag + y_off).reshape((bsz, nc * q) + xdt.shape[3:])[:, :length]
    return y, h_final


def ssd_mixer(u, conv_prev, h0, w_in, conv_w, conv_b, dt_bias, a_log, d_skip, norm_w, w_out):
    bsz, length, _ = u.shape
    g, r, p, n = SSD_N_GROUPS, SSD_HEADS_PER_GROUP, SSD_HEAD_DIM, SSD_D_STATE
    proj = u @ w_in
    z = proj[..., :SSD_D_INNER]
    xbc = proj[..., SSD_D_INNER:SSD_D_INNER + SSD_CONV_DIM]
    dt = proj[..., SSD_D_INNER + SSD_CONV_DIM:]
    xbc, conv_new = causal_dwconv(xbc, conv_prev, conv_w, conv_b)
    xbc = jax.nn.silu(xbc)
    xs = xbc[..., :SSD_D_INNER].reshape(bsz, length, g, r, p).astype(F32)
    bm = xbc[..., SSD_D_INNER:SSD_D_INNER + g * n].reshape(bsz, length, g, n).astype(F32)
    cm = xbc[..., SSD_D_INNER + g * n:].reshape(bsz, length, g, n).astype(F32)
    dt = jax.nn.softplus(dt.astype(F32) + dt_bias.astype(F32)).reshape(bsz, length, g, r)
    a = -jnp.exp(a_log.astype(F32)).reshape(g, r)
    y, h_new = ssd_chunked_scan(xs * dt[..., None], dt * a, bm, cm, h0.astype(F32).reshape(bsz, g, r, p, n))
    y = (y + d_skip.astype(F32).reshape(g, r, 1) * xs).reshape(bsz, length, SSD_D_INNER)
    yg = (y * jax.nn.silu(z.astype(F32))).reshape(bsz, length, g, SSD_D_INNER // g)
    yg = yg * lax.rsqrt(jnp.mean(yg * yg, axis=-1, keepdims=True) + RMS_EPS)
    y = (yg.reshape(bsz, length, SSD_D_INNER) * norm_w.astype(F32)).astype(u.dtype)
    return y @ w_out, conv_new, h_new.reshape(bsz, SSD_N_HEADS, p, n).astype(h0.dtype)


def s5_mixer(u, h0_re, h0_im, a_re, a_im, log_step, b_re, b_im, c_re, c_im, d_skip, w_val, w_gate):
    bsz, length, _ = u.shape
    ug = u.astype(F32).reshape(bsz, length, S5_N_GROUPS, S5_GROUP)
    a_re, a_im = a_re.astype(F32), a_im.astype(F32)
    delta = jnp.exp(log_step.astype(F32))[:, None]
    mag = jnp.exp(a_re * delta)
    ang = a_im * delta
    lam_re, lam_im = mag * jnp.cos(ang), mag * jnp.sin(ang)
    den = a_re * a_re + a_im * a_im
    nr, ni = lam_re - 1.0, lam_im
    f_re = (nr * a_re + ni * a_im) / den
    f_im = (ni * a_re - nr * a_im) / den
    b_re, b_im = b_re.astype(F32), b_im.astype(F32)
    bb_re = f_re[..., None] * b_re - f_im[..., None] * b_im
    bb_im = f_re[..., None] * b_im + f_im[..., None] * b_re
    bu_re = jnp.einsum("blgc,gsc->blgs", ug, bb_re)
    bu_im = jnp.einsum("blgc,gsc->blgs", ug, bb_im)
    h0r, h0i = h0_re.astype(F32), h0_im.astype(F32)
    bu_re = bu_re.at[:, 0].add(lam_re * h0r - lam_im * h0i)
    bu_im = bu_im.at[:, 0].add(lam_re * h0i + lam_im * h0r)
    at_re = jnp.broadcast_to(lam_re, bu_re.shape)
    at_im = jnp.broadcast_to(lam_im, bu_im.shape)

    def combine(e1, e2):
        ar1, ai1, br1, bi1 = e1
        ar2, ai2, br2, bi2 = e2
        return (ar2 * ar1 - ai2 * ai1, ar2 * ai1 + ai2 * ar1,
                ar2 * br1 - ai2 * bi1 + br2, ar2 * bi1 + ai2 * br1 + bi2)

    _, _, h_re, h_im = lax.associative_scan(combine, (at_re, at_im, bu_re, bu_im), axis=1)
    y = (jnp.einsum("gcs,blgs->blgc", c_re.astype(F32), h_re)
         - jnp.einsum("gcs,blgs->blgc", c_im.astype(F32), h_im))
    y = y.reshape(bsz, length, D_MODEL) + d_skip.astype(F32) * u.astype(F32)
    gl = jax.nn.gelu(y, approximate=False).astype(u.dtype)
    out = (gl @ w_val) * jax.nn.sigmoid(gl @ w_gate)
    return out, h_re[:, -1].astype(h0_re.dtype), h_im[:, -1].astype(h0_im.dtype)


def rope(x, pos):
    half = x.shape[-1] // 2
    inv = ROPE_THETA ** (-jnp.arange(half, dtype=F32) / half)
    ang = pos.astype(F32)[:, None] * inv[None, :]
    shape = (ang.shape[0],) + (1,) * (x.ndim - 3) + (half,)
    cos, sin = jnp.cos(ang).reshape(shape), jnp.sin(ang).reshape(shape)
    xf = x.astype(F32)
    x1, x2 = xf[..., :half], xf[..., half:]
    return jnp.concatenate([x1 * cos - x2 * sin, x2 * cos + x1 * sin], axis=-1).astype(x.dtype)


def mla_project(u, pos, w_dq, q_norm, w_uq, w_dkv, kv_norm, w_uk):
    bsz, length, _ = u.shape
    cq = rms_norm(u @ w_dq, q_norm)
    q = (cq @ w_uq).reshape(bsz, length, MLA_HEADS, MLA_NOPE + MLA_ROPE)
    q_rope = rope(q[..., MLA_NOPE:], pos)
    q_lat = jnp.einsum("blhd,rhd->blhr", q[..., :MLA_NOPE], w_uk)
    kv = u @ w_dkv
    c_kv = rms_norm(kv[..., :MLA_KV_LORA], kv_norm)
    k_rope = rope(kv[..., MLA_KV_LORA:], pos)
    return q_lat, q_rope, c_kv, k_rope


def mla_core(q_lat, q_rope, c_kv, k_rope, mask):
    s = (jnp.einsum("bqhr,bkr->bhqk", q_lat, c_kv, preferred_element_type=F32)
         + jnp.einsum("bqhe,bke->bhqk", q_rope, k_rope, preferred_element_type=F32)) * MLA_SCALE
    s = jnp.where(mask, s, -jnp.inf)
    pr = jax.nn.softmax(s, axis=-1).astype(c_kv.dtype)
    return jnp.einsum("bhqk,bkr->bqhr", pr, c_kv)


def mla_prompt_attention(q_lat, q_rope, c_kv, k_rope):
    bsz, length = q_lat.shape[:2]
    qb = Q_BLOCK if length % Q_BLOCK == 0 else length
    nb = length // qb
    blocks = lambda a: jnp.moveaxis(a.reshape((bsz, nb, qb) + a.shape[2:]), 1, 0)
    key_pos = jnp.arange(length)

    def one_block(args):
        ql, qr, start = args
        q_pos = start + jnp.arange(qb)
        return mla_core(ql, qr, c_kv, k_rope, key_pos[None, :] <= q_pos[:, None])

    o = lax.map(one_block, (blocks(q_lat), blocks(q_rope), jnp.arange(nb) * qb))
    return jnp.moveaxis(o, 0, 1).reshape(bsz, length, MLA_HEADS, MLA_KV_LORA)


def mla_sample_attention(q_lat, q_rope, c_kv, k_rope, pool_lat, pool_rope, layer, page_table):
    nseq, npages = page_table.shape
    past = npages * PAGE_SIZE
    length = q_lat.shape[1]
    past_lat = pool_lat[layer, page_table].reshape(nseq, past, MLA_KV_LORA)
    past_rope = pool_rope[layer, page_table].reshape(nseq, past, MLA_ROPE)
    keys_lat = jnp.concatenate([past_lat, c_kv.astype(past_lat.dtype)], axis=1)
    keys_rope = jnp.concatenate([past_rope, k_rope.astype(past_rope.dtype)], axis=1)
    key_pos = jnp.arange(past + length)
    q_pos = past + jnp.arange(length)
    return mla_core(q_lat, q_rope, keys_lat, keys_rope, key_pos[None, :] <= q_pos[:, None])


def mla_output(o_lat, w_uv, w_o):
    o = jnp.einsum("bqhr,rhv->bqhv", o_lat, w_uv)
    return o.reshape(o.shape[:2] + (MLA_HEADS * MLA_V,)) @ w_o


def conv_ffn(u, conv_prev, w_gate, w_up, conv_w, conv_b, w_down):
    g, conv_new = causal_dwconv(u @ w_gate, conv_prev, conv_w, conv_b)
    return (jax.nn.silu(g) * (u @ w_up)) @ w_down, conv_new


def setup_inputs(seed: int = 0) -> dict:
    key = jax.random.key(seed)
    ks = iter(jax.random.split(key, 64))
    nk = lambda: next(ks)
    normal = lambda shape, scale: jax.random.normal(nk(), shape, F32) * scale
    gain = lambda shape: 1.0 + 0.02 * jax.random.normal(nk(), shape, F32)
    n_pages = PAST_LEN // PAGE_SIZE
    n_phys = (5 * DEC_BATCH * n_pages + 3) // 4
    x_prompt = normal((BATCH, SEQ, D_MODEL), 1.0)
    x_sample = normal((DEC_BATCH, DEC_SEQ, D_MODEL), 1.0)
    state_ssd = normal((N_SSD_LAYERS, DEC_BATCH, SSD_N_HEADS, SSD_HEAD_DIM, SSD_D_STATE), 0.1)
    state_ssd_conv = normal((N_SSD_LAYERS, DEC_BATCH, SSD_CONV - 1, SSD_CONV_DIM), 1.0)
    state_s5_re = normal((N_S5_LAYERS, DEC_BATCH, S5_N_GROUPS, S5_STATE), 0.5)
    state_s5_im = normal((N_S5_LAYERS, DEC_BATCH, S5_N_GROUPS, S5_STATE), 0.5)
    cache_mla_latent = normal((N_MLA_LAYERS, n_phys, PAGE_SIZE, MLA_KV_LORA), 1.0)
    cache_mla_krope = normal((N_MLA_LAYERS, n_phys, PAGE_SIZE, MLA_ROPE), 1.0)
    page_table = jax.random.permutation(nk(), n_phys)[:DEC_BATCH * n_pages].reshape(DEC_BATCH, n_pages).astype(jnp.int32)
    state_ffn_conv = normal((DEPTH, DEC_BATCH, FFN_CONV - 1, FFN_DIM), 1.0)
    dt0 = jnp.exp(jax.random.uniform(nk(), (N_SSD_LAYERS, SSD_N_HEADS), F32, math.log(1e-3), math.log(1e-1)))
    return {
        "x_prompt": x_prompt, "x_sample": x_sample,
        "state_ssd": state_ssd, "state_ssd_conv": state_ssd_conv,
        "state_s5_re": state_s5_re, "state_s5_im": state_s5_im,
        "cache_mla_latent": cache_mla_latent, "cache_mla_krope": cache_mla_krope,
        "page_table": page_table, "state_ffn_conv": state_ffn_conv,
        "norm_mix_pre": gain((DEPTH, D_MODEL)), "norm_mix_post": gain((DEPTH, D_MODEL)),
        "norm_ffn_pre": gain((DEPTH, D_MODEL)), "norm_ffn_post": gain((DEPTH, D_MODEL)),
        "ssd_w_in": normal((N_SSD_LAYERS, D_MODEL, SSD_IN_DIM), D_MODEL ** -0.5),
        "ssd_conv_w": normal((N_SSD_LAYERS, SSD_CONV, SSD_CONV_DIM), SSD_CONV ** -0.5),
        "ssd_conv_b": normal((N_SSD_LAYERS, SSD_CONV_DIM), 0.02),
        "ssd_dt_bias": dt0 + jnp.log(-jnp.expm1(-dt0)),
        "ssd_a_log": jnp.log(jax.random.uniform(nk(), (N_SSD_LAYERS, SSD_N_HEADS), F32, 1.0, 16.0)),
        "ssd_d": gain((N_SSD_LAYERS, SSD_N_HEADS)),
        "ssd_norm": gain((N_SSD_LAYERS, SSD_D_INNER)),
        "ssd_w_out": normal((N_SSD_LAYERS, SSD_D_INNER, D_MODEL), SSD_D_INNER ** -0.5),
        "s5_a_re": -0.5 + normal((N_S5_LAYERS, S5_N_GROUPS, S5_STATE), 0.01),
        "s5_a_im": math.pi * jnp.arange(S5_STATE, dtype=F32)[None, None, :] + normal((N_S5_LAYERS, S5_N_GROUPS, S5_STATE), 0.01),
        "s5_log_step": jax.random.uniform(nk(), (N_S5_LAYERS, S5_N_GROUPS), F32, math.log(1e-3), math.log(1e-1)),
        "s5_b_re": normal((N_S5_LAYERS, S5_N_GROUPS, S5_STATE, S5_GROUP), (2 * S5_GROUP) ** -0.5),
        "s5_b_im": normal((N_S5_LAYERS, S5_N_GROUPS, S5_STATE, S5_GROUP), (2 * S5_GROUP) ** -0.5),
        "s5_c_re": normal((N_S5_LAYERS, S5_N_GROUPS, S5_GROUP, S5_STATE), S5_STATE ** -0.5),
        "s5_c_im": normal((N_S5_LAYERS, S5_N_GROUPS, S5_GROUP, S5_STATE), S5_STATE ** -0.5),
        "s5_d": normal((N_S5_LAYERS, D_MODEL), 1.0),
        "s5_w_val": normal((N_S5_LAYERS, D_MODEL, D_MODEL), D_MODEL ** -0.5),
        "s5_w_gate": normal((N_S5_LAYERS, D_MODEL, D_MODEL), D_MODEL ** -0.5),
        "mla_w_dq": normal((N_MLA_LAYERS, D_MODEL, MLA_Q_LORA), D_MODEL ** -0.5),
        "mla_q_norm": gain((N_MLA_LAYERS, MLA_Q_LORA)),
        "mla_w_uq": normal((N_MLA_LAYERS, MLA_Q_LORA, MLA_HEADS * (MLA_NOPE + MLA_ROPE)), MLA_Q_LORA ** -0.5),
        "mla_w_dkv": normal((N_MLA_LAYERS, D_MODEL, MLA_KV_LORA + MLA_ROPE), D_MODEL ** -0.5),
        "mla_kv_norm": gain((N_MLA_LAYERS, MLA_KV_LORA)),
        "mla_w_uk": normal((N_MLA_LAYERS, MLA_KV_LORA, MLA_HEADS, MLA_NOPE), MLA_KV_LORA ** -0.5),
        "mla_w_uv": normal((N_MLA_LAYERS, MLA_KV_LORA, MLA_HEADS, MLA_V), MLA_KV_LORA ** -0.5),
        "mla_w_o": normal((N_MLA_LAYERS, MLA_HEADS * MLA_V, D_MODEL), (MLA_HEADS * MLA_V) ** -0.5),
        "ffn_w_gate": normal((DEPTH, D_MODEL, FFN_DIM), D_MODEL ** -0.5),
        "ffn_w_up": normal((DEPTH, D_MODEL, FFN_DIM), D_MODEL ** -0.5),
        "ffn_conv_w": normal((DEPTH, FFN_CONV, FFN_DIM), FFN_CONV ** -0.5),
        "ffn_conv_b": normal((DEPTH, FFN_DIM), 0.02),
        "ffn_w_down": normal((DEPTH, FFN_DIM, D_MODEL), FFN_DIM ** -0.5),
    }


def reference(x_prompt, x_sample, state_ssd, state_ssd_conv, state_s5_re, state_s5_im,
              cache_mla_latent, cache_mla_krope, page_table, state_ffn_conv,
              norm_mix_pre, norm_mix_post, norm_ffn_pre, norm_ffn_post,
              ssd_w_in, ssd_conv_w, ssd_conv_b, ssd_dt_bias, ssd_a_log, ssd_d, ssd_norm, ssd_w_out,
              s5_a_re, s5_a_im, s5_log_step, s5_b_re, s5_b_im, s5_c_re, s5_c_im, s5_d, s5_w_val, s5_w_gate,
              mla_w_dq, mla_q_norm, mla_w_uq, mla_w_dkv, mla_kv_norm, mla_w_uk, mla_w_uv, mla_w_o,
              ffn_w_gate, ffn_w_up, ffn_conv_w, ffn_conv_b, ffn_w_down):
    hp, hs = x_prompt, x_sample
    b_p, l_p = hp.shape[:2]
    b_s, l_s = hs.shape[:2]
    past_len = page_table.shape[1] * PAGE_SIZE
    pos_p = jnp.arange(l_p)
    pos_s = past_len + jnp.arange(l_s)
    ssd_h_p, ssd_c_p, s5r_p, s5i_p, lat_p, kr_p, ffc_p = [], [], [], [], [], [], []
    ssd_h_s, ssd_c_s, s5r_s, s5i_s, lat_s, kr_s, ffc_s = [], [], [], [], [], [], []
    for i in range(DEPTH):
        kind, j = i % N_MIXERS, i // N_MIXERS
        up = rms_norm(hp, norm_mix_pre[i])
        us = rms_norm(hs, norm_mix_pre[i])
        if kind == 0:
            w = (ssd_w_in[j], ssd_conv_w[j], ssd_conv_b[j], ssd_dt_bias[j], ssd_a_log[j], ssd_d[j], ssd_norm[j], ssd_w_out[j])
            mp, c_new, h_new = ssd_mixer(up, jnp.zeros((b_p, SSD_CONV - 1, SSD_CONV_DIM), up.dtype),
                                         jnp.zeros((b_p,) + state_ssd.shape[2:], state_ssd.dtype), *w)
            ssd_c_p.append(c_new)
            ssd_h_p.append(h_new)
            ms, c_new, h_new = ssd_mixer(us, state_ssd_conv[j], state_ssd[j], *w)
            ssd_c_s.append(c_new)
            ssd_h_s.append(h_new)
        elif kind == 1:
            w = (s5_a_re[j], s5_a_im[j], s5_log_step[j], s5_b_re[j], s5_b_im[j], s5_c_re[j], s5_c_im[j], s5_d[j], s5_w_val[j], s5_w_gate[j])
            zero_state = jnp.zeros((b_p, S5_N_GROUPS, S5_STATE), state_s5_re.dtype)
            mp, h_re, h_im = s5_mixer(up, zero_state, zero_state, *w)
            s5r_p.append(h_re)
            s5i_p.append(h_im)
            ms, h_re, h_im = s5_mixer(us, state_s5_re[j], state_s5_im[j], *w)
            s5r_s.append(h_re)
            s5i_s.append(h_im)
        else:
            w = (mla_w_dq[j], mla_q_norm[j], mla_w_uq[j], mla_w_dkv[j], mla_kv_norm[j], mla_w_uk[j])
            ql, qr, ckv, kr = mla_project(up, pos_p, *w)
            mp = mla_output(mla_prompt_attention(ql, qr, ckv, kr), mla_w_uv[j], mla_w_o[j])
            lat_p.append(ckv)
            kr_p.append(kr)
            ql, qr, ckv, kr = mla_project(us, pos_s, *w)
            ms = mla_output(mla_sample_attention(ql, qr, ckv, kr, cache_mla_latent, cache_mla_krope, j, page_table),
                            mla_w_uv[j], mla_w_o[j])
            lat_s.append(ckv)
            kr_s.append(kr)
        hp = hp + rms_norm(mp, norm_mix_post[i])
        hs = hs + rms_norm(ms, norm_mix_post[i])
        wf = (ffn_w_gate[i], ffn_w_up[i], ffn_conv_w[i], ffn_conv_b[i], ffn_w_down[i])
        fp, fc = conv_ffn(rms_norm(hp, norm_ffn_pre[i]), jnp.zeros((b_p, FFN_CONV - 1, FFN_DIM), hp.dtype), *wf)
        ffc_p.append(fc)
        fs, fc = conv_ffn(rms_norm(hs, norm_ffn_pre[i]), state_ffn_conv[i], *wf)
        ffc_s.append(fc)
        hp = hp + rms_norm(fp, norm_ffn_post[i])
        hs = hs + rms_norm(fs, norm_ffn_post[i])
    return (hp, hs,
            jnp.stack(ssd_h_p), jnp.stack(ssd_c_p), jnp.stack(s5r_p), jnp.stack(s5i_p),
            jnp.stack(lat_p), jnp.stack(kr_p), jnp.stack(ffc_p),
            jnp.stack(ssd_h_s), jnp.stack(ssd_c_s), jnp.stack(s5r_s), jnp.stack(s5i_s),
            jnp.stack(lat_s), jnp.stack(kr_s), jnp.stack(ffc_s))
```

```python
import functools
import math

import jax
import jax.numpy as jnp
from jax import lax
from jax.experimental import pallas as pl
from jax.experimental.pallas import tpu as pltpu

F32 = jnp.float32
BF16 = jnp.bfloat16
RMS_EPS = 1e-6
ROPE_THETA = 10000.0
PAGE_SIZE = 128
N_MIXERS = 3

SUBLANES = 8
LANES = 128
VMEM_LIMIT_BYTES = 56 * 1024 * 1024

SSD_CONV = 4
FFN_CONV = 3


def _cparams(n_axes=1):
    return pltpu.CompilerParams(dimension_semantics=("arbitrary",) * n_axes,
                                vmem_limit_bytes=VMEM_LIMIT_BYTES)


def _full(shape):
    nd = len(shape)
    return pl.BlockSpec(shape, lambda i: (0,) * nd)


def _row_tile(n, pref):
    t = min(n, pref)
    while n % t or t % SUBLANES:
        t -= SUBLANES
    assert t > 0
    return t


def _rms(x, w):
    return x * lax.rsqrt(jnp.mean(x * x, axis=-1, keepdims=True) + RMS_EPS) * w


def _silu(x):
    return x * (1.0 / (1.0 + jnp.exp(-x)))


def _softplus(x):
    return jnp.maximum(x, 0.0) + jnp.log1p(jnp.exp(-jnp.abs(x)))


def _dot(a, b):
    return jnp.dot(a, b, preferred_element_type=F32)


def _dot_nt(a, b):
    return lax.dot_general(a, b, (((1,), (1,)), ((), ())), preferred_element_type=F32)


def _dot_tn(a, b):
    return lax.dot_general(a, b, (((0,), (0,)), ((), ())), preferred_element_type=F32)


def _shifted_rows(x, prev8, s, seg):
    if s == 0:
        return x
    rows = x.shape[0]
    rolled = pltpu.roll(x, s, axis=0)
    if seg == rows:
        head = jnp.where(lax.broadcasted_iota(jnp.int32, (SUBLANES, x.shape[1]), 0) < s,
                         pltpu.roll(prev8, s, axis=0), rolled[:SUBLANES])
        if rows == SUBLANES:
            return head
        return jnp.concatenate([head, rolled[SUBLANES:]], axis=0)
    assert seg == SUBLANES
    hist = pltpu.roll(prev8, (rows + s - SUBLANES) % rows, axis=0) if rows > SUBLANES else pltpu.roll(prev8, s, axis=0)
    t = lax.broadcasted_iota(jnp.int32, x.shape, 0) % SUBLANES
    return jnp.where(t < s, hist, rolled)


def _causal_conv(x, prev8, w_ref, b_ref, seg):
    k = w_ref.shape[0]
    acc = b_ref[...]
    for j in range(k):
        acc = acc + _shifted_rows(x, prev8, k - 1 - j, seg) * w_ref[j:j + 1, :]
    return acc


def _ssd_in_kernel(x_ref, nw_ref, wz_ref, wx_ref, wdt_ref, z_ref, xbc_ref, dt_ref):
    u = _rms(x_ref[...], nw_ref[...]).astype(BF16)
    z_ref[...] = _dot(u, wz_ref[...])
    xbc_ref[...] = _dot(u, wx_ref[...])
    dt_ref[...] = _dot(u, wdt_ref[...])


def _ssd_in(x, nw, wz, wx, wdt):
    t, d = x.shape
    tm = _row_tile(t, 256)
    di, cd, dp = wz.shape[1], wx.shape[1], wdt.shape[1]
    row = lambda n: pl.BlockSpec((tm, n), lambda i: (i, 0))
    return pl.pallas_call(
        _ssd_in_kernel,
        grid=(t // tm,),
        in_specs=[row(d), _full(nw.shape), _full(wz.shape), _full(wx.shape), _full(wdt.shape)],
        out_specs=[row(di), row(cd), row(dp)],
        out_shape=[jax.ShapeDtypeStruct((t, di), F32), jax.ShapeDtypeStruct((t, cd), F32),
                   jax.ShapeDtypeStruct((t, dp), F32)],
        compiler_params=_cparams(),
        name="ssd_in",
    )(x, nw, wz, wx, wdt)


def _cumsum_rows(x):
    rows = x.shape[0]
    row = lax.broadcasted_iota(jnp.int32, x.shape, 0)
    k = 1
    while k < rows:
        x = x + jnp.where(row >= k, pltpu.roll(x, k, axis=0), 0.0)
        k *= 2
    return x


def _ssd_scan_kernel(xbc_ref, dt_ref, prev_ref, h0_ref, cw_ref, cb_ref, dtb_ref, alog_ref, dsk_ref,
                     y_ref, convnew_ref, hfin_ref, xpad_ref, dtpad_ref, hist_ref, h_ref,
                     *, lt, q, groups, hpg, hd, ds, nc):
    c = pl.program_id(1)
    di = groups * hpg * hd
    gn = groups * ds

    @pl.when(c == 0)
    def _():
        hist_ref[...] = prev_ref[0]
        h_ref[...] = h0_ref[0]

    xraw = xbc_ref[...]
    if lt < q:
        xpad_ref[...] = jnp.zeros_like(xpad_ref)
        xpad_ref[0:lt, :] = xraw
        dtpad_ref[...] = jnp.zeros_like(dtpad_ref)
        dtpad_ref[0:lt, :] = dt_ref[...]
        xfull = xpad_ref[...]
        dtraw = dtpad_ref[...]
    else:
        xfull = xraw
        dtraw = dt_ref[...]
    conv = _causal_conv(xfull, hist_ref[...], cw_ref, cb_ref, q)
    hist_ref[...] = xraw[lt - SUBLANES:lt, :]

    @pl.when(c == nc - 1)
    def _():
        convnew_ref[0] = xraw[lt - SUBLANES:lt, :]

    xact = _silu(conv)
    xs = xact[:, :di]
    bm = xact[:, di:di + gn]
    cm = xact[:, di + gn:]
    dt = _softplus(dtraw + dtb_ref[...])
    if lt < q:
        dt = jnp.where(lax.broadcasted_iota(jnp.int32, dt.shape, 0) < lt, dt, 0.0)
    a = -jnp.exp(alog_ref[...])
    cs = _cumsum_rows(dt * a)
    cs_t = cs.T
    cs_last = cs[q - 1:q, :]
    to_end_dt = jnp.exp(cs_last - cs) * dt
    ecs = jnp.exp(cs)
    e_last = jnp.exp(cs_last)
    causal = (lax.broadcasted_iota(jnp.int32, (q, q), 0) >= lax.broadcasted_iota(jnp.int32, (q, q), 1))
    dsk = dsk_ref[...]
    y_heads = []
    for g in range(groups):
        bg = bm[:, g * ds:(g + 1) * ds].astype(BF16)
        cg = cm[:, g * ds:(g + 1) * ds].astype(BF16)
        cb = _dot_nt(cg, bg)
        hg = h_ref[g * hpg:(g + 1) * hpg].reshape(hpg * hd, ds)
        yoff = _dot_nt(cg, hg.astype(BF16))
        xw = []
        for r in range(hpg):
            h = g * hpg + r
            lo = g * hpg * hd + r * hd
            xs_h = xs[:, lo:lo + hd]
            seg = cs[:, h:h + 1] - cs_t[h:h + 1, :]
            m = (cb * jnp.exp(jnp.where(causal, seg, -jnp.inf))).astype(BF16)
            yd = _dot(m, (xs_h * dt[:, h:h + 1]).astype(BF16))
            yo = yoff[:, r * hd:(r + 1) * hd] * ecs[:, h:h + 1]
            y_heads.append(yd + yo + dsk[:, h:h + 1] * xs_h)
            xw.append(xs_h * to_end_dt[:, h:h + 1])
        s_g = _dot_tn(jnp.concatenate(xw, axis=1).astype(BF16), bg)
        for r in range(hpg):
            h = g * hpg + r
            h_ref[h] = h_ref[h] * e_last[:, h:h + 1] + s_g[r * hd:(r + 1) * hd, :]
    y = jnp.concatenate(y_heads, axis=1)
    y_ref[...] = y[0:lt, :]

    @pl.when(c == nc - 1)
    def _():
        hfin_ref[0] = h_ref[...]


def _ssd_scan(xbc, dt, prev8, h0, cw, cb, dtb, alog, dsk, *, nb, seqlen, groups):
    _, nh, hd, ds = h0.shape
    cd = xbc.shape[1]
    hpg = nh // groups
    di = nh * hd
    if seqlen >= 256:
        lt = q = 256
    else:
        lt, q = seqlen, LANES
    assert seqlen % lt == 0 and lt % SUBLANES == 0 and lt >= SUBLANES
    nc = seqlen // lt
    kern = functools.partial(_ssd_scan_kernel, lt=lt, q=q, groups=groups, hpg=hpg, hd=hd, ds=ds, nc=nc)
    row = lambda n: pl.BlockSpec((lt, n), lambda b, c: (b * nc + c, 0))
    par = lambda a: pl.BlockSpec(a.shape, lambda b, c: (0,) * a.ndim)
    return pl.pallas_call(
        kern,
        grid=(nb, nc),
        in_specs=[row(cd), row(dt.shape[1]),
                  pl.BlockSpec((1, SUBLANES, cd), lambda b, c: (b, 0, 0)),
                  pl.BlockSpec((1, nh, hd, ds), lambda b, c: (b, 0, 0, 0)),
                  par(cw), par(cb), par(dtb), par(alog), par(dsk)],
        out_specs=[row(di),
                   pl.BlockSpec((1, SUBLANES, cd), lambda b, c: (b, 0, 0)),
                   pl.BlockSpec((1, nh, hd, ds), lambda b, c: (b, 0, 0, 0))],
        out_shape=[jax.ShapeDtypeStruct((nb * seqlen, di), F32),
                   jax.ShapeDtypeStruct((nb, SUBLANES, cd), F32),
                   jax.ShapeDtypeStruct((nb, nh, hd, ds), F32)],
        scratch_shapes=[pltpu.VMEM((q, cd), F32), pltpu.VMEM((q, dt.shape[1]), F32),
                        pltpu.VMEM((SUBLANES, cd), F32), pltpu.VMEM((nh, hd, ds), F32)],
        compiler_params=_cparams(2),
        name="ssd_scan",
    )(xbc, dt, prev8, h0, cw, cb, dtb, alog, dsk)


def _ssd_out_kernel(y_ref, z_ref, x_ref, gw_ref, wo_ref, pw_ref, o_ref, *, groups):
    yg = y_ref[...] * _silu(z_ref[...])
    gs = yg.shape[1] // groups
    parts = []
    for g in range(groups):
        blk = yg[:, g * gs:(g + 1) * gs]
        parts.append(blk * lax.rsqrt(jnp.mean(blk * blk, axis=-1, keepdims=True) + RMS_EPS))
    yn = (jnp.concatenate(parts, axis=1) * gw_ref[...]).astype(BF16)
    o_ref[...] = x_ref[...] + _rms(_dot(yn, wo_ref[...]), pw_ref[...])


def _ssd_out(y, z, x, gw, wo, pw, *, groups):
    t, d = x.shape
    tm = _row_tile(t, 256)
    row = lambda n: pl.BlockSpec((tm, n), lambda i: (i, 0))
    return pl.pallas_call(
        functools.partial(_ssd_out_kernel, groups=groups),
        grid=(t // tm,),
        in_specs=[row(y.shape[1]), row(z.shape[1]), row(d), _full(gw.shape), _full(wo.shape), _full(pw.shape)],
        out_specs=row(d),
        out_shape=jax.ShapeDtypeStruct((t, d), F32),
        compiler_params=_cparams(),
        name="ssd_out",
    )(y, z, x, gw, wo, pw)


def _ffn_kernel(x_ref, prev_ref, nw_ref, wg_ref, wu_ref, cw_ref, cb_ref, wd_ref, pw_ref,
                o_ref, convnew_ref, hist_ref, *, seg, tiles_per_seq):
    x = x_ref[...]
    tm = x.shape[0]
    u = _rms(x, nw_ref[...]).astype(BF16)
    g = _dot(u, wg_ref[...])
    up = _dot(u, wu_ref[...])
    if seg == tm:
        i = pl.program_id(0)

        @pl.when(i % tiles_per_seq == 0)
        def _():
            hist_ref[...] = prev_ref[...]

        gc = _causal_conv(g, hist_ref[...], cw_ref, cb_ref, seg)
        hist_ref[...] = g[tm - SUBLANES:, :]
        convnew_ref[...] = g[tm - SUBLANES:, :]
    else:
        gc = _causal_conv(g, prev_ref[...], cw_ref, cb_ref, seg)
        convnew_ref[...] = g
    act = (_silu(gc) * up).astype(BF16)
    o_ref[...] = x + _rms(_dot(act, wd_ref[...]), pw_ref[...])


def _ffn(x, prev8, nw, wg, wu, cw, cb, wd, pw, *, nb, seqlen):
    t, d = x.shape
    f = wg.shape[1]
    if seqlen == SUBLANES:
        tm = _row_tile(t, 256)
        seg, tps = SUBLANES, 1
        prev_spec = pl.BlockSpec((tm, f), lambda i: (i, 0))
        new_spec = pl.BlockSpec((tm, f), lambda i: (i, 0))
    else:
        tm = _row_tile(seqlen, 256)
        seg, tps = tm, seqlen // tm
        prev_spec = pl.BlockSpec((SUBLANES, f), lambda i: (i // tps, 0))
        new_spec = pl.BlockSpec((SUBLANES, f), lambda i: (i // tps, 0))
    row = pl.BlockSpec((tm, d), lambda i: (i, 0))
    return pl.pallas_call(
        functools.partial(_ffn_kernel, seg=seg, tiles_per_seq=tps),
        grid=(t // tm,),
        in_specs=[row, prev_spec, _full(nw.shape), _full(wg.shape), _full(wu.shape), _full(cw.shape),
                  _full(cb.shape), _full(wd.shape), _full(pw.shape)],
        out_specs=[row, new_spec],
        out_shape=[jax.ShapeDtypeStruct((t, d), F32), jax.ShapeDtypeStruct((nb * SUBLANES, f), F32)],
        scratch_shapes=[pltpu.VMEM((SUBLANES, f), F32)],
        compiler_params=_cparams(),
        name="conv_ffn",
    )(x, prev8, nw, wg, wu, cw, cb, wd, pw)


def _gelu(x):
    return 0.5 * x * (1.0 + lax.erf(x * (1.0 / math.sqrt(2.0))))


def _s5_kernel(x_ref, h0r_ref, h0i_ref, nw_ref, wbr_ref, wbi_ref, lamr_ref, lami_ref, wcr_ref, wci_ref,
               dsk_ref, wv_ref, wg_ref, pw_ref, o_ref, hr_out_ref, hi_out_ref, sr_ref, si_ref, cr_ref, ci_ref,
               *, long_seq, tiles_per_seq, kb, nst):
    x = x_ref[...]
    tm, d = x.shape
    u = _rms(x, nw_ref[...])
    ub = u.astype(BF16)
    kw = d // kb
    st_per_kb = nst // kb
    if long_seq:
        i = pl.program_id(0)

        @pl.when(i % tiles_per_seq == 0)
        def _():
            cr_ref[...] = h0r_ref[0]
            ci_ref[...] = h0i_ref[0]

        for b in range(kb):
            blk = ub[:, b * kw:(b + 1) * kw]
            pr = _dot(blk, wbr_ref[b])
            pi = _dot(blk, wbi_ref[b])
            for j in range(st_per_kb):
                st = b * st_per_kb + j
                sr_ref[pl.ds(st, tm, stride=nst), :] = pr[:, j * LANES:(j + 1) * LANES]
                si_ref[pl.ds(st, tm, stride=nst), :] = pi[:, j * LANES:(j + 1) * LANES]
        lr = lamr_ref[...]
        li = lami_ref[...]

        def step(t, carry):
            hr, hi = carry
            off = pl.multiple_of(t * nst, nst)
            br = sr_ref[pl.ds(off, nst), :]
            bi = si_ref[pl.ds(off, nst), :]
            nr = lr * hr - li * hi + br
            ni = lr * hi + li * hr + bi
            sr_ref[pl.ds(off, nst), :] = nr
            si_ref[pl.ds(off, nst), :] = ni
            return nr, ni

        hr, hi = lax.fori_loop(0, tm, step, (cr_ref[...], ci_ref[...]), unroll=8)
        cr_ref[...] = hr
        ci_ref[...] = hi
        hr_out_ref[0] = hr
        hi_out_ref[0] = hi
        load_r = lambda st: sr_ref[pl.ds(st, tm, stride=nst), :]
        load_i = lambda st: si_ref[pl.ds(st, tm, stride=nst), :]
    else:
        nseq = tm // SUBLANES
        for b in range(kb):
            blk = ub[:, b * kw:(b + 1) * kw]
            pr = _dot(blk, wbr_ref[b])
            pi = _dot(blk, wbi_ref[b])
            for j in range(st_per_kb):
                sr_ref[b * st_per_kb + j] = pr[:, j * LANES:(j + 1) * LANES]
                si_ref[b * st_per_kb + j] = pi[:, j * LANES:(j + 1) * LANES]
        for st in range(nst):
            lanes = slice(st * LANES, (st + 1) * LANES)
            lr = lamr_ref[:, lanes]
            li = lami_ref[:, lanes]
            hr = h0r_ref[:, lanes]
            hi = h0i_ref[:, lanes]
            for t in range(SUBLANES):
                rows_t = pl.ds(t, nseq, stride=SUBLANES)
                br = sr_ref[st, rows_t, :]
                bi = si_ref[st, rows_t, :]
                hr, hi = lr * hr - li * hi + br, lr * hi + li * hr + bi
                sr_ref[st, rows_t, :] = hr
                si_ref[st, rows_t, :] = hi
            hr_out_ref[:, lanes] = hr
            hi_out_ref[:, lanes] = hi
        load_r = lambda st: sr_ref[st]
        load_i = lambda st: si_ref[st]

    ys = []
    for b in range(kb):
        hr_b = jnp.concatenate([load_r(b * st_per_kb + j) for j in range(st_per_kb)], axis=1).astype(BF16)
        hi_b = jnp.concatenate([load_i(b * st_per_kb + j) for j in range(st_per_kb)], axis=1).astype(BF16)
        ys.append(_dot(hr_b, wcr_ref[b]) - _dot(hi_b, wci_ref[b]))
    y = jnp.concatenate(ys, axis=1) + dsk_ref[...] * u
    gl = _gelu(y).astype(BF16)
    val = _dot(gl, wv_ref[...])
    gate = _dot(gl, wg_ref[...])
    out = val * (1.0 / (1.0 + jnp.exp(-gate)))
    o_ref[...] = x + _rms(out, pw_ref[...])


def _s5(x, h0r, h0i, nw, wbr, wbi, lamr, lami, wcr, wci, dsk, wv, wg, pw, *, nb, seqlen):
    t, d = x.shape
    kb = wbr.shape[0]
    nst = kb * wbr.shape[2] // LANES
    long_seq = seqlen != SUBLANES
    if long_seq:
        tm = _row_tile(seqlen, 128)
        tps = seqlen // tm
        st_spec = pl.BlockSpec((1, nst, LANES), lambda i: (i // tps, 0, 0))
        st_shape = jax.ShapeDtypeStruct((nb, nst, LANES), F32)
        scratch = [pltpu.VMEM((tm * nst, LANES), F32), pltpu.VMEM((tm * nst, LANES), F32),
                   pltpu.VMEM((nst, LANES), F32), pltpu.VMEM((nst, LANES), F32)]
    else:
        tm = _row_tile(t, 256)
        tps = 1
        ns = tm // SUBLANES
        st_spec = pl.BlockSpec((ns, nst * LANES), lambda i: (i, 0))
        st_shape = jax.ShapeDtypeStruct((nb, nst * LANES), F32)
        scratch = [pltpu.VMEM((nst, tm, LANES), F32), pltpu.VMEM((nst, tm, LANES), F32),
                   pltpu.VMEM((SUBLANES, LANES), F32), pltpu.VMEM((SUBLANES, LANES), F32)]
    row = pl.BlockSpec((tm, d), lambda i: (i, 0))
    consts = [nw, wbr, wbi, lamr, lami, wcr, wci, dsk, wv, wg, pw]
    return pl.pallas_call(
        functools.partial(_s5_kernel, long_seq=long_seq, tiles_per_seq=tps, kb=kb, nst=nst),
        grid=(t // tm,),
        in_specs=[row, st_spec, st_spec] + [_full(a.shape) for a in consts],
        out_specs=[row, st_spec, st_spec],
        out_shape=[jax.ShapeDtypeStruct((t, d), F32), st_shape, st_shape],
        scratch_shapes=scratch,
        compiler_params=_cparams(),
        name="s5",
    )(x, h0r, h0i, *consts)


def _mla_proj_kernel(x_ref, cosq_ref, sinq_ref, cosk_ref, sink_ref, nw_ref, wdq_ref, qn_ref, wqn_ref,
                     wqr_ref, wqrs_ref, wukt_ref, wkv_ref, wkr_ref, wkrs_ref, kvn_ref,
                     qlat_ref, qrope_ref, ckv_ref, krope_ref, ckvb_ref, kropeb_ref, *, heads, nope, rope):
    u = _rms(x_ref[...], nw_ref[...]).astype(BF16)
    cq = _rms(_dot(u, wdq_ref[...]), qn_ref[...]).astype(BF16)
    qn = _dot(cq, wqn_ref[...])
    qr = _dot(cq, wqr_ref[...]) * cosq_ref[...] + _dot(cq, wqrs_ref[...]) * sinq_ref[...]
    for h in range(heads):
        qlat_ref[h] = _dot(qn[:, h * nope:(h + 1) * nope].astype(BF16), wukt_ref[h]).astype(qlat_ref.dtype)
        qrope_ref[h] = qr[:, h * rope:(h + 1) * rope].astype(qrope_ref.dtype)
    ckv = _rms(_dot(u, wkv_ref[...]), kvn_ref[...])
    kr = _dot(u, wkr_ref[...]) * cosk_ref[...] + _dot(u, wkrs_ref[...]) * sink_ref[...]
    ckv_ref[...] = ckv
    krope_ref[...] = kr
    ckvb_ref[...] = ckv.astype(BF16)
    kropeb_ref[...] = kr.astype(BF16)


def _mla_proj(x, tabs, nw, wdq, qn, wqn, wqr, wqrs, wukt, wkv, wkr, wkrs, kvn, *, pos_tiles, tm, qdtype):
    t, d = x.shape
    heads, nope, lora = wukt.shape
    rope = wkr.shape[1]
    cosq, sinq, cosk, sink = tabs
    row = lambda n: pl.BlockSpec((tm, n), lambda i: (i, 0))
    tab = lambda n: pl.BlockSpec((tm, n), lambda i: (i % pos_tiles, 0))
    consts = [nw, wdq, qn, wqn, wqr, wqrs, wukt, wkv, wkr, wkrs, kvn]
    return pl.pallas_call(
        functools.partial(_mla_proj_kernel, heads=heads, nope=nope, rope=rope),
        grid=(t // tm,),
        in_specs=[row(d), tab(heads * rope), tab(heads * rope), tab(rope), tab(rope)]
        + [_full(a.shape) for a in consts],
        out_specs=[pl.BlockSpec((heads, tm, lora), lambda i: (0, i, 0)),
                   pl.BlockSpec((heads, tm, rope), lambda i: (0, i, 0)),
                   row(lora), row(rope), row(lora), row(rope)],
        out_shape=[jax.ShapeDtypeStruct((heads, t, lora), qdtype), jax.ShapeDtypeStruct((heads, t, rope), qdtype),
                   jax.ShapeDtypeStruct((t, lora), F32), jax.ShapeDtypeStruct((t, rope), F32),
                   jax.ShapeDtypeStruct((t, lora), BF16), jax.ShapeDtypeStruct((t, rope), BF16)],
        compiler_params=_cparams(),
        name="mla_proj",
    )(x, cosq, sinq, cosk, sink, *consts)


def _softmax_step(s, v, m_ref, l_ref, acc_ref, idx):
    m_prev = m_ref[idx]
    m_new = jnp.maximum(m_prev, jnp.max(s, axis=-1, keepdims=True))
    alpha = jnp.exp(m_prev - m_new)
    p = jnp.exp(s - m_new)
    l_ref[idx] = alpha * l_ref[idx] + jnp.sum(p, axis=-1, keepdims=True)
    acc_ref[idx] = alpha * acc_ref[idx] + _dot(p.astype(BF16), v)
    m_ref[idx] = m_new


def _mla_prompt_kernel(ql_ref, qr_ref, kc_ref, kr_ref, o_ref, m_ref, l_ref, acc_ref, *, heads, tq, scale):
    qi = pl.program_id(1)
    m_ref[...] = jnp.full_like(m_ref, -jnp.inf)
    l_ref[...] = jnp.zeros_like(l_ref)
    acc_ref[...] = jnp.zeros_like(acc_ref)

    def block(ki, masked):
        off = pl.multiple_of(ki * tq, tq)
        kc = kc_ref[pl.ds(off, tq), :]
        kr = kr_ref[pl.ds(off, tq), :]
        for h in range(heads):
            s = (_dot_nt(ql_ref[h], kc) + _dot_nt(qr_ref[h], kr)) * scale
            if masked:
                keep = (lax.broadcasted_iota(jnp.int32, (tq, tq), 0) >= lax.broadcasted_iota(jnp.int32, (tq, tq), 1))
                s = jnp.where(keep, s, -jnp.inf)
            _softmax_step(s, kc, m_ref, l_ref, acc_ref, h)

    def body(ki, carry):
        block(ki, False)
        return carry

    lax.fori_loop(0, qi, body, 0)
    block(qi, True)
    for h in range(heads):
        o_ref[h] = (acc_ref[h] / l_ref[h]).astype(BF16)


def _mla_prompt_attn(qlat, qrope, ckvb, kropeb, *, nb, seqlen, scale):
    heads, t, lora = qlat.shape
    rope = qrope.shape[2]
    tq = _row_tile(seqlen, 256)
    nq = seqlen // tq
    return pl.pallas_call(
        functools.partial(_mla_prompt_kernel, heads=heads, tq=tq, scale=scale),
        grid=(nb, nq),
        in_specs=[pl.BlockSpec((heads, tq, lora), lambda b, i: (0, b * nq + i, 0)),
                  pl.BlockSpec((heads, tq, rope), lambda b, i: (0, b * nq + i, 0)),
                  pl.BlockSpec((seqlen, lora), lambda b, i: (b, 0)),
                  pl.BlockSpec((seqlen, rope), lambda b, i: (b, 0))],
        out_specs=pl.BlockSpec((heads, tq, lora), lambda b, i: (0, b * nq + i, 0)),
        out_shape=jax.ShapeDtypeStruct((heads, t, lora), BF16),
        scratch_shapes=[pltpu.VMEM((heads, tq, 1), F32), pltpu.VMEM((heads, tq, 1), F32),
                        pltpu.VMEM((heads, tq, lora), F32)],
        compiler_params=_cparams(2),
        name="mla_prompt_attn",
    )(qlat, qrope, ckvb, kropeb)


def _mla_sample_kernel(pt_ref, ql_ref, qr_ref, kc_ref, kr_ref, *rest, heads, pages, scale):
    lat_refs = rest[:pages]
    rope_refs = rest[pages:2 * pages]
    o_ref, m_ref, l_ref, acc_ref = rest[2 * pages:]
    j = pl.program_id(1)
    nj = pl.num_programs(1)
    rows = ql_ref.shape[0] * ql_ref.shape[1]
    ql = ql_ref[...].reshape(rows, ql_ref.shape[2]).astype(BF16)
    qr = qr_ref[...].reshape(rows, qr_ref.shape[2]).astype(BF16)

    @pl.when(j == 0)
    def _():
        m_ref[...] = jnp.full_like(m_ref, -jnp.inf)
        l_ref[...] = jnp.zeros_like(l_ref)
        acc_ref[...] = jnp.zeros_like(acc_ref)

    kc = jnp.concatenate([r[0, 0] for r in lat_refs], axis=0).astype(BF16)
    kr = jnp.concatenate([r[0, 0] for r in rope_refs], axis=0).astype(BF16)
    s = (_dot_nt(ql, kc) + _dot_nt(qr, kr)) * scale
    _softmax_step(s, kc, m_ref, l_ref, acc_ref, 0)

    @pl.when(j == nj - 1)
    def _():
        ntok = kc_ref.shape[0]
        pad = PAGE_SIZE - ntok
        kc_own = jnp.concatenate([kc_ref[...], jnp.zeros((pad, kc_ref.shape[1]), F32)], axis=0).astype(BF16)
        kr_own = jnp.concatenate([kr_ref[...], jnp.zeros((pad, kr_ref.shape[1]), F32)], axis=0).astype(BF16)
        s_own = (_dot_nt(ql, kc_own) + _dot_nt(qr, kr_own)) * scale
        q_tok = lax.broadcasted_iota(jnp.int32, s_own.shape, 0) % ntok
        k_tok = lax.broadcasted_iota(jnp.int32, s_own.shape, 1)
        s_own = jnp.where(k_tok <= q_tok, s_own, -jnp.inf)
        _softmax_step(s_own, kc_own, m_ref, l_ref, acc_ref, 0)
        o_ref[...] = (acc_ref[0] / l_ref[0]).reshape(o_ref.shape)


def _mla_sample_attn(qlat, qrope, ckv, krope, pool_lat, pool_rope, layer, page_table, *, seqlen, scale):
    heads, t, lora = qlat.shape
    rope = qrope.shape[2]
    nb, npages = page_table.shape
    pages = min(16, npages)
    assert npages % pages == 0 and seqlen == SUBLANES
    ngrp = npages // pages
    rows = heads * seqlen

    def page_spec(width, k):
        return pl.BlockSpec((1, 1, PAGE_SIZE, width), lambda b, j, pt: (layer, pt[b, j * pages + k], 0, 0))

    grid_spec = pltpu.PrefetchScalarGridSpec(
        num_scalar_prefetch=1,
        grid=(nb, ngrp),
        in_specs=[pl.BlockSpec((heads, seqlen, lora), lambda b, j, pt: (0, b, 0)),
                  pl.BlockSpec((heads, seqlen, rope), lambda b, j, pt: (0, b, 0)),
                  pl.BlockSpec((seqlen, lora), lambda b, j, pt: (b, 0)),
                  pl.BlockSpec((seqlen, rope), lambda b, j, pt: (b, 0))]
        + [page_spec(lora, k) for k in range(pages)] + [page_spec(rope, k) for k in range(pages)],
        out_specs=pl.BlockSpec((heads, seqlen, lora), lambda b, j, pt: (0, b, 0)),
        scratch_shapes=[pltpu.VMEM((1, rows, 1), F32), pltpu.VMEM((1, rows, 1), F32),
                        pltpu.VMEM((1, rows, lora), F32)],
    )
    return pl.pallas_call(
        functools.partial(_mla_sample_kernel, heads=heads, pages=pages, scale=scale),
        grid_spec=grid_spec,
        out_shape=jax.ShapeDtypeStruct((heads, t, lora), F32),
        compiler_params=_cparams(2),
        name="mla_sample_attn",
    )(page_table, qlat, qrope, ckv, krope, *([pool_lat] * pages), *([pool_rope] * pages))


def _mla_out_kernel(o_ref, x_ref, wuv_ref, wo_ref, pw_ref, out_ref, *, heads):
    parts = [_dot(o_ref[h].astype(BF16), wuv_ref[h]) for h in range(heads)]
    o = jnp.concatenate(parts, axis=1).astype(BF16)
    out_ref[...] = x_ref[...] + _rms(_dot(o, wo_ref[...]), pw_ref[...])


def _mla_out(o, x, wuv, wo, pw):
    heads, t, lora = o.shape
    d = x.shape[1]
    tm = _row_tile(t, 256)
    row = pl.BlockSpec((tm, d), lambda i: (i, 0))
    return pl.pallas_call(
        functools.partial(_mla_out_kernel, heads=heads),
        grid=(t // tm,),
        in_specs=[pl.BlockSpec((heads, tm, lora), lambda i: (0, i, 0)), row,
                  _full(wuv.shape), _full(wo.shape), _full(pw.shape)],
        out_specs=row,
        out_shape=jax.ShapeDtypeStruct((t, d), F32),
        compiler_params=_cparams(),
        name="mla_out",
    )(o, x, wuv, wo, pw)


def _pad_lanes(a, width=LANES):
    return jnp.pad(a, [(0, 0)] * (a.ndim - 1) + [(0, width - a.shape[-1])])


def _hist8(prev):
    return jnp.pad(prev, ((0, 0), (SUBLANES - prev.shape[1], 0), (0, 0)))


def _rope_tables(pos, rope, heads):
    half = rope // 2
    inv = ROPE_THETA ** (-jnp.arange(half, dtype=F32) / half)
    ang = pos.astype(F32)[:, None] * inv[None, :]
    cos, sin = jnp.cos(ang), jnp.sin(ang)
    cosk = jnp.concatenate([cos, cos], axis=1)
    sink = jnp.concatenate([-sin, sin], axis=1)
    return jnp.tile(cosk, (1, heads)), jnp.tile(sink, (1, heads)), cosk, sink


def _swap_halves(w, rope):
    k, n = w.shape
    w = w.reshape(k, n // rope, 2, rope // 2)
    return w[:, :, ::-1, :].reshape(k, n)


def _s5_params(a_re, a_im, log_step, b_re, b_im, c_re, c_im, d_model):
    g, s = a_re.shape
    grp = d_model // g
    delta = jnp.exp(log_step)[:, None]
    mag = jnp.exp(a_re * delta)
    ang = a_im * delta
    lam_re, lam_im = mag * jnp.cos(ang), mag * jnp.sin(ang)
    den = a_re * a_re + a_im * a_im
    nr, ni = lam_re - 1.0, lam_im
    f_re = (nr * a_re + ni * a_im) / den
    f_im = (ni * a_re - nr * a_im) / den
    bb_re = f_re[..., None] * b_re - f_im[..., None] * b_im
    bb_im = f_re[..., None] * b_im + f_im[..., None] * b_re
    kw = 256 if d_model % 256 == 0 else d_model
    kb = d_model // kw
    gpb = kw // grp

    def block_diag_in(bb):
        bb = bb.reshape(kb, gpb, s, grp)
        eye = jnp.eye(gpb, dtype=F32)
        w = jnp.einsum("kgsc,gh->kgchs", bb, eye)
        return w.reshape(kb, gpb * grp, gpb * s).astype(BF16)

    def block_diag_out(cc):
        cc = cc.reshape(kb, gpb, grp, s)
        eye = jnp.eye(gpb, dtype=F32)
        w = jnp.einsum("kgcs,gh->kgshc", cc, eye)
        return w.reshape(kb, gpb * s, gpb * grp).astype(BF16)

    nst = g * s // LANES
    return dict(wbr=block_diag_in(bb_re), wbi=block_diag_in(bb_im),
                wcr=block_diag_out(c_re), wci=block_diag_out(c_im),
                lam_re=lam_re.reshape(nst, LANES), lam_im=lam_im.reshape(nst, LANES))


def kernel(x_prompt, x_sample, state_ssd, state_ssd_conv, state_s5_re, state_s5_im, cache_mla_latent, cache_mla_krope, page_table, state_ffn_conv, norm_mix_pre, norm_mix_post, norm_ffn_pre, norm_ffn_post, ssd_w_in, ssd_conv_w, ssd_conv_b, ssd_dt_bias, ssd_a_log, ssd_d, ssd_norm, ssd_w_out, s5_a_re, s5_a_im, s5_log_step, s5_b_re, s5_b_im, s5_c_re, s5_c_im, s5_d, s5_w_val, s5_w_gate, mla_w_dq, mla_q_norm, mla_w_uq, mla_w_dkv, mla_kv_norm, mla_w_uk, mla_w_uv, mla_w_o, ffn_w_gate, ffn_w_up, ffn_conv_w, ffn_conv_b, ffn_w_down):
    b_p, l_p, d = x_prompt.shape
    b_s, l_s, _ = x_sample.shape
    depth = norm_mix_pre.shape[0]
    past_len = page_table.shape[1] * PAGE_SIZE
    groups_of = {"p": (b_p, l_p), "s": (b_s, l_s)}
    hid = {"p": x_prompt.reshape(b_p * l_p, d), "s": x_sample.reshape(b_s * l_s, d)}
    outs = {k: {n: [] for n in ("ssd_h", "ssd_c", "s5r", "s5i", "lat", "kr", "ffc")} for k in ("p", "s")}

    _, _, nh, hd, ds = state_ssd.shape
    di = nh * hd
    cd = state_ssd_conv.shape[-1]
    ssd_groups = (cd - di) // (2 * ds)
    s5_g, s5_s = s5_a_re.shape[1:]
    nst = s5_g * s5_s // LANES
    lora, heads, nope = mla_w_uk.shape[1:]
    rope = cache_mla_krope.shape[-1]
    mla_scale = 1.0 / math.sqrt(nope + rope)
    row1 = lambda a: a.reshape(1, -1)

    for i in range(depth):
        kind, j = i % N_MIXERS, i // N_MIXERS
        if kind == 0:
            w_in = ssd_w_in[j].astype(BF16)
            wz, wx, wdt = w_in[:, :di], w_in[:, di:di + cd], _pad_lanes(w_in[:, di + cd:])
            dtb, alog, dsk = (_pad_lanes(row1(a[j])) for a in (ssd_dt_bias, ssd_a_log, ssd_d))
            wo = ssd_w_out[j].astype(BF16)
            for k, (nb, sl) in groups_of.items():
                if k == "p":
                    prev8 = jnp.zeros((nb, SUBLANES, cd), F32)
                    h0 = jnp.zeros((nb, nh, hd, ds), F32)
                else:
                    prev8 = _hist8(state_ssd_conv[j])
                    h0 = state_ssd[j]
                z, xbc, dtr = _ssd_in(hid[k], row1(norm_mix_pre[i]), wz, wx, wdt)
                y, cnew, hfin = _ssd_scan(xbc, dtr, prev8, h0, ssd_conv_w[j], row1(ssd_conv_b[j]), dtb, alog, dsk,
                                          nb=nb, seqlen=sl, groups=ssd_groups)
                hid[k] = _ssd_out(y, z, hid[k], row1(ssd_norm[j]), wo, row1(norm_mix_post[i]), groups=ssd_groups)
                outs[k]["ssd_c"].append(cnew[:, SUBLANES - (SSD_CONV - 1):, :])
                outs[k]["ssd_h"].append(hfin)
        elif kind == 1:
            sp = _s5_params(s5_a_re[j], s5_a_im[j], s5_log_step[j], s5_b_re[j], s5_b_im[j], s5_c_re[j], s5_c_im[j], d)
            wv, wg = s5_w_val[j].astype(BF16), s5_w_gate[j].astype(BF16)
            for k, (nb, sl) in groups_of.items():
                if k == "p":
                    h0r = h0i = jnp.zeros((nb, nst, LANES), F32)
                    lamr, lami = sp["lam_re"], sp["lam_im"]
                else:
                    h0r, h0i = state_s5_re[j].reshape(nb, nst * LANES), state_s5_im[j].reshape(nb, nst * LANES)
                    lamr, lami = sp["lam_re"].reshape(1, -1), sp["lam_im"].reshape(1, -1)
                hid[k], hr, hi = _s5(hid[k], h0r, h0i, row1(norm_mix_pre[i]), sp["wbr"], sp["wbi"], lamr, lami,
                                     sp["wcr"], sp["wci"], row1(s5_d[j]), wv, wg, row1(norm_mix_post[i]),
                                     nb=nb, seqlen=sl)
                outs[k]["s5r"].append(hr.reshape(nb, s5_g, s5_s))
                outs[k]["s5i"].append(hi.reshape(nb, s5_g, s5_s))
        else:
            w_uq = mla_w_uq[j].reshape(-1, heads, nope + rope)
            wqn = w_uq[:, :, :nope].reshape(-1, heads * nope).astype(BF16)
            wqr = w_uq[:, :, nope:].reshape(-1, heads * rope).astype(BF16)
            wqrs = _swap_halves(wqr, rope)
            wkv = mla_w_dkv[j][:, :lora].astype(BF16)
            wkr = mla_w_dkv[j][:, lora:].astype(BF16)
            wkrs = _swap_halves(wkr, rope)
            wukt = jnp.transpose(mla_w_uk[j], (1, 2, 0)).astype(BF16)
            wuv = jnp.transpose(mla_w_uv[j], (1, 0, 2)).astype(BF16)
            wdq, wo = mla_w_dq[j].astype(BF16), mla_w_o[j].astype(BF16)
            for k, (nb, sl) in groups_of.items():
                t = nb * sl
                if k == "p":
                    tm = _row_tile(sl, 256)
                    tabs = _rope_tables(jnp.arange(sl), rope, heads)
                    pos_tiles = sl // tm
                else:
                    tm = _row_tile(t, 256)
                    tabs = _rope_tables(past_len + jnp.arange(tm) % sl, rope, heads)
                    pos_tiles = 1
                qlat, qrope, ckv, kr, ckvb, krb = _mla_proj(
                    hid[k], tabs, row1(norm_mix_pre[i]), wdq, row1(mla_q_norm[j]), wqn, wqr, wqrs, wukt, wkv, wkr, wkrs,
                    row1(mla_kv_norm[j]), pos_tiles=pos_tiles, tm=tm, qdtype=BF16 if k == "p" else F32)
                if k == "p":
                    o = _mla_prompt_attn(qlat, qrope, ckvb, krb, nb=nb, seqlen=sl, scale=mla_scale)
                else:
                    o = _mla_sample_attn(qlat, qrope, ckv, kr, cache_mla_latent, cache_mla_krope, j, page_table,
                                         seqlen=sl, scale=mla_scale)
                hid[k] = _mla_out(o, hid[k], wuv, wo, row1(norm_mix_post[i]))
                outs[k]["lat"].append(ckv.reshape(nb, sl, lora))
                outs[k]["kr"].append(kr.reshape(nb, sl, rope))
        wg, wu, wd = ffn_w_gate[i].astype(BF16), ffn_w_up[i].astype(BF16), ffn_w_down[i].astype(BF16)
        f = wg.shape[1]
        for k, (nb, sl) in groups_of.items():
            if k == "p":
                prev8 = jnp.zeros((nb * SUBLANES, f), F32)
            else:
                prev8 = _hist8(state_ffn_conv[i]).reshape(nb * SUBLANES, f)
            hid[k], cnew = _ffn(hid[k], prev8, row1(norm_ffn_pre[i]), wg, wu, ffn_conv_w[i], row1(ffn_conv_b[i]), wd,
                                row1(norm_ffn_post[i]), nb=nb, seqlen=sl)
            outs[k]["ffc"].append(cnew.reshape(nb, SUBLANES, f)[:, SUBLANES - (FFN_CONV - 1):, :])

    res = [hid["p"].reshape(b_p, l_p, d), hid["s"].reshape(b_s, l_s, d)]
    for k in ("p", "s"):
        res += [jnp.stack(outs[k][n]) for n in ("ssd_h", "ssd_c", "s5r", "s5i", "lat", "kr", "ffc")]
    return tuple(res)
```

```python
import functools
import math

import jax
import jax.numpy as jnp
from jax import lax
from jax.experimental import pallas as pl
from jax.experimental.pallas import tpu as pltpu

F32 = jnp.float32
BF16 = jnp.bfloat16
RMS_EPS = 1e-6
ROPE_THETA = 10000.0
LOG2E = math.log2(math.e)
PAGE_SIZE = 128
N_MIXERS = 3

SUBLANES = 8
LANES = 128
VMEM_LIMIT_BYTES = 56 * 1024 * 1024

SSD_CONV = 4
FFN_CONV = 3
CONV_CHUNK = 512


def _cparams(n_axes=1):
    return pltpu.CompilerParams(dimension_semantics=("arbitrary",) * n_axes,
                                vmem_limit_bytes=VMEM_LIMIT_BYTES)


def _full(shape):
    nd = len(shape)
    return pl.BlockSpec(shape, lambda i: (0,) * nd)


def _row_tile(n, pref):
    t = min(n, pref)
    while n % t or t % SUBLANES:
        t -= SUBLANES
    assert t > 0
    return t


def _rms(x, w):
    return x * lax.rsqrt(jnp.mean(x * x, axis=-1, keepdims=True) + RMS_EPS) * w


def _sigmoid(x):
    return 0.5 * jnp.tanh(0.5 * x) + 0.5


def _silu(x):
    h = 0.5 * x
    return h * jnp.tanh(h) + h


def _softplus(x):
    return jnp.maximum(x, 0.0) + jnp.log1p(jnp.exp(-jnp.abs(x)))


def _dot(a, b):
    return jnp.dot(a, b, preferred_element_type=F32)


def _dot_nt(a, b):
    return lax.dot_general(a, b, (((1,), (1,)), ((), ())), preferred_element_type=F32)


def _dot_tn(a, b):
    return lax.dot_general(a, b, (((0,), (0,)), ((), ())), preferred_element_type=F32)


def _shifted_rows(x, prev8, s, seg):
    if s == 0:
        return x
    rows = x.shape[0]
    rolled = pltpu.roll(x, s, axis=0)
    if seg == rows:
        head = jnp.where(lax.broadcasted_iota(jnp.int32, (SUBLANES, x.shape[1]), 0) < s,
                         pltpu.roll(prev8, s, axis=0), rolled[:SUBLANES])
        if rows == SUBLANES:
            return head
        return jnp.concatenate([head, rolled[SUBLANES:]], axis=0)
    assert seg == SUBLANES
    hist = pltpu.roll(prev8, (rows + s - SUBLANES) % rows, axis=0) if rows > SUBLANES else pltpu.roll(prev8, s, axis=0)
    t = lax.broadcasted_iota(jnp.int32, x.shape, 0) % SUBLANES
    return jnp.where(t < s, hist, rolled)


def _causal_conv(x, prev8, w, b, seg):
    k = w.shape[0]
    acc = b
    for j in range(k):
        acc = acc + _shifted_rows(x, prev8, k - 1 - j, seg) * w[j:j + 1, :]
    return acc


def _carried_conv(hist_ref, cols, x, w, b):
    acc = _causal_conv(x, hist_ref[:, cols], w, b, x.shape[0])
    hist_ref[:, cols] = x[x.shape[0] - SUBLANES:, :]
    return acc


def _col_chunks(n, width):
    return [slice(lo, min(lo + width, n)) for lo in range(0, n, width)]


def _ssd_in_kernel(x_ref, prev_ref, nw_ref, wz_ref, wx_ref, wdt_ref, cw_ref, cb_ref, dtb_ref,
                   z_ref, xact_ref, dt_ref, convnew_ref, hist_ref, *, seg, tiles_per_seq):
    tm = x_ref.shape[0]
    u = _rms(x_ref[...], nw_ref[...]).astype(BF16)
    if seg == tm:
        @pl.when(pl.program_id(0) % tiles_per_seq == 0)
        def _():
            hist_ref[...] = prev_ref[...]

    for cols in _col_chunks(wx_ref.shape[1], CONV_CHUNK):
        xbc = _dot(u, wx_ref[:, cols])
        if seg == tm:
            conv = _carried_conv(hist_ref, cols, xbc, cw_ref[:, cols], cb_ref[:, cols])
            convnew_ref[:, cols] = xbc[tm - SUBLANES:, :]
        else:
            conv = _causal_conv(xbc, prev_ref[:, cols], cw_ref[:, cols], cb_ref[:, cols], seg)
            convnew_ref[:, cols] = xbc
        xact_ref[:, cols] = _silu(conv)
    z_ref[...] = _dot(u, wz_ref[...])
    dt_ref[...] = _softplus(_dot(u, wdt_ref[...]) + dtb_ref[...])


def _ssd_in(x, prev8, nw, wz, wx, wdt, cw, cb, dtb, *, nb, seqlen):
    t, d = x.shape
    di, cd, dp = wz.shape[1], wx.shape[1], wdt.shape[1]
    if seqlen == SUBLANES:
        tm = _row_tile(t, 256)
        seg, tps = SUBLANES, 1
        hist_spec = pl.BlockSpec((tm, cd), lambda i: (i, 0))
    else:
        tm = _row_tile(seqlen, 256)
        seg, tps = tm, seqlen // tm
        hist_spec = pl.BlockSpec((SUBLANES, cd), lambda i: (i // tps, 0))
    row = lambda n: pl.BlockSpec((tm, n), lambda i: (i, 0))
    consts = [nw, wz, wx, wdt, cw, cb, dtb]
    return pl.pallas_call(
        functools.partial(_ssd_in_kernel, seg=seg, tiles_per_seq=tps),
        grid=(t // tm,),
        in_specs=[row(d), hist_spec] + [_full(a.shape) for a in consts],
        out_specs=[row(di), row(cd), row(dp), hist_spec],
        out_shape=[jax.ShapeDtypeStruct((t, di), F32), jax.ShapeDtypeStruct((t, cd), F32),
                   jax.ShapeDtypeStruct((t, dp), F32), jax.ShapeDtypeStruct((nb * SUBLANES, cd), F32)],
        scratch_shapes=[pltpu.VMEM((SUBLANES, cd), F32)],
        compiler_params=_cparams(),
        name="ssd_in",
    )(x, prev8, *consts)


def _cumsum_rows(x):
    rows = x.shape[0]
    row = lax.broadcasted_iota(jnp.int32, x.shape, 0)
    k = 1
    while k < rows:
        x = x + jnp.where(row >= k, pltpu.roll(x, k, axis=0), 0.0)
        k *= 2
    return x


def _ssd_scan_kernel(xact_ref, dt_ref, h0_ref, alog_ref, dsk_ref, y_ref, hfin_ref, xpad_ref, dtpad_ref, h_ref,
                     *, lt, q, groups, hpg, hd, ds, nc):
    c = pl.program_id(1)
    di = groups * hpg * hd
    gn = groups * ds

    @pl.when(c == 0)
    def _():
        h_ref[...] = h0_ref[0, 0]

    if lt < q:
        xpad_ref[...] = jnp.zeros_like(xpad_ref)
        xpad_ref[0:lt, :] = xact_ref[...]
        dtpad_ref[...] = jnp.zeros_like(dtpad_ref)
        dtpad_ref[0:lt, :] = dt_ref[...]
        xact = xpad_ref[...]
        dt = dtpad_ref[...]
    else:
        xact = xact_ref[...]
        dt = dt_ref[...]
    xs = xact[:, :di]
    bm = xact[:, di:di + gn]
    cm = xact[:, di + gn:]
    a = -jnp.exp(alog_ref[...])
    cs = _cumsum_rows(dt * a) * LOG2E
    cs_last = cs[q - 1:q, :]
    ecs = jnp.exp2(cs)
    e_last = jnp.exp2(cs_last)
    cs_t = cs.T
    dt_t = dt.T
    w_t = (jnp.exp2(cs_last - cs) * dt).T
    causal = (lax.broadcasted_iota(jnp.int32, (q, q), 0) >= lax.broadcasted_iota(jnp.int32, (q, q), 1))
    even = lax.broadcasted_iota(jnp.int32, (q, 2 * hd), 1) < hd
    assert 2 * hd == LANES and hpg % 2 == 0
    y_pairs = []
    for g in range(groups):
        bg = bm[:, g * ds:(g + 1) * ds].astype(BF16)
        cg = cm[:, g * ds:(g + 1) * ds].astype(BF16)
        cb = _dot_nt(cg, bg)
        hg = h_ref[g * hpg:(g + 1) * hpg].reshape(hpg * hd, ds)
        yoff = _dot_nt(cg, hg.astype(BF16))
        xs_g = xs[:, g * hpg * hd:(g + 1) * hpg * hd]
        for k in range(hpg // 2):
            pair = xs_g[:, k * LANES:(k + 1) * LANES].astype(BF16)
            yd, ec = [], []
            for h in (g * hpg + 2 * k, g * hpg + 2 * k + 1):
                seg = cs[:, h:h + 1] - cs_t[h:h + 1, :]
                m = cb * jnp.exp2(jnp.where(causal, seg, -jnp.inf)) * dt_t[h:h + 1, :]
                yd.append(_dot(m.astype(BF16), pair))
                ec.append(ecs[:, h:h + 1])
            y_pairs.append(jnp.where(even, yd[0], yd[1])
                           + yoff[:, k * LANES:(k + 1) * LANES] * jnp.where(even, ec[0], ec[1]))
        xs_gt = xs_g.T
        xw = [xs_gt[r * hd:(r + 1) * hd, :] * w_t[g * hpg + r:g * hpg + r + 1, :] for r in range(hpg)]
        s_g = _dot(jnp.concatenate(xw, axis=0).astype(BF16), bg)
        for r in range(hpg):
            h = g * hpg + r
            h_ref[h] = h_ref[h] * e_last[:, h:h + 1] + s_g[r * hd:(r + 1) * hd, :]
    y = jnp.concatenate(y_pairs, axis=1) + dsk_ref[...] * xs
    y_ref[...] = y[0:lt, :]

    @pl.when(c == nc - 1)
    def _():
        hfin_ref[0] = h_ref[...]


def _ssd_scan(xact, dt, h0, layer, alog, dsk, *, nb, seqlen, groups):
    _, _, nh, hd, ds = h0.shape
    cd = xact.shape[1]
    hpg = nh // groups
    di = nh * hd
    if seqlen >= 256:
        lt = q = 256
    else:
        lt, q = seqlen, LANES
    assert seqlen % lt == 0 and lt % SUBLANES == 0
    nc = seqlen // lt
    pad_rows = q if lt < q else SUBLANES
    kern = functools.partial(_ssd_scan_kernel, lt=lt, q=q, groups=groups, hpg=hpg, hd=hd, ds=ds, nc=nc)
    row = lambda n: pl.BlockSpec((lt, n), lambda b, c: (b * nc + c, 0))
    par = lambda a: pl.BlockSpec(a.shape, lambda b, c: (0,) * a.ndim)
    return pl.pallas_call(
        kern,
        grid=(nb, nc),
        in_specs=[row(cd), row(dt.shape[1]),
                  pl.BlockSpec((1, 1, nh, hd, ds), lambda b, c: (layer, b, 0, 0, 0)),
                  par(alog), par(dsk)],
        out_specs=[row(di), pl.BlockSpec((1, nh, hd, ds), lambda b, c: (b, 0, 0, 0))],
        out_shape=[jax.ShapeDtypeStruct((nb * seqlen, di), F32),
                   jax.ShapeDtypeStruct((nb, nh, hd, ds), F32)],
        scratch_shapes=[pltpu.VMEM((pad_rows, cd), F32), pltpu.VMEM((pad_rows, dt.shape[1]), F32),
                        pltpu.VMEM((nh, hd, ds), F32)],
        compiler_params=_cparams(2),
        name="ssd_scan",
    )(xact, dt, h0, alog, dsk)


def _ssd_out_kernel(y_ref, z_ref, x_ref, gw_ref, wo_ref, pw_ref, o_ref, *, groups):
    yg = y_ref[...] * _silu(z_ref[...])
    gs = yg.shape[1] // groups
    parts = []
    for g in range(groups):
        blk = yg[:, g * gs:(g + 1) * gs]
        parts.append(blk * lax.rsqrt(jnp.mean(blk * blk, axis=-1, keepdims=True) + RMS_EPS))
    yn = (jnp.concatenate(parts, axis=1) * gw_ref[...]).astype(BF16)
    o_ref[...] = x_ref[...] + _rms(_dot(yn, wo_ref[...]), pw_ref[...])


def _ssd_out(y, z, x, gw, wo, pw, *, groups):
    t, d = x.shape
    tm = _row_tile(t, 256)
    row = lambda n: pl.BlockSpec((tm, n), lambda i: (i, 0))
    return pl.pallas_call(
        functools.partial(_ssd_out_kernel, groups=groups),
        grid=(t // tm,),
        in_specs=[row(y.shape[1]), row(z.shape[1]), row(d), _full(gw.shape), _full(wo.shape), _full(pw.shape)],
        out_specs=row(d),
        out_shape=jax.ShapeDtypeStruct((t, d), F32),
        compiler_params=_cparams(),
        name="ssd_out",
    )(y, z, x, gw, wo, pw)


def _ffn_kernel(x_ref, prev_ref, nw_ref, wg_ref, wu_ref, cw_ref, cb_ref, wd_ref, pw_ref,
                o_ref, convnew_ref, hist_ref, *, seg, tiles_per_seq):
    x = x_ref[...]
    tm = x.shape[0]
    u = _rms(x, nw_ref[...]).astype(BF16)
    if seg == tm:
        @pl.when(pl.program_id(0) % tiles_per_seq == 0)
        def _():
            hist_ref[...] = prev_ref[...]

    out = None
    for cols in _col_chunks(wg_ref.shape[1], CONV_CHUNK):
        g = _dot(u, wg_ref[:, cols])
        up = _dot(u, wu_ref[:, cols])
        if seg == tm:
            gc = _carried_conv(hist_ref, cols, g, cw_ref[:, cols], cb_ref[:, cols])
            convnew_ref[:, cols] = g[tm - SUBLANES:, :]
        else:
            gc = _causal_conv(g, prev_ref[:, cols], cw_ref[:, cols], cb_ref[:, cols], seg)
            convnew_ref[:, cols] = g
        part = _dot((_silu(gc) * up).astype(BF16), wd_ref[cols, :])
        out = part if out is None else out + part
    o_ref[...] = x + _rms(out, pw_ref[...])


def _ffn(x, prev8, nw, wg, wu, cw, cb, wd, pw, *, nb, seqlen):
    t, d = x.shape
    f = wg.shape[1]
    if seqlen == SUBLANES:
        tm = _row_tile(t, 256)
        seg, tps = SUBLANES, 1
        prev_spec = pl.BlockSpec((tm, f), lambda i: (i, 0))
        new_spec = pl.BlockSpec((tm, f), lambda i: (i, 0))
    else:
        tm = _row_tile(seqlen, 256)
        seg, tps = tm, seqlen // tm
        prev_spec = pl.BlockSpec((SUBLANES, f), lambda i: (i // tps, 0))
        new_spec = pl.BlockSpec((SUBLANES, f), lambda i: (i // tps, 0))
    row = pl.BlockSpec((tm, d), lambda i: (i, 0))
    return pl.pallas_call(
        functools.partial(_ffn_kernel, seg=seg, tiles_per_seq=tps),
        grid=(t // tm,),
        in_specs=[row, prev_spec, _full(nw.shape), _full(wg.shape), _full(wu.shape), _full(cw.shape),
                  _full(cb.shape), _full(wd.shape), _full(pw.shape)],
        out_specs=[row, new_spec],
        out_shape=[jax.ShapeDtypeStruct((t, d), F32), jax.ShapeDtypeStruct((nb * SUBLANES, f), F32)],
        scratch_shapes=[pltpu.VMEM((SUBLANES, f), F32)],
        compiler_params=_cparams(),
        name="conv_ffn",
    )(x, prev8, nw, wg, wu, cw, cb, wd, pw)


def _gelu(x):
    return 0.5 * x * (1.0 + lax.erf(x * (1.0 / math.sqrt(2.0))))


def _s5_token_pitch(nst):
    groups = -(-nst // SUBLANES)
    return SUBLANES * (groups + 1 - groups % 2)


def _s5_kernel(x_ref, h0r_ref, h0i_ref, nw_ref, wbr_ref, wbi_ref, lamr_ref, lami_ref, wcr_ref, wci_ref,
               dsk_ref, wv_ref, wg_ref, pw_ref, o_ref, hr_out_ref, hi_out_ref, sr_ref, si_ref, cr_ref, ci_ref,
               *, long_seq, tiles_per_seq, kb, nst):
    x = x_ref[...]
    tm, d = x.shape
    u = _rms(x, nw_ref[...])
    ub = u.astype(BF16)
    kw = d // kb
    st_per_kb = nst // kb
    if long_seq:
        i = pl.program_id(0)

        @pl.when(i % tiles_per_seq == 0)
        def _():
            cr_ref[...] = h0r_ref[0]
            ci_ref[...] = h0i_ref[0]

        pitch = _s5_token_pitch(nst)
        for b in range(kb):
            blk = ub[:, b * kw:(b + 1) * kw]
            pr = _dot(blk, wbr_ref[b])
            pi = _dot(blk, wbi_ref[b])
            for j in range(st_per_kb):
                st = b * st_per_kb + j
                sr_ref[pl.ds(st, tm, stride=pitch), :] = pr[:, j * LANES:(j + 1) * LANES]
                si_ref[pl.ds(st, tm, stride=pitch), :] = pi[:, j * LANES:(j + 1) * LANES]
        lr = lamr_ref[...]
        li = lami_ref[...]

        def step(t, carry):
            hr, hi = carry
            off = pl.multiple_of(t * pitch, SUBLANES)
            br = sr_ref[pl.ds(off, nst), :]
            bi = si_ref[pl.ds(off, nst), :]
            nr = lr * hr - li * hi + br
            ni = lr * hi + li * hr + bi
            sr_ref[pl.ds(off, nst), :] = nr
            si_ref[pl.ds(off, nst), :] = ni
            return nr, ni

        hr, hi = lax.fori_loop(0, tm, step, (cr_ref[...], ci_ref[...]), unroll=8)
        cr_ref[...] = hr
        ci_ref[...] = hi
        hr_out_ref[0] = hr
        hi_out_ref[0] = hi
        load_r = lambda st: sr_ref[pl.ds(st, tm, stride=pitch), :]
        load_i = lambda st: si_ref[pl.ds(st, tm, stride=pitch), :]
    else:
        nseq = tm // SUBLANES
        for b in range(kb):
            blk = ub[:, b * kw:(b + 1) * kw]
            pr = _dot(blk, wbr_ref[b])
            pi = _dot(blk, wbi_ref[b])
            for j in range(st_per_kb):
                sr_ref[b * st_per_kb + j] = pr[:, j * LANES:(j + 1) * LANES]
                si_ref[b * st_per_kb + j] = pi[:, j * LANES:(j + 1) * LANES]
        for st in range(nst):
            lanes = slice(st * LANES, (st + 1) * LANES)
            lr = lamr_ref[:, lanes]
            li = lami_ref[:, lanes]
            hr = h0r_ref[:, lanes]
            hi = h0i_ref[:, lanes]
            for t in range(SUBLANES):
                rows_t = pl.ds(t, nseq, stride=SUBLANES)
                br = sr_ref[st, rows_t, :]
                bi = si_ref[st, rows_t, :]
                hr, hi = lr * hr - li * hi + br, lr * hi + li * hr + bi
                sr_ref[st, rows_t, :] = hr
                si_ref[st, rows_t, :] = hi
            hr_out_ref[:, lanes] = hr
            hi_out_ref[:, lanes] = hi
        load_r = lambda st: sr_ref[st]
        load_i = lambda st: si_ref[st]

    ys = []
    for b in range(kb):
        hr_b = jnp.concatenate([load_r(b * st_per_kb + j) for j in range(st_per_kb)], axis=1).astype(BF16)
        hi_b = jnp.concatenate([load_i(b * st_per_kb + j) for j in range(st_per_kb)], axis=1).astype(BF16)
        ys.append(_dot(hr_b, wcr_ref[b]) - _dot(hi_b, wci_ref[b]))
    y = jnp.concatenate(ys, axis=1) + dsk_ref[...] * u
    gl = _gelu(y).astype(BF16)
    val = _dot(gl, wv_ref[...])
    gate = _dot(gl, wg_ref[...])
    out = val * _sigmoid(gate)
    o_ref[...] = x + _rms(out, pw_ref[...])


def _s5(x, h0r, h0i, nw, wbr, wbi, lamr, lami, wcr, wci, dsk, wv, wg, pw, *, nb, seqlen):
    t, d = x.shape
    kb = wbr.shape[0]
    nst = kb * wbr.shape[2] // LANES
    long_seq = seqlen != SUBLANES
    if long_seq:
        tm = _row_tile(seqlen, 128)
        tps = seqlen // tm
        st_spec = pl.BlockSpec((1, nst, LANES), lambda i: (i // tps, 0, 0))
        st_shape = jax.ShapeDtypeStruct((nb, nst, LANES), F32)
        rows = tm * _s5_token_pitch(nst)
        scratch = [pltpu.VMEM((rows, LANES), F32), pltpu.VMEM((rows, LANES), F32),
                   pltpu.VMEM((nst, LANES), F32), pltpu.VMEM((nst, LANES), F32)]
    else:
        tm = _row_tile(t, 256)
        tps = 1
        ns = tm // SUBLANES
        st_spec = pl.BlockSpec((ns, nst * LANES), lambda i: (i, 0))
        st_shape = jax.ShapeDtypeStruct((nb, nst * LANES), F32)
        scratch = [pltpu.VMEM((nst, tm, LANES), F32), pltpu.VMEM((nst, tm, LANES), F32),
                   pltpu.VMEM((SUBLANES, LANES), F32), pltpu.VMEM((SUBLANES, LANES), F32)]
    row = pl.BlockSpec((tm, d), lambda i: (i, 0))
    consts = [nw, wbr, wbi, lamr, lami, wcr, wci, dsk, wv, wg, pw]
    return pl.pallas_call(
        functools.partial(_s5_kernel, long_seq=long_seq, tiles_per_seq=tps, kb=kb, nst=nst),
        grid=(t // tm,),
        in_specs=[row, st_spec, st_spec] + [_full(a.shape) for a in consts],
        out_specs=[row, st_spec, st_spec],
        out_shape=[jax.ShapeDtypeStruct((t, d), F32), st_shape, st_shape],
        scratch_shapes=scratch,
        compiler_params=_cparams(),
        name="s5",
    )(x, h0r, h0i, *consts)


def _mla_proj_kernel(x_ref, cosq_ref, sinq_ref, cosk_ref, sink_ref, nw_ref, wdq_ref, qn_ref, wqn_ref,
                     wqr_ref, wqrs_ref, wukt_ref, wkv_ref, wkr_ref, wkrs_ref, kvn_ref,
                     qlat_ref, qrope_ref, ckv_ref, krope_ref, *key_refs, heads, nope, rope):
    u = _rms(x_ref[...], nw_ref[...]).astype(BF16)
    cq = _rms(_dot(u, wdq_ref[...]), qn_ref[...]).astype(BF16)
    qn = _dot(cq, wqn_ref[...])
    qr = _dot(cq, wqr_ref[...]) * cosq_ref[...] + _dot(cq, wqrs_ref[...]) * sinq_ref[...]
    ckv = _rms(_dot(u, wkv_ref[...]), kvn_ref[...])
    kr = _dot(u, wkr_ref[...]) * cosk_ref[...] + _dot(u, wkrs_ref[...]) * sink_ref[...]
    ckv_ref[...] = ckv
    krope_ref[...] = kr
    for h in range(heads):
        qlat_ref[h] = _dot(qn[:, h * nope:(h + 1) * nope].astype(BF16), wukt_ref[h]).astype(qlat_ref.dtype)
        qrope_ref[h] = qr[:, h * rope:(h + 1) * rope].astype(qrope_ref.dtype)
    if key_refs:
        ckvb_ref, kropeb_ref, ckvt_ref = key_refs
        ckvb_ref[...] = ckv.astype(BF16)
        kropeb_ref[...] = kr.astype(BF16)
        ckvt_ref[0] = ckv.T.astype(BF16)


def _mla_proj(x, tabs, nw, wdq, qn, wqn, wqr, wqrs, wukt, wkv, wkr, wkrs, kvn, *, pos_tiles, tm, for_prompt):
    t, d = x.shape
    heads, nope, lora = wukt.shape
    rope = wkr.shape[1]
    cosq, sinq, cosk, sink = tabs
    row = lambda n: pl.BlockSpec((tm, n), lambda i: (i, 0))
    tab = lambda n: pl.BlockSpec((tm, n), lambda i: (i % pos_tiles, 0))
    consts = [nw, wdq, qn, wqn, wqr, wqrs, wukt, wkv, wkr, wkrs, kvn]
    qdtype = BF16 if for_prompt else F32
    out_specs = [pl.BlockSpec((heads, tm, lora), lambda i: (0, i, 0)),
                 pl.BlockSpec((heads, tm, rope), lambda i: (0, i, 0)), row(lora), row(rope)]
    out_shape = [jax.ShapeDtypeStruct((heads, t, lora), qdtype), jax.ShapeDtypeStruct((heads, t, rope), qdtype),
                 jax.ShapeDtypeStruct((t, lora), F32), jax.ShapeDtypeStruct((t, rope), F32)]
    if for_prompt:
        out_specs += [row(lora), row(rope), pl.BlockSpec((1, lora, tm), lambda i: (i, 0, 0))]
        out_shape += [jax.ShapeDtypeStruct((t, lora), BF16), jax.ShapeDtypeStruct((t, rope), BF16),
                      jax.ShapeDtypeStruct((t // tm, lora, tm), BF16)]
    return pl.pallas_call(
        functools.partial(_mla_proj_kernel, heads=heads, nope=nope, rope=rope),
        grid=(t // tm,),
        in_specs=[row(d), tab(heads * rope), tab(heads * rope), tab(rope), tab(rope)]
        + [_full(a.shape) for a in consts],
        out_specs=out_specs,
        out_shape=out_shape,
        compiler_params=_cparams(),
        name="mla_proj",
    )(x, cosq, sinq, cosk, sink, *consts)


def _softmax_step(s, v, m_ref, l_ref, acc_ref, idx):
    m_prev = m_ref[idx]
    m_new = jnp.maximum(m_prev, jnp.max(s, axis=-1, keepdims=True))
    alpha = jnp.exp(m_prev - m_new)
    p = jnp.exp(s - m_new)
    l_ref[idx] = alpha * l_ref[idx] + jnp.sum(p, axis=-1, keepdims=True)
    acc_ref[idx] = alpha * acc_ref[idx] + _dot(p.astype(BF16), v)
    m_ref[idx] = m_new


def _mla_prompt_kernel(ql_ref, qr_ref, kc_ref, kr_ref, vt_ref, o_ref, m_ref, l_ref, acc_ref, *, heads, tq, scale):
    qi = pl.program_id(1)
    m_ref[...] = jnp.full_like(m_ref, -jnp.inf)
    l_ref[...] = jnp.zeros_like(l_ref)
    acc_ref[...] = jnp.zeros_like(acc_ref)
    scale2 = scale * math.log2(math.e)

    nqh = heads * tq
    ql = ql_ref[...].reshape(nqh, ql_ref.shape[2])
    qr = qr_ref[...].reshape(nqh, qr_ref.shape[2])

    def block(ki, masked):
        off = pl.multiple_of(ki * tq, tq)
        kc = kc_ref[pl.ds(off, tq), :]
        kr = kr_ref[pl.ds(off, tq), :]
        s = (_dot_nt(kc, ql) + _dot_nt(kr, qr)) * scale2
        if masked:
            key = lax.broadcasted_iota(jnp.int32, (tq, nqh), 0)
            qry = lax.broadcasted_iota(jnp.int32, (tq, nqh), 1) % tq
            s = jnp.where(key <= qry, s, -jnp.inf)
        m_prev = m_ref[...]
        m_new = jnp.maximum(m_prev, jnp.max(s, axis=0, keepdims=True))
        alpha = jnp.exp2(m_prev - m_new)
        p = jnp.exp2(s - m_new)
        l_ref[...] = alpha * l_ref[...] + jnp.sum(p, axis=0, keepdims=True)
        acc_ref[...] = alpha * acc_ref[...] + _dot(vt_ref[ki], p.astype(BF16))
        m_ref[...] = m_new

    def body(i, carry):
        block(2 * i, False)
        block(2 * i + 1, False)
        return carry

    lax.fori_loop(0, qi // 2, body, 0)

    @pl.when(qi % 2 == 1)
    def _():
        block(qi - 1, False)

    block(qi, True)
    o_t = acc_ref[...] / l_ref[...]
    for h in range(heads):
        o_ref[h] = o_t[:, h * tq:(h + 1) * tq].T.astype(BF16)


def _mla_prompt_attn(qlat, qrope, ckvb, kropeb, ckv_t, *, nb, seqlen, scale):
    heads, t, lora = qlat.shape
    rope = qrope.shape[2]
    tq = ckv_t.shape[2]
    nq = seqlen // tq
    return pl.pallas_call(
        functools.partial(_mla_prompt_kernel, heads=heads, tq=tq, scale=scale),
        grid=(nb, nq),
        in_specs=[pl.BlockSpec((heads, tq, lora), lambda b, i: (0, b * nq + i, 0)),
                  pl.BlockSpec((heads, tq, rope), lambda b, i: (0, b * nq + i, 0)),
                  pl.BlockSpec((seqlen, lora), lambda b, i: (b, 0)),
                  pl.BlockSpec((seqlen, rope), lambda b, i: (b, 0)),
                  pl.BlockSpec((nq, lora, tq), lambda b, i: (b, 0, 0))],
        out_specs=pl.BlockSpec((heads, tq, lora), lambda b, i: (0, b * nq + i, 0)),
        out_shape=jax.ShapeDtypeStruct((heads, t, lora), BF16),
        scratch_shapes=[pltpu.VMEM((1, heads * tq), F32), pltpu.VMEM((1, heads * tq), F32),
                        pltpu.VMEM((lora, heads * tq), F32)],
        compiler_params=_cparams(2),
        name="mla_prompt_attn",
    )(qlat, qrope, ckvb, kropeb, ckv_t)


def _mla_sample_kernel(pt_ref, ql_ref, qr_ref, kc_ref, kr_ref, *rest, heads, pages, scale):
    lat_refs = rest[:pages]
    rope_refs = rest[pages:2 * pages]
    o_ref, m_ref, l_ref, acc_ref = rest[2 * pages:]
    j = pl.program_id(1)
    nj = pl.num_programs(1)
    rows = ql_ref.shape[0] * ql_ref.shape[1]
    ql = ql_ref[...].reshape(rows, ql_ref.shape[2]).astype(BF16)
    qr = qr_ref[...].reshape(rows, qr_ref.shape[2]).astype(BF16)

    @pl.when(j == 0)
    def _():
        m_ref[...] = jnp.full_like(m_ref, -jnp.inf)
        l_ref[...] = jnp.zeros_like(l_ref)
        acc_ref[...] = jnp.zeros_like(acc_ref)

    kc = jnp.concatenate([r[0, 0] for r in lat_refs], axis=0).astype(BF16)
    kr_t = jnp.concatenate([r[0, 0] for r in rope_refs], axis=1).astype(BF16)
    s = (_dot_nt(ql, kc) + _dot(qr, kr_t)) * scale
    _softmax_step(s, kc, m_ref, l_ref, acc_ref, 0)

    @pl.when(j == nj - 1)
    def _():
        ntok = kc_ref.shape[0]
        pad = PAGE_SIZE - ntok
        kc_own = jnp.concatenate([kc_ref[...], jnp.zeros((pad, kc_ref.shape[1]), F32)], axis=0).astype(BF16)
        kr_own = jnp.concatenate([kr_ref[...], jnp.zeros((pad, kr_ref.shape[1]), F32)], axis=0).astype(BF16)
        s_own = (_dot_nt(ql, kc_own) + _dot_nt(qr, kr_own)) * scale
        q_tok = lax.broadcasted_iota(jnp.int32, s_own.shape, 0) % ntok
        k_tok = lax.broadcasted_iota(jnp.int32, s_own.shape, 1)
        s_own = jnp.where(k_tok <= q_tok, s_own, -jnp.inf)
        _softmax_step(s_own, kc_own, m_ref, l_ref, acc_ref, 0)
        o_ref[...] = (acc_ref[0] / l_ref[0]).reshape(o_ref.shape)


def _mla_sample_attn(qlat, qrope, ckv, krope, pool_lat, pool_rope, layer, page_table, *, seqlen, scale):
    heads, t, lora = qlat.shape
    rope = qrope.shape[2]
    nb, npages = page_table.shape
    pages = min(16, npages)
    assert npages % pages == 0 and seqlen == SUBLANES
    ngrp = npages // pages
    rows = heads * seqlen

    def page_spec(shape, k):
        return pl.BlockSpec((1, 1) + shape, lambda b, j, pt: (layer, pt[b, j * pages + k], 0, 0))

    grid_spec = pltpu.PrefetchScalarGridSpec(
        num_scalar_prefetch=1,
        grid=(nb, ngrp),
        in_specs=[pl.BlockSpec((heads, seqlen, lora), lambda b, j, pt: (0, b, 0)),
                  pl.BlockSpec((heads, seqlen, rope), lambda b, j, pt: (0, b, 0)),
                  pl.BlockSpec((seqlen, lora), lambda b, j, pt: (b, 0)),
                  pl.BlockSpec((seqlen, rope), lambda b, j, pt: (b, 0))]
        + [page_spec((PAGE_SIZE, lora), k) for k in range(pages)]
        + [page_spec((rope, PAGE_SIZE), k) for k in range(pages)],
        out_specs=pl.BlockSpec((heads, seqlen, lora), lambda b, j, pt: (0, b, 0)),
        scratch_shapes=[pltpu.VMEM((1, rows, 1), F32), pltpu.VMEM((1, rows, 1), F32),
                        pltpu.VMEM((1, rows, lora), F32)],
    )
    return pl.pallas_call(
        functools.partial(_mla_sample_kernel, heads=heads, pages=pages, scale=scale),
        grid_spec=grid_spec,
        out_shape=jax.ShapeDtypeStruct((heads, t, lora), F32),
        compiler_params=_cparams(2),
        name="mla_sample_attn",
    )(page_table, qlat, qrope, ckv, krope, *([pool_lat] * pages), *([pool_rope] * pages))


def _mla_out_kernel(o_ref, x_ref, wuv_ref, wo_ref, pw_ref, out_ref, *, heads):
    parts = [_dot(o_ref[h].astype(BF16), wuv_ref[h]) for h in range(heads)]
    o = jnp.concatenate(parts, axis=1).astype(BF16)
    out_ref[...] = x_ref[...] + _rms(_dot(o, wo_ref[...]), pw_ref[...])


def _mla_out(o, x, wuv, wo, pw):
    heads, t, lora = o.shape
    d = x.shape[1]
    tm = _row_tile(t, 256)
    row = pl.BlockSpec((tm, d), lambda i: (i, 0))
    return pl.pallas_call(
        functools.partial(_mla_out_kernel, heads=heads),
        grid=(t // tm,),
        in_specs=[pl.BlockSpec((heads, tm, lora), lambda i: (0, i, 0)), row,
                  _full(wuv.shape), _full(wo.shape), _full(pw.shape)],
        out_specs=row,
        out_shape=jax.ShapeDtypeStruct((t, d), F32),
        compiler_params=_cparams(),
        name="mla_out",
    )(o, x, wuv, wo, pw)


def _pad_lanes(a, width=LANES):
    return jnp.pad(a, [(0, 0)] * (a.ndim - 1) + [(0, width - a.shape[-1])])


def _hist8(prev):
    return jnp.pad(prev, ((0, 0), (SUBLANES - prev.shape[1], 0), (0, 0)))


def _rope_tables(pos, rope, heads):
    half = rope // 2
    inv = ROPE_THETA ** (-jnp.arange(half, dtype=F32) / half)
    ang = pos.astype(F32)[:, None] * inv[None, :]
    cos, sin = jnp.cos(ang), jnp.sin(ang)
    cosk = jnp.concatenate([cos, cos], axis=1)
    sink = jnp.concatenate([-sin, sin], axis=1)
    return jnp.tile(cosk, (1, heads)), jnp.tile(sink, (1, heads)), cosk, sink


def _swap_halves(w, rope):
    k, n = w.shape
    w = w.reshape(k, n // rope, 2, rope // 2)
    return w[:, :, ::-1, :].reshape(k, n)


def _s5_params(a_re, a_im, log_step, b_re, b_im, c_re, c_im, d_model):
    g, s = a_re.shape
    grp = d_model // g
    delta = jnp.exp(log_step)[:, None]
    mag = jnp.exp(a_re * delta)
    ang = a_im * delta
    lam_re, lam_im = mag * jnp.cos(ang), mag * jnp.sin(ang)
    den = a_re * a_re + a_im * a_im
    nr, ni = lam_re - 1.0, lam_im
    f_re = (nr * a_re + ni * a_im) / den
    f_im = (ni * a_re - nr * a_im) / den
    bb_re = f_re[..., None] * b_re - f_im[..., None] * b_im
    bb_im = f_re[..., None] * b_im + f_im[..., None] * b_re
    kw = 256 if d_model % 256 == 0 else d_model
    kb = d_model // kw
    gpb = kw // grp

    def block_diag_in(bb):
        bb = bb.reshape(kb, gpb, s, grp)
        eye = jnp.eye(gpb, dtype=F32)
        w = jnp.einsum("kgsc,gh->kgchs", bb, eye)
        return w.reshape(kb, gpb * grp, gpb * s).astype(BF16)

    def block_diag_out(cc):
        cc = cc.reshape(kb, gpb, grp, s)
        eye = jnp.eye(gpb, dtype=F32)
        w = jnp.einsum("kgcs,gh->kgshc", cc, eye)
        return w.reshape(kb, gpb * s, gpb * grp).astype(BF16)

    nst = g * s // LANES
    return dict(wbr=block_diag_in(bb_re), wbi=block_diag_in(bb_im),
                wcr=block_diag_out(c_re), wci=block_diag_out(c_im),
                lam_re=lam_re.reshape(nst, LANES), lam_im=lam_im.reshape(nst, LANES))


def kernel(x_prompt, x_sample, state_ssd, state_ssd_conv, state_s5_re, state_s5_im, cache_mla_latent, cache_mla_krope, page_table, state_ffn_conv, norm_mix_pre, norm_mix_post, norm_ffn_pre, norm_ffn_post, ssd_w_in, ssd_conv_w, ssd_conv_b, ssd_dt_bias, ssd_a_log, ssd_d, ssd_norm, ssd_w_out, s5_a_re, s5_a_im, s5_log_step, s5_b_re, s5_b_im, s5_c_re, s5_c_im, s5_d, s5_w_val, s5_w_gate, mla_w_dq, mla_q_norm, mla_w_uq, mla_w_dkv, mla_kv_norm, mla_w_uk, mla_w_uv, mla_w_o, ffn_w_gate, ffn_w_up, ffn_conv_w, ffn_conv_b, ffn_w_down):
    b_p, l_p, d = x_prompt.shape
    b_s, l_s, _ = x_sample.shape
    depth = norm_mix_pre.shape[0]
    past_len = page_table.shape[1] * PAGE_SIZE
    groups_of = {"p": (b_p, l_p), "s": (b_s, l_s)}
    hid = {"p": x_prompt.reshape(b_p * l_p, d), "s": x_sample.reshape(b_s * l_s, d)}
    outs = {k: {n: [] for n in ("ssd_h", "ssd_c", "s5r", "s5i", "lat", "kr", "ffc")} for k in ("p", "s")}

    _, _, nh, hd, ds = state_ssd.shape
    di = nh * hd
    cd = state_ssd_conv.shape[-1]
    ssd_groups = (cd - di) // (2 * ds)
    s5_g, s5_s = s5_a_re.shape[1:]
    nst = s5_g * s5_s // LANES
    lora, heads, nope = mla_w_uk.shape[1:]
    rope = cache_mla_krope.shape[-1]
    mla_scale = 1.0 / math.sqrt(nope + rope)
    row1 = lambda a: a.reshape(1, -1)

    for i in range(depth):
        kind, j = i % N_MIXERS, i // N_MIXERS
        if kind == 0:
            w_in = ssd_w_in[j].astype(BF16)
            wz, wx, wdt = w_in[:, :di], w_in[:, di:di + cd], _pad_lanes(w_in[:, di + cd:])
            dtb, alog = (_pad_lanes(row1(a[j])) for a in (ssd_dt_bias, ssd_a_log))
            dsk = jnp.repeat(ssd_d[j], hd).reshape(1, di)
            wo = ssd_w_out[j].astype(BF16)
            for k, (nb, sl) in groups_of.items():
                if k == "p":
                    prev8 = jnp.zeros((nb * SUBLANES, cd), F32)
                    h0, h0_layer = jnp.zeros((1, nb, nh, hd, ds), F32), 0
                else:
                    prev8 = _hist8(state_ssd_conv[j]).reshape(nb * SUBLANES, cd)
                    h0, h0_layer = state_ssd, j
                z, xact, dt, cnew = _ssd_in(hid[k], prev8, row1(norm_mix_pre[i]), wz, wx, wdt, ssd_conv_w[j],
                                            row1(ssd_conv_b[j]), dtb, nb=nb, seqlen=sl)
                y, hfin = _ssd_scan(xact, dt, h0, h0_layer, alog, dsk, nb=nb, seqlen=sl, groups=ssd_groups)
                hid[k] = _ssd_out(y, z, hid[k], row1(ssd_norm[j]), wo, row1(norm_mix_post[i]), groups=ssd_groups)
                outs[k]["ssd_c"].append(cnew.reshape(nb, SUBLANES, cd)[:, SUBLANES - (SSD_CONV - 1):, :])
                outs[k]["ssd_h"].append(hfin)
        elif kind == 1:
            sp = _s5_params(s5_a_re[j], s5_a_im[j], s5_log_step[j], s5_b_re[j], s5_b_im[j], s5_c_re[j], s5_c_im[j], d)
            wv, wg = s5_w_val[j].astype(BF16), s5_w_gate[j].astype(BF16)
            for k, (nb, sl) in groups_of.items():
                if k == "p":
                    h0r = h0i = jnp.zeros((nb, nst, LANES), F32)
                    lamr, lami = sp["lam_re"], sp["lam_im"]
                else:
                    h0r, h0i = state_s5_re[j].reshape(nb, nst * LANES), state_s5_im[j].reshape(nb, nst * LANES)
                    lamr, lami = sp["lam_re"].reshape(1, -1), sp["lam_im"].reshape(1, -1)
                hid[k], hr, hi = _s5(hid[k], h0r, h0i, row1(norm_mix_pre[i]), sp["wbr"], sp["wbi"], lamr, lami,
                                     sp["wcr"], sp["wci"], row1(s5_d[j]), wv, wg, row1(norm_mix_post[i]),
                                     nb=nb, seqlen=sl)
                outs[k]["s5r"].append(hr.reshape(nb, s5_g, s5_s))
                outs[k]["s5i"].append(hi.reshape(nb, s5_g, s5_s))
        else:
            w_uq = mla_w_uq[j].reshape(-1, heads, nope + rope)
            wqn = w_uq[:, :, :nope].reshape(-1, heads * nope).astype(BF16)
            wqr = w_uq[:, :, nope:].reshape(-1, heads * rope).astype(BF16)
            wqrs = _swap_halves(wqr, rope)
            wkv = mla_w_dkv[j][:, :lora].astype(BF16)
            wkr = mla_w_dkv[j][:, lora:].astype(BF16)
            wkrs = _swap_halves(wkr, rope)
            wukt = jnp.transpose(mla_w_uk[j], (1, 2, 0)).astype(BF16)
            wuv = jnp.transpose(mla_w_uv[j], (1, 0, 2)).astype(BF16)
            wdq, wo = mla_w_dq[j].astype(BF16), mla_w_o[j].astype(BF16)
            for k, (nb, sl) in groups_of.items():
                t = nb * sl
                if k == "p":
                    tm = _row_tile(sl, 256)
                    tabs = _rope_tables(jnp.arange(sl), rope, heads)
                    pos_tiles = sl // tm
                else:
                    tm = _row_tile(t, 256)
                    tabs = _rope_tables(past_len + jnp.arange(tm) % sl, rope, heads)
                    pos_tiles = 1
                proj = _mla_proj(
                    hid[k], tabs, row1(norm_mix_pre[i]), wdq, row1(mla_q_norm[j]), wqn, wqr, wqrs, wukt, wkv, wkr, wkrs,
                    row1(mla_kv_norm[j]), pos_tiles=pos_tiles, tm=tm, for_prompt=k == "p")
                qlat, qrope, ckv, kr = proj[:4]
                if k == "p":
                    o = _mla_prompt_attn(qlat, qrope, *proj[4:], nb=nb, seqlen=sl, scale=mla_scale)
                else:
                    o = _mla_sample_attn(qlat, qrope, ckv, kr, cache_mla_latent, jnp.swapaxes(cache_mla_krope, 2, 3),
                                         j, page_table, seqlen=sl, scale=mla_scale)
                hid[k] = _mla_out(o, hid[k], wuv, wo, row1(norm_mix_post[i]))
                outs[k]["lat"].append(ckv.reshape(nb, sl, lora))
                outs[k]["kr"].append(kr.reshape(nb, sl, rope))
        wg, wu, wd = ffn_w_gate[i].astype(BF16), ffn_w_up[i].astype(BF16), ffn_w_down[i].astype(BF16)
        f = wg.shape[1]
        for k, (nb, sl) in groups_of.items():
            if k == "p":
                prev8 = jnp.zeros((nb * SUBLANES, f), F32)
            else:
                prev8 = _hist8(state_ffn_conv[i]).reshape(nb * SUBLANES, f)
            hid[k], cnew = _ffn(hid[k], prev8, row1(norm_ffn_pre[i]), wg, wu, ffn_conv_w[i], row1(ffn_conv_b[i]), wd,
                                row1(norm_ffn_post[i]), nb=nb, seqlen=sl)
            outs[k]["ffc"].append(cnew.reshape(nb, SUBLANES, f)[:, SUBLANES - (FFN_CONV - 1):, :])

    res = [hid["p"].reshape(b_p, l_p, d), hid["s"].reshape(b_s, l_s, d)]
    for k in ("p", "s"):
        res += [jnp.stack(outs[k][n]) for n in ("ssd_h", "ssd_c", "s5r", "s5i", "lat", "kr", "ffc")]
    return tuple(res)
```

```python
import functools
import math

import jax
import jax.numpy as jnp
from jax import lax
from jax.experimental import pallas as pl
from jax.experimental.pallas import tpu as pltpu

F32 = jnp.float32
BF16 = jnp.bfloat16
RMS_EPS = 1e-6
ROPE_THETA = 10000.0
LOG2E = math.log2(math.e)
PAGE_SIZE = 128
N_MIXERS = 3

SUBLANES = 8
LANES = 128
VMEM_LIMIT_BYTES = 56 * 1024 * 1024

SSD_CONV = 4
FFN_CONV = 3
CONV_CHUNK = 1024
ATTN_QUERY_TILE = 256
ATTN_KEY_BLOCK = 512
SAMPLE_KEY_PAGES = 16


def _cparams(n_axes=1):
    return pltpu.CompilerParams(dimension_semantics=("arbitrary",) * n_axes,
                                vmem_limit_bytes=VMEM_LIMIT_BYTES)


def _full(shape):
    nd = len(shape)
    return pl.BlockSpec(shape, lambda i: (0,) * nd)


def _row_tile(n, pref):
    t = min(n, pref)
    while n % t or t % SUBLANES:
        t -= SUBLANES
    assert t > 0
    return t


def _rms(x, w):
    return x * lax.rsqrt(jnp.mean(x * x, axis=-1, keepdims=True) + RMS_EPS) * w


def _sigmoid(x):
    return 0.5 * jnp.tanh(0.5 * x) + 0.5


def _silu(x):
    h = 0.5 * x
    return h * jnp.tanh(h) + h


def _softplus(x):
    return jnp.maximum(x, 0.0) + jnp.log1p(jnp.exp(-jnp.abs(x)))


def _dot(a, b):
    return jnp.dot(a, b, preferred_element_type=F32)


def _dot_nt(a, b):
    return lax.dot_general(a, b, (((1,), (1,)), ((), ())), preferred_element_type=F32)


def _dot_tn(a, b):
    return lax.dot_general(a, b, (((0,), (0,)), ((), ())), preferred_element_type=F32)


def _shifted_rows(x, prev8, s, seg):
    if s == 0:
        return x
    rows = x.shape[0]
    rolled = pltpu.roll(x, s, axis=0)
    if seg == rows:
        head = jnp.where(lax.broadcasted_iota(jnp.int32, (SUBLANES, x.shape[1]), 0) < s,
                         pltpu.roll(prev8, s, axis=0), rolled[:SUBLANES])
        if rows == SUBLANES:
            return head
        return jnp.concatenate([head, rolled[SUBLANES:]], axis=0)
    assert seg == SUBLANES
    hist = pltpu.roll(prev8, (rows + s - SUBLANES) % rows, axis=0) if rows > SUBLANES else pltpu.roll(prev8, s, axis=0)
    t = lax.broadcasted_iota(jnp.int32, x.shape, 0) % SUBLANES
    return jnp.where(t < s, hist, rolled)


def _causal_conv(x, prev8, w, b, seg):
    k = w.shape[0]
    acc = b
    for j in range(k):
        acc = acc + _shifted_rows(x, prev8, k - 1 - j, seg) * w[j:j + 1, :]
    return acc


def _carried_conv(hist_ref, cols, x, w, b):
    acc = _causal_conv(x, hist_ref[:, cols], w, b, x.shape[0])
    hist_ref[:, cols] = x[x.shape[0] - SUBLANES:, :]
    return acc


def _col_chunks(n, width):
    return [slice(lo, min(lo + width, n)) for lo in range(0, n, width)]


def _ssd_in_kernel(x_ref, prev_ref, nw_ref, wz_ref, wx_ref, wdt_ref, cw_ref, cb_ref, dtb_ref,
                   z_ref, xact_ref, dt_ref, convnew_ref, hist_ref, *, seg, tiles_per_seq):
    tm = x_ref.shape[0]
    u = _rms(x_ref[...], nw_ref[...]).astype(BF16)
    if seg == tm:
        @pl.when(pl.program_id(0) % tiles_per_seq == 0)
        def _():
            hist_ref[...] = prev_ref[...]

    for cols in _col_chunks(wx_ref.shape[1], CONV_CHUNK):
        xbc = _dot(u, wx_ref[:, cols])
        if seg == tm:
            conv = _carried_conv(hist_ref, cols, xbc, cw_ref[:, cols], cb_ref[:, cols])
            convnew_ref[:, cols] = xbc[tm - SUBLANES:, :]
        else:
            conv = _causal_conv(xbc, prev_ref[:, cols], cw_ref[:, cols], cb_ref[:, cols], seg)
            convnew_ref[:, cols] = xbc
        xact_ref[:, cols] = _silu(conv)
    z_ref[...] = _dot(u, wz_ref[...])
    dt_ref[...] = _softplus(_dot(u, wdt_ref[...]) + dtb_ref[...])


def _ssd_in(x, prev8, nw, wz, wx, wdt, cw, cb, dtb, *, nb, seqlen):
    t, d = x.shape
    di, cd, dp = wz.shape[1], wx.shape[1], wdt.shape[1]
    if seqlen == SUBLANES:
        tm = _row_tile(t, 256)
        seg, tps = SUBLANES, 1
        hist_spec = pl.BlockSpec((tm, cd), lambda i: (i, 0))
    else:
        tm = _row_tile(seqlen, 256)
        seg, tps = tm, seqlen // tm
        hist_spec = pl.BlockSpec((SUBLANES, cd), lambda i: (i // tps, 0))
    row = lambda n: pl.BlockSpec((tm, n), lambda i: (i, 0))
    consts = [nw, wz, wx, wdt, cw, cb, dtb]
    return pl.pallas_call(
        functools.partial(_ssd_in_kernel, seg=seg, tiles_per_seq=tps),
        grid=(t // tm,),
        in_specs=[row(d), hist_spec] + [_full(a.shape) for a in consts],
        out_specs=[row(di), row(cd), row(dp), hist_spec],
        out_shape=[jax.ShapeDtypeStruct((t, di), F32), jax.ShapeDtypeStruct((t, cd), F32),
                   jax.ShapeDtypeStruct((t, dp), F32), jax.ShapeDtypeStruct((nb * SUBLANES, cd), F32)],
        scratch_shapes=[pltpu.VMEM((SUBLANES, cd), F32)],
        compiler_params=_cparams(),
        name="ssd_in",
    )(x, prev8, *consts)


def _cumsum_rows(x):
    rows = x.shape[0]
    row = lax.broadcasted_iota(jnp.int32, x.shape, 0)
    k = 1
    while k < rows:
        x = x + jnp.where(row >= k, pltpu.roll(x, k, axis=0), 0.0)
        k *= 2
    return x


def _ssd_scan_kernel(xact_ref, dt_ref, h0_ref, alog_ref, dsk_ref, y_ref, hfin_ref, xpad_ref, dtpad_ref, h_ref,
                     *, lt, q, groups, hpg, hd, ds, nc):
    c = pl.program_id(1)
    di = groups * hpg * hd
    gn = groups * ds

    @pl.when(c == 0)
    def _():
        h_ref[...] = h0_ref[0, 0]

    if lt < q:
        xpad_ref[...] = jnp.zeros_like(xpad_ref)
        xpad_ref[0:lt, :] = xact_ref[...]
        dtpad_ref[...] = jnp.zeros_like(dtpad_ref)
        dtpad_ref[0:lt, :] = dt_ref[...]
        xact = xpad_ref[...]
        dt = dtpad_ref[...]
    else:
        xact = xact_ref[...]
        dt = dt_ref[...]
    xs = xact[:, :di]
    bm = xact[:, di:di + gn]
    cm = xact[:, di + gn:]
    a = -jnp.exp(alog_ref[...])
    cs = _cumsum_rows(dt * a) * LOG2E
    cs_last = cs[q - 1:q, :]
    ecs = jnp.exp2(cs)
    e_last = jnp.exp2(cs_last)
    cs_t = cs.T
    dt_t = dt.T
    w_t = (jnp.exp2(cs_last - cs) * dt).T
    causal = (lax.broadcasted_iota(jnp.int32, (q, q), 0) >= lax.broadcasted_iota(jnp.int32, (q, q), 1))
    even = lax.broadcasted_iota(jnp.int32, (q, 2 * hd), 1) < hd
    assert 2 * hd == LANES and hpg % 2 == 0
    y_pairs = []
    for g in range(groups):
        bg = bm[:, g * ds:(g + 1) * ds].astype(BF16)
        cg = cm[:, g * ds:(g + 1) * ds].astype(BF16)
        cb = _dot_nt(cg, bg)
        hg = h_ref[g * hpg:(g + 1) * hpg].reshape(hpg * hd, ds)
        yoff = _dot_nt(cg, hg.astype(BF16))
        xs_g = xs[:, g * hpg * hd:(g + 1) * hpg * hd]
        for k in range(hpg // 2):
            pair = xs_g[:, k * LANES:(k + 1) * LANES].astype(BF16)
            yd, ec = [], []
            for h in (g * hpg + 2 * k, g * hpg + 2 * k + 1):
                seg = cs[:, h:h + 1] - cs_t[h:h + 1, :]
                m = cb * jnp.exp2(jnp.where(causal, seg, -jnp.inf)) * dt_t[h:h + 1, :]
                yd.append(_dot(m.astype(BF16), pair))
                ec.append(ecs[:, h:h + 1])
            y_pairs.append(jnp.where(even, yd[0], yd[1])
                           + yoff[:, k * LANES:(k + 1) * LANES] * jnp.where(even, ec[0], ec[1]))
        xs_gt = xs_g.T
        xw = [xs_gt[r * hd:(r + 1) * hd, :] * w_t[g * hpg + r:g * hpg + r + 1, :] for r in range(hpg)]
        s_g = _dot(jnp.concatenate(xw, axis=0).astype(BF16), bg)
        for r in range(hpg):
            h = g * hpg + r
            h_ref[h] = h_ref[h] * e_last[:, h:h + 1] + s_g[r * hd:(r + 1) * hd, :]
    y = jnp.concatenate(y_pairs, axis=1) + dsk_ref[...] * xs
    y_ref[...] = y[0:lt, :]

    @pl.when(c == nc - 1)
    def _():
        hfin_ref[0] = h_ref[...]


def _ssd_scan(xact, dt, h0, layer, alog, dsk, *, nb, seqlen, groups):
    _, _, nh, hd, ds = h0.shape
    cd = xact.shape[1]
    hpg = nh // groups
    di = nh * hd
    if seqlen >= 256:
        lt = q = 256
    else:
        lt, q = seqlen, LANES
    assert seqlen % lt == 0 and lt % SUBLANES == 0
    nc = seqlen // lt
    pad_rows = q if lt < q else SUBLANES
    kern = functools.partial(_ssd_scan_kernel, lt=lt, q=q, groups=groups, hpg=hpg, hd=hd, ds=ds, nc=nc)
    row = lambda n: pl.BlockSpec((lt, n), lambda b, c: (b * nc + c, 0))
    par = lambda a: pl.BlockSpec(a.shape, lambda b, c: (0,) * a.ndim)
    return pl.pallas_call(
        kern,
        grid=(nb, nc),
        in_specs=[row(cd), row(dt.shape[1]),
                  pl.BlockSpec((1, 1, nh, hd, ds), lambda b, c: (layer, b, 0, 0, 0)),
                  par(alog), par(dsk)],
        out_specs=[row(di), pl.BlockSpec((1, nh, hd, ds), lambda b, c: (b, 0, 0, 0))],
        out_shape=[jax.ShapeDtypeStruct((nb * seqlen, di), F32),
                   jax.ShapeDtypeStruct((nb, nh, hd, ds), F32)],
        scratch_shapes=[pltpu.VMEM((pad_rows, cd), F32), pltpu.VMEM((pad_rows, dt.shape[1]), F32),
                        pltpu.VMEM((nh, hd, ds), F32)],
        compiler_params=_cparams(2),
        name="ssd_scan",
    )(xact, dt, h0, alog, dsk)


def _ssd_out_kernel(y_ref, z_ref, x_ref, gw_ref, wo_ref, pw_ref, o_ref, *, groups):
    yg = y_ref[...] * _silu(z_ref[...])
    gs = yg.shape[1] // groups
    parts = []
    for g in range(groups):
        blk = yg[:, g * gs:(g + 1) * gs]
        parts.append(blk * lax.rsqrt(jnp.mean(blk * blk, axis=-1, keepdims=True) + RMS_EPS))
    yn = (jnp.concatenate(parts, axis=1) * gw_ref[...]).astype(BF16)
    o_ref[...] = x_ref[...] + _rms(_dot(yn, wo_ref[...]), pw_ref[...])


def _ssd_out(y, z, x, gw, wo, pw, *, groups):
    t, d = x.shape
    tm = _row_tile(t, 256)
    row = lambda n: pl.BlockSpec((tm, n), lambda i: (i, 0))
    return pl.pallas_call(
        functools.partial(_ssd_out_kernel, groups=groups),
        grid=(t // tm,),
        in_specs=[row(y.shape[1]), row(z.shape[1]), row(d), _full(gw.shape), _full(wo.shape), _full(pw.shape)],
        out_specs=row(d),
        out_shape=jax.ShapeDtypeStruct((t, d), F32),
        compiler_params=_cparams(),
        name="ssd_out",
    )(y, z, x, gw, wo, pw)


def _ffn_kernel(x_ref, prev_ref, nw_ref, wg_ref, wu_ref, cw_ref, cb_ref, wd_ref, pw_ref,
                o_ref, convnew_ref, hist_ref, *, seg, tiles_per_seq):
    x = x_ref[...]
    tm = x.shape[0]
    u = _rms(x, nw_ref[...]).astype(BF16)
    if seg == tm:
        @pl.when(pl.program_id(0) % tiles_per_seq == 0)
        def _():
            hist_ref[...] = prev_ref[...]

    out = None
    for cols in _col_chunks(wg_ref.shape[1], CONV_CHUNK):
        g = _dot(u, wg_ref[:, cols])
        up = _dot(u, wu_ref[:, cols])
        if seg == tm:
            gc = _carried_conv(hist_ref, cols, g, cw_ref[:, cols], cb_ref[:, cols])
            convnew_ref[:, cols] = g[tm - SUBLANES:, :]
        else:
            gc = _causal_conv(g, prev_ref[:, cols], cw_ref[:, cols], cb_ref[:, cols], seg)
            convnew_ref[:, cols] = g
        part = _dot((_silu(gc) * up).astype(BF16), wd_ref[cols, :])
        out = part if out is None else out + part
    o_ref[...] = x + _rms(out, pw_ref[...])


def _ffn(x, prev8, nw, wg, wu, cw, cb, wd, pw, *, nb, seqlen):
    t, d = x.shape
    f = wg.shape[1]
    if seqlen == SUBLANES:
        tm = _row_tile(t, 256)
        seg, tps = SUBLANES, 1
        prev_spec = pl.BlockSpec((tm, f), lambda i: (i, 0))
        new_spec = pl.BlockSpec((tm, f), lambda i: (i, 0))
    else:
        tm = _row_tile(seqlen, 512)
        seg, tps = tm, seqlen // tm
        prev_spec = pl.BlockSpec((SUBLANES, f), lambda i: (i // tps, 0))
        new_spec = pl.BlockSpec((SUBLANES, f), lambda i: (i // tps, 0))
    row = pl.BlockSpec((tm, d), lambda i: (i, 0))
    return pl.pallas_call(
        functools.partial(_ffn_kernel, seg=seg, tiles_per_seq=tps),
        grid=(t // tm,),
        in_specs=[row, prev_spec, _full(nw.shape), _full(wg.shape), _full(wu.shape), _full(cw.shape),
                  _full(cb.shape), _full(wd.shape), _full(pw.shape)],
        out_specs=[row, new_spec],
        out_shape=[jax.ShapeDtypeStruct((t, d), F32), jax.ShapeDtypeStruct((nb * SUBLANES, f), F32)],
        scratch_shapes=[pltpu.VMEM((SUBLANES, f), F32)],
        compiler_params=_cparams(),
        name="conv_ffn",
    )(x, prev8, nw, wg, wu, cw, cb, wd, pw)


def _gelu(x):
    return 0.5 * x * (1.0 + lax.erf(x * (1.0 / math.sqrt(2.0))))


def _s5_token_pitch(nst):
    groups = -(-nst // SUBLANES)
    return SUBLANES * (groups + 1 - groups % 2)


def _s5_kernel(x_ref, h0r_ref, h0i_ref, nw_ref, wbr_ref, wbi_ref, lamr_ref, lami_ref, wcr_ref, wci_ref,
               dsk_ref, wv_ref, wg_ref, pw_ref, o_ref, hr_out_ref, hi_out_ref, sr_ref, si_ref, cr_ref, ci_ref,
               *, long_seq, tiles_per_seq, kb, nst):
    x = x_ref[...]
    tm, d = x.shape
    u = _rms(x, nw_ref[...])
    ub = u.astype(BF16)
    kw = d // kb
    st_per_kb = nst // kb
    if long_seq:
        i = pl.program_id(0)

        @pl.when(i % tiles_per_seq == 0)
        def _():
            cr_ref[...] = h0r_ref[0]
            ci_ref[...] = h0i_ref[0]

        pitch = _s5_token_pitch(nst)
        for b in range(kb):
            blk = ub[:, b * kw:(b + 1) * kw]
            pr = _dot(blk, wbr_ref[b])
            pi = _dot(blk, wbi_ref[b])
            for j in range(st_per_kb):
                st = b * st_per_kb + j
                sr_ref[pl.ds(st, tm, stride=pitch), :] = pr[:, j * LANES:(j + 1) * LANES]
                si_ref[pl.ds(st, tm, stride=pitch), :] = pi[:, j * LANES:(j + 1) * LANES]
        lr = lamr_ref[...]
        li = lami_ref[...]

        def step(t, carry):
            hr, hi = carry
            off = pl.multiple_of(t * pitch, SUBLANES)
            br = sr_ref[pl.ds(off, nst), :]
            bi = si_ref[pl.ds(off, nst), :]
            nr = lr * hr - li * hi + br
            ni = lr * hi + li * hr + bi
            sr_ref[pl.ds(off, nst), :] = nr
            si_ref[pl.ds(off, nst), :] = ni
            return nr, ni

        hr, hi = lax.fori_loop(0, tm, step, (cr_ref[...], ci_ref[...]), unroll=8)
        cr_ref[...] = hr
        ci_ref[...] = hi
        hr_out_ref[0] = hr
        hi_out_ref[0] = hi
        load_r = lambda st: sr_ref[pl.ds(st, tm, stride=pitch), :]
        load_i = lambda st: si_ref[pl.ds(st, tm, stride=pitch), :]
    else:
        nseq = tm // SUBLANES
        for b in range(kb):
            blk = ub[:, b * kw:(b + 1) * kw]
            pr = _dot(blk, wbr_ref[b])
            pi = _dot(blk, wbi_ref[b])
            for j in range(st_per_kb):
                sr_ref[b * st_per_kb + j] = pr[:, j * LANES:(j + 1) * LANES]
                si_ref[b * st_per_kb + j] = pi[:, j * LANES:(j + 1) * LANES]
        for st in range(nst):
            lanes = slice(st * LANES, (st + 1) * LANES)
            lr = lamr_ref[:, lanes]
            li = lami_ref[:, lanes]
            hr = h0r_ref[:, lanes]
            hi = h0i_ref[:, lanes]
            for t in range(SUBLANES):
                rows_t = pl.ds(t, nseq, stride=SUBLANES)
                br = sr_ref[st, rows_t, :]
                bi = si_ref[st, rows_t, :]
                hr, hi = lr * hr - li * hi + br, lr * hi + li * hr + bi
                sr_ref[st, rows_t, :] = hr
                si_ref[st, rows_t, :] = hi
            hr_out_ref[:, lanes] = hr
            hi_out_ref[:, lanes] = hi
        load_r = lambda st: sr_ref[st]
        load_i = lambda st: si_ref[st]

    ys = []
    for b in range(kb):
        hr_b = jnp.concatenate([load_r(b * st_per_kb + j) for j in range(st_per_kb)], axis=1).astype(BF16)
        hi_b = jnp.concatenate([load_i(b * st_per_kb + j) for j in range(st_per_kb)], axis=1).astype(BF16)
        ys.append(_dot(hr_b, wcr_ref[b]) - _dot(hi_b, wci_ref[b]))
    y = jnp.concatenate(ys, axis=1) + dsk_ref[...] * u
    gl = _gelu(y).astype(BF16)
    val = _dot(gl, wv_ref[...])
    gate = _dot(gl, wg_ref[...])
    out = val * _sigmoid(gate)
    o_ref[...] = x + _rms(out, pw_ref[...])


def _s5(x, h0r, h0i, nw, wbr, wbi, lamr, lami, wcr, wci, dsk, wv, wg, pw, *, nb, seqlen):
    t, d = x.shape
    kb = wbr.shape[0]
    nst = kb * wbr.shape[2] // LANES
    long_seq = seqlen != SUBLANES
    if long_seq:
        tm = _row_tile(seqlen, 128)
        tps = seqlen // tm
        st_spec = pl.BlockSpec((1, nst, LANES), lambda i: (i // tps, 0, 0))
        st_shape = jax.ShapeDtypeStruct((nb, nst, LANES), F32)
        rows = tm * _s5_token_pitch(nst)
        scratch = [pltpu.VMEM((rows, LANES), F32), pltpu.VMEM((rows, LANES), F32),
                   pltpu.VMEM((nst, LANES), F32), pltpu.VMEM((nst, LANES), F32)]
    else:
        tm = _row_tile(t, 256)
        tps = 1
        ns = tm // SUBLANES
        st_spec = pl.BlockSpec((ns, nst * LANES), lambda i: (i, 0))
        st_shape = jax.ShapeDtypeStruct((nb, nst * LANES), F32)
        scratch = [pltpu.VMEM((nst, tm, LANES), F32), pltpu.VMEM((nst, tm, LANES), F32),
                   pltpu.VMEM((SUBLANES, LANES), F32), pltpu.VMEM((SUBLANES, LANES), F32)]
    row = pl.BlockSpec((tm, d), lambda i: (i, 0))
    consts = [nw, wbr, wbi, lamr, lami, wcr, wci, dsk, wv, wg, pw]
    return pl.pallas_call(
        functools.partial(_s5_kernel, long_seq=long_seq, tiles_per_seq=tps, kb=kb, nst=nst),
        grid=(t // tm,),
        in_specs=[row, st_spec, st_spec] + [_full(a.shape) for a in consts],
        out_specs=[row, st_spec, st_spec],
        out_shape=[jax.ShapeDtypeStruct((t, d), F32), st_shape, st_shape],
        scratch_shapes=scratch,
        compiler_params=_cparams(),
        name="s5",
    )(x, h0r, h0i, *consts)


def _mla_proj_kernel(x_ref, cosq_ref, sinq_ref, cosk_ref, sink_ref, nw_ref, wdq_ref, qn_ref, wqn_ref,
                     wqr_ref, wqrs_ref, wukt_ref, wkv_ref, wkr_ref, wkrs_ref, kvn_ref,
                     qlat_ref, qrope_ref, ckv_ref, krope_ref, *key_refs, heads, nope, rope):
    u = _rms(x_ref[...], nw_ref[...]).astype(BF16)
    cq = _rms(_dot(u, wdq_ref[...]), qn_ref[...]).astype(BF16)
    qn = _dot(cq, wqn_ref[...])
    qr = _dot(cq, wqr_ref[...]) * cosq_ref[...] + _dot(cq, wqrs_ref[...]) * sinq_ref[...]
    ckv = _rms(_dot(u, wkv_ref[...]), kvn_ref[...])
    kr = _dot(u, wkr_ref[...]) * cosk_ref[...] + _dot(u, wkrs_ref[...]) * sink_ref[...]
    ckv_ref[...] = ckv
    krope_ref[...] = kr
    for h in range(heads):
        qlat_ref[h] = _dot(qn[:, h * nope:(h + 1) * nope].astype(BF16), wukt_ref[h]).astype(qlat_ref.dtype)
        qrope_ref[h] = qr[:, h * rope:(h + 1) * rope].astype(qrope_ref.dtype)
    if key_refs:
        ckvb_ref, kropeb_ref, ckvt_ref = key_refs
        ckvb_ref[...] = ckv.astype(BF16)
        kropeb_ref[...] = kr.astype(BF16)
        ckvt_ref[0] = ckv.T.astype(BF16)


def _mla_proj(x, tabs, nw, wdq, qn, wqn, wqr, wqrs, wukt, wkv, wkr, wkrs, kvn, *, pos_tiles, tm, for_prompt):
    t, d = x.shape
    heads, nope, lora = wukt.shape
    rope = wkr.shape[1]
    cosq, sinq, cosk, sink = tabs
    row = lambda n: pl.BlockSpec((tm, n), lambda i: (i, 0))
    tab = lambda n: pl.BlockSpec((tm, n), lambda i: (i % pos_tiles, 0))
    consts = [nw, wdq, qn, wqn, wqr, wqrs, wukt, wkv, wkr, wkrs, kvn]
    qdtype = BF16 if for_prompt else F32
    out_specs = [pl.BlockSpec((heads, tm, lora), lambda i: (0, i, 0)),
                 pl.BlockSpec((heads, tm, rope), lambda i: (0, i, 0)), row(lora), row(rope)]
    out_shape = [jax.ShapeDtypeStruct((heads, t, lora), qdtype), jax.ShapeDtypeStruct((heads, t, rope), qdtype),
                 jax.ShapeDtypeStruct((t, lora), F32), jax.ShapeDtypeStruct((t, rope), F32)]
    if for_prompt:
        out_specs += [row(lora), row(rope), pl.BlockSpec((1, lora, tm), lambda i: (i, 0, 0))]
        out_shape += [jax.ShapeDtypeStruct((t, lora), BF16), jax.ShapeDtypeStruct((t, rope), BF16),
                      jax.ShapeDtypeStruct((t // tm, lora, tm), BF16)]
    return pl.pallas_call(
        functools.partial(_mla_proj_kernel, heads=heads, nope=nope, rope=rope),
        grid=(t // tm,),
        in_specs=[row(d), tab(heads * rope), tab(heads * rope), tab(rope), tab(rope)]
        + [_full(a.shape) for a in consts],
        out_specs=out_specs,
        out_shape=out_shape,
        compiler_params=_cparams(),
        name="mla_proj",
    )(x, cosq, sinq, cosk, sink, *consts)


def _mla_prompt_kernel(ql_ref, qr_ref, kc_ref, kr_ref, vt_ref, o_ref, m_ref, l_ref, acc_ref,
                       *, heads, tq, tk, scale):
    qi = pl.program_id(1)
    n_full = (qi * tq) // tk
    n_diag = max(1, tq // tk)
    sub = tk // vt_ref.shape[2]
    m_ref[...] = jnp.full_like(m_ref, -jnp.inf)
    l_ref[...] = jnp.zeros_like(l_ref)
    acc_ref[...] = jnp.zeros_like(acc_ref)
    scale2 = scale * math.log2(math.e)

    nqh = heads * tq
    ql = ql_ref[...].reshape(nqh, ql_ref.shape[2])
    qr = qr_ref[...].reshape(nqh, qr_ref.shape[2])

    def block(ki, masked):
        off = pl.multiple_of(ki * tk, tk)
        kc = kc_ref[pl.ds(off, tk), :]
        kr = kr_ref[pl.ds(off, tk), :]
        s = (_dot_nt(kc, ql) + _dot_nt(kr, qr)) * scale2
        if masked:
            key = off + lax.broadcasted_iota(jnp.int32, (tk, nqh), 0)
            qry = qi * tq + lax.broadcasted_iota(jnp.int32, (tk, nqh), 1) % tq
            s = jnp.where(key <= qry, s, -jnp.inf)
        m_prev = m_ref[...]
        m_new = jnp.maximum(m_prev, jnp.max(s, axis=0, keepdims=True))
        alpha = jnp.exp2(m_prev - m_new)
        p = jnp.exp2(s - m_new)
        l_ref[...] = alpha * l_ref[...] + jnp.sum(p, axis=0, keepdims=True)
        vt = jnp.concatenate([vt_ref[ki * sub + j] for j in range(sub)], axis=1)
        acc_ref[...] = alpha * acc_ref[...] + _dot(vt, p.astype(BF16))
        m_ref[...] = m_new

    def body(ki, carry):
        block(ki, False)
        return carry

    lax.fori_loop(0, n_full, body, 0)
    for j in range(n_diag):
        block(n_full + j, True)
    o_t = acc_ref[...] / l_ref[...]
    for h in range(heads):
        o_ref[h] = o_t[:, h * tq:(h + 1) * tq].T.astype(BF16)


def _mla_prompt_attn(qlat, qrope, ckvb, kropeb, ckv_t, *, nb, seqlen, scale):
    heads, t, lora = qlat.shape
    rope = qrope.shape[2]
    tile = ckv_t.shape[2]
    tq = ATTN_QUERY_TILE if seqlen % ATTN_QUERY_TILE == 0 else tile
    tk = ATTN_KEY_BLOCK if seqlen % ATTN_KEY_BLOCK == 0 else tile
    assert tq % tile == 0 and tk % tile == 0 and (tq % tk == 0 or tk % tq == 0)
    nq = seqlen // tq
    return pl.pallas_call(
        functools.partial(_mla_prompt_kernel, heads=heads, tq=tq, tk=tk, scale=scale),
        grid=(nb, nq),
        in_specs=[pl.BlockSpec((heads, tq, lora), lambda b, i: (0, b * nq + i, 0)),
                  pl.BlockSpec((heads, tq, rope), lambda b, i: (0, b * nq + i, 0)),
                  pl.BlockSpec((seqlen, lora), lambda b, i: (b, 0)),
                  pl.BlockSpec((seqlen, rope), lambda b, i: (b, 0)),
                  pl.BlockSpec((seqlen // tile, lora, tile), lambda b, i: (b, 0, 0))],
        out_specs=pl.BlockSpec((heads, tq, lora), lambda b, i: (0, b * nq + i, 0)),
        out_shape=jax.ShapeDtypeStruct((heads, t, lora), BF16),
        scratch_shapes=[pltpu.VMEM((1, heads * tq), F32), pltpu.VMEM((1, heads * tq), F32),
                        pltpu.VMEM((lora, heads * tq), F32)],
        compiler_params=_cparams(2),
        name="mla_prompt_attn",
    )(qlat, qrope, ckvb, kropeb, ckv_t)


def _mla_sample_kernel(pt_ref, ql_ref, qr_ref, kc_ref, kr_ref, lat_hbm, rope_hbm, o_ref,
                       lat_buf, rope_buf, kcb_ref, sem_lat, sem_rope, *, layer, npages, chunk, scale):
    b = pl.program_id(0)
    slot = b % 2

    def page_copies(seq, dst_slot, p):
        page = pt_ref[seq, p]
        return (pltpu.make_async_copy(lat_hbm.at[layer, page], lat_buf.at[dst_slot, p], sem_lat.at[dst_slot]),
                pltpu.make_async_copy(rope_hbm.at[layer, page], rope_buf.at[dst_slot, p], sem_rope.at[dst_slot]))

    def start_all(seq, dst_slot):
        def body(p, carry):
            for cp in page_copies(seq, dst_slot, p):
                cp.start()
            return carry
        lax.fori_loop(0, npages, body, 0)

    @pl.when(b == 0)
    def _():
        start_all(0, 0)

    @pl.when(b + 1 < pl.num_programs(0))
    def _():
        start_all(b + 1, 1 - slot)

    def wait_body(p, carry):
        for cp in page_copies(b, slot, p):
            cp.wait()
        return carry

    lax.fori_loop(0, npages, wait_body, 0)

    rows = ql_ref.shape[0] * ql_ref.shape[1]
    lora = ql_ref.shape[2]
    ql = ql_ref[...].reshape(rows, lora).astype(BF16)
    qr = qr_ref[...].reshape(rows, qr_ref.shape[2]).astype(BF16)

    ntok = kc_ref.shape[0]
    pad = PAGE_SIZE - ntok
    kc_own = jnp.concatenate([kc_ref[...], jnp.zeros((pad, lora), F32)], axis=0).astype(BF16)
    kr_own = jnp.concatenate([kr_ref[...], jnp.zeros((pad, kr_ref.shape[1]), F32)], axis=0).astype(BF16)
    s_own = (_dot_nt(ql, kc_own) + _dot_nt(qr, kr_own)) * scale
    q_tok = lax.broadcasted_iota(jnp.int32, s_own.shape, 0) % ntok
    k_tok = lax.broadcasted_iota(jnp.int32, s_own.shape, 1)
    scores = [jnp.where(k_tok <= q_tok, s_own, -jnp.inf)]
    ckeys = chunk * PAGE_SIZE
    for c in range(npages // chunk):
        keys = slice(c * ckeys, (c + 1) * ckeys)
        kcb_ref[keys, :] = lat_buf[slot, c * chunk:(c + 1) * chunk].reshape(ckeys, lora).astype(BF16)
        kr_t = jnp.concatenate([rope_buf[slot, c * chunk + k] for k in range(chunk)], axis=1).astype(BF16)
        scores.append((_dot_nt(ql, kcb_ref[keys, :]) + _dot(qr, kr_t)) * scale)
    m = functools.reduce(jnp.maximum, [jnp.max(s, axis=-1, keepdims=True) for s in scores])
    p = jnp.exp(scores[0] - m)
    l = jnp.sum(p, axis=-1, keepdims=True)
    acc = _dot(p.astype(BF16), kc_own)
    for c in range(npages // chunk):
        p = jnp.exp(scores[c + 1] - m)
        l = l + jnp.sum(p, axis=-1, keepdims=True)
        acc = acc + _dot(p.astype(BF16), kcb_ref[c * ckeys:(c + 1) * ckeys, :])
    o_ref[...] = (acc / l).reshape(o_ref.shape)


def _mla_sample_attn(qlat, qrope, ckv, krope, pool_lat, pool_rope_t, layer, page_table, *, seqlen, scale):
    heads, t, lora = qlat.shape
    rope = qrope.shape[2]
    nb, npages = page_table.shape
    chunk = math.gcd(npages, SAMPLE_KEY_PAGES)
    assert seqlen == SUBLANES
    grid_spec = pltpu.PrefetchScalarGridSpec(
        num_scalar_prefetch=1,
        grid=(nb,),
        in_specs=[pl.BlockSpec((heads, seqlen, lora), lambda b, pt: (0, b, 0)),
                  pl.BlockSpec((heads, seqlen, rope), lambda b, pt: (0, b, 0)),
                  pl.BlockSpec((seqlen, lora), lambda b, pt: (b, 0)),
                  pl.BlockSpec((seqlen, rope), lambda b, pt: (b, 0)),
                  pl.BlockSpec(memory_space=pl.ANY), pl.BlockSpec(memory_space=pl.ANY)],
        out_specs=pl.BlockSpec((heads, seqlen, lora), lambda b, pt: (0, b, 0)),
        scratch_shapes=[pltpu.VMEM((2, npages, PAGE_SIZE, lora), F32), pltpu.VMEM((2, npages, rope, PAGE_SIZE), F32),
                        pltpu.VMEM((npages * PAGE_SIZE, lora), BF16),
                        pltpu.SemaphoreType.DMA((2,)), pltpu.SemaphoreType.DMA((2,))],
    )
    return pl.pallas_call(
        functools.partial(_mla_sample_kernel, layer=layer, npages=npages, chunk=chunk, scale=scale),
        grid_spec=grid_spec,
        out_shape=jax.ShapeDtypeStruct((heads, t, lora), F32),
        compiler_params=_cparams(1),
        name="mla_sample_attn",
    )(page_table, qlat, qrope, ckv, krope, pool_lat, pool_rope_t)


def _mla_out_kernel(o_ref, x_ref, wuv_ref, wo_ref, pw_ref, out_ref, *, heads):
    parts = [_dot(o_ref[h].astype(BF16), wuv_ref[h]) for h in range(heads)]
    o = jnp.concatenate(parts, axis=1).astype(BF16)
    out_ref[...] = x_ref[...] + _rms(_dot(o, wo_ref[...]), pw_ref[...])


def _mla_out(o, x, wuv, wo, pw):
    heads, t, lora = o.shape
    d = x.shape[1]
    tm = _row_tile(t, 256)
    row = pl.BlockSpec((tm, d), lambda i: (i, 0))
    return pl.pallas_call(
        functools.partial(_mla_out_kernel, heads=heads),
        grid=(t // tm,),
        in_specs=[pl.BlockSpec((heads, tm, lora), lambda i: (0, i, 0)), row,
                  _full(wuv.shape), _full(wo.shape), _full(pw.shape)],
        out_specs=row,
        out_shape=jax.ShapeDtypeStruct((t, d), F32),
        compiler_params=_cparams(),
        name="mla_out",
    )(o, x, wuv, wo, pw)


def _pad_lanes(a, width=LANES):
    return jnp.pad(a, [(0, 0)] * (a.ndim - 1) + [(0, width - a.shape[-1])])


def _hist8(prev):
    return jnp.pad(prev, ((0, 0), (SUBLANES - prev.shape[1], 0), (0, 0)))


def _rope_tables(pos, rope, heads):
    half = rope // 2
    inv = ROPE_THETA ** (-jnp.arange(half, dtype=F32) / half)
    ang = pos.astype(F32)[:, None] * inv[None, :]
    cos, sin = jnp.cos(ang), jnp.sin(ang)
    cosk = jnp.concatenate([cos, cos], axis=1)
    sink = jnp.concatenate([-sin, sin], axis=1)
    return jnp.tile(cosk, (1, heads)), jnp.tile(sink, (1, heads)), cosk, sink


def _swap_halves(w, rope):
    k, n = w.shape
    w = w.reshape(k, n // rope, 2, rope // 2)
    return w[:, :, ::-1, :].reshape(k, n)


def _s5_params(a_re, a_im, log_step, b_re, b_im, c_re, c_im, d_model):
    g, s = a_re.shape
    grp = d_model // g
    delta = jnp.exp(log_step)[:, None]
    mag = jnp.exp(a_re * delta)
    ang = a_im * delta
    lam_re, lam_im = mag * jnp.cos(ang), mag * jnp.sin(ang)
    den = a_re * a_re + a_im * a_im
    nr, ni = lam_re - 1.0, lam_im
    f_re = (nr * a_re + ni * a_im) / den
    f_im = (ni * a_re - nr * a_im) / den
    bb_re = f_re[..., None] * b_re - f_im[..., None] * b_im
    bb_im = f_re[..., None] * b_im + f_im[..., None] * b_re
    kw = 256 if d_model % 256 == 0 else d_model
    kb = d_model // kw
    gpb = kw // grp

    def block_diag_in(bb):
        bb = bb.reshape(kb, gpb, s, grp)
        eye = jnp.eye(gpb, dtype=F32)
        w = jnp.einsum("kgsc,gh->kgchs", bb, eye)
        return w.reshape(kb, gpb * grp, gpb * s).astype(BF16)

    def block_diag_out(cc):
        cc = cc.reshape(kb, gpb, grp, s)
        eye = jnp.eye(gpb, dtype=F32)
        w = jnp.einsum("kgcs,gh->kgshc", cc, eye)
        return w.reshape(kb, gpb * s, gpb * grp).astype(BF16)

    nst = g * s // LANES
    return dict(wbr=block_diag_in(bb_re), wbi=block_diag_in(bb_im),
                wcr=block_diag_out(c_re), wci=block_diag_out(c_im),
                lam_re=lam_re.reshape(nst, LANES), lam_im=lam_im.reshape(nst, LANES))


def kernel(x_prompt, x_sample, state_ssd, state_ssd_conv, state_s5_re, state_s5_im, cache_mla_latent, cache_mla_krope, page_table, state_ffn_conv, norm_mix_pre, norm_mix_post, norm_ffn_pre, norm_ffn_post, ssd_w_in, ssd_conv_w, ssd_conv_b, ssd_dt_bias, ssd_a_log, ssd_d, ssd_norm, ssd_w_out, s5_a_re, s5_a_im, s5_log_step, s5_b_re, s5_b_im, s5_c_re, s5_c_im, s5_d, s5_w_val, s5_w_gate, mla_w_dq, mla_q_norm, mla_w_uq, mla_w_dkv, mla_kv_norm, mla_w_uk, mla_w_uv, mla_w_o, ffn_w_gate, ffn_w_up, ffn_conv_w, ffn_conv_b, ffn_w_down):
    b_p, l_p, d = x_prompt.shape
    b_s, l_s, _ = x_sample.shape
    depth = norm_mix_pre.shape[0]
    past_len = page_table.shape[1] * PAGE_SIZE
    groups_of = {"p": (b_p, l_p), "s": (b_s, l_s)}
    hid = {"p": x_prompt.reshape(b_p * l_p, d), "s": x_sample.reshape(b_s * l_s, d)}
    outs = {k: {n: [] for n in ("ssd_h", "ssd_c", "s5r", "s5i", "lat", "kr", "ffc")} for k in ("p", "s")}

    _, _, nh, hd, ds = state_ssd.shape
    di = nh * hd
    cd = state_ssd_conv.shape[-1]
    ssd_groups = (cd - di) // (2 * ds)
    s5_g, s5_s = s5_a_re.shape[1:]
    nst = s5_g * s5_s // LANES
    lora, heads, nope = mla_w_uk.shape[1:]
    rope = cache_mla_krope.shape[-1]
    mla_scale = 1.0 / math.sqrt(nope + rope)
    row1 = lambda a: a.reshape(1, -1)

    for i in range(depth):
        kind, j = i % N_MIXERS, i // N_MIXERS
        if kind == 0:
            w_in = ssd_w_in[j].astype(BF16)
            wz, wx, wdt = w_in[:, :di], w_in[:, di:di + cd], _pad_lanes(w_in[:, di + cd:])
            dtb, alog = (_pad_lanes(row1(a[j])) for a in (ssd_dt_bias, ssd_a_log))
            dsk = jnp.repeat(ssd_d[j], hd).reshape(1, di)
            wo = ssd_w_out[j].astype(BF16)
            for k, (nb, sl) in groups_of.items():
                if k == "p":
                    prev8 = jnp.zeros((nb * SUBLANES, cd), F32)
                    h0, h0_layer = jnp.zeros((1, nb, nh, hd, ds), F32), 0
                else:
                    prev8 = _hist8(state_ssd_conv[j]).reshape(nb * SUBLANES, cd)
                    h0, h0_layer = state_ssd, j
                z, xact, dt, cnew = _ssd_in(hid[k], prev8, row1(norm_mix_pre[i]), wz, wx, wdt, ssd_conv_w[j],
                                            row1(ssd_conv_b[j]), dtb, nb=nb, seqlen=sl)
                y, hfin = _ssd_scan(xact, dt, h0, h0_layer, alog, dsk, nb=nb, seqlen=sl, groups=ssd_groups)
                hid[k] = _ssd_out(y, z, hid[k], row1(ssd_norm[j]), wo, row1(norm_mix_post[i]), groups=ssd_groups)
                outs[k]["ssd_c"].append(cnew.reshape(nb, SUBLANES, cd)[:, SUBLANES - (SSD_CONV - 1):, :])
                outs[k]["ssd_h"].append(hfin)
        elif kind == 1:
            sp = _s5_params(s5_a_re[j], s5_a_im[j], s5_log_step[j], s5_b_re[j], s5_b_im[j], s5_c_re[j], s5_c_im[j], d)
            wv, wg = s5_w_val[j].astype(BF16), s5_w_gate[j].astype(BF16)
            for k, (nb, sl) in groups_of.items():
                if k == "p":
                    h0r = h0i = jnp.zeros((nb, nst, LANES), F32)
                    lamr, lami = sp["lam_re"], sp["lam_im"]
                else:
                    h0r, h0i = state_s5_re[j].reshape(nb, nst * LANES), state_s5_im[j].reshape(nb, nst * LANES)
                    lamr, lami = sp["lam_re"].reshape(1, -1), sp["lam_im"].reshape(1, -1)
                hid[k], hr, hi = _s5(hid[k], h0r, h0i, row1(norm_mix_pre[i]), sp["wbr"], sp["wbi"], lamr, lami,
                                     sp["wcr"], sp["wci"], row1(s5_d[j]), wv, wg, row1(norm_mix_post[i]),
                                     nb=nb, seqlen=sl)
                outs[k]["s5r"].append(hr.reshape(nb, s5_g, s5_s))
                outs[k]["s5i"].append(hi.reshape(nb, s5_g, s5_s))
        else:
            w_uq = mla_w_uq[j].reshape(-1, heads, nope + rope)
            wqn = w_uq[:, :, :nope].reshape(-1, heads * nope).astype(BF16)
            wqr = w_uq[:, :, nope:].reshape(-1, heads * rope).astype(BF16)
            wqrs = _swap_halves(wqr, rope)
            wkv = mla_w_dkv[j][:, :lora].astype(BF16)
            wkr = mla_w_dkv[j][:, lora:].astype(BF16)
            wkrs = _swap_halves(wkr, rope)
            wukt = jnp.transpose(mla_w_uk[j], (1, 2, 0)).astype(BF16)
            wuv = jnp.transpose(mla_w_uv[j], (1, 0, 2)).astype(BF16)
            wdq, wo = mla_w_dq[j].astype(BF16), mla_w_o[j].astype(BF16)
            for k, (nb, sl) in groups_of.items():
                t = nb * sl
                if k == "p":
                    tm = _row_tile(sl, 256)
                    tabs = _rope_tables(jnp.arange(sl), rope, heads)
                    pos_tiles = sl // tm
                else:
                    tm = _row_tile(t, 256)
                    tabs = _rope_tables(past_len + jnp.arange(tm) % sl, rope, heads)
                    pos_tiles = 1
                proj = _mla_proj(
                    hid[k], tabs, row1(norm_mix_pre[i]), wdq, row1(mla_q_norm[j]), wqn, wqr, wqrs, wukt, wkv, wkr, wkrs,
                    row1(mla_kv_norm[j]), pos_tiles=pos_tiles, tm=tm, for_prompt=k == "p")
                qlat, qrope, ckv, kr = proj[:4]
                if k == "p":
                    o = _mla_prompt_attn(qlat, qrope, *proj[4:], nb=nb, seqlen=sl, scale=mla_scale)
                else:
                    o = _mla_sample_attn(qlat, qrope, ckv, kr, cache_mla_latent, jnp.swapaxes(cache_mla_krope, 2, 3),
                                         j, page_table, seqlen=sl, scale=mla_scale)
                hid[k] = _mla_out(o, hid[k], wuv, wo, row1(norm_mix_post[i]))
                outs[k]["lat"].append(ckv.reshape(nb, sl, lora))
                outs[k]["kr"].append(kr.reshape(nb, sl, rope))
        wg, wu, wd = ffn_w_gate[i].astype(BF16), ffn_w_up[i].astype(BF16), ffn_w_down[i].astype(BF16)
        f = wg.shape[1]
        for k, (nb, sl) in groups_of.items():
            if k == "p":
                prev8 = jnp.zeros((nb * SUBLANES, f), F32)
            else:
                prev8 = _hist8(state_ffn_conv[i]).reshape(nb * SUBLANES, f)
            hid[k], cnew = _ffn(hid[k], prev8, row1(norm_ffn_pre[i]), wg, wu, ffn_conv_w[i], row1(ffn_conv_b[i]), wd,
                                row1(norm_ffn_post[i]), nb=nb, seqlen=sl)
            outs[k]["ffc"].append(cnew.reshape(nb, SUBLANES, f)[:, SUBLANES - (FFN_CONV - 1):, :])

    res = [hid["p"].reshape(b_p, l_p, d), hid["s"].reshape(b_s, l_s, d)]
    for k in ("p", "s"):
        res += [jnp.stack(outs[k][n]) for n in ("ssd_h", "ssd_c", "s5r", "s5i", "lat", "kr", "ffc")]
    return tuple(res)
```

```python
import functools
import math

import jax
import jax.numpy as jnp
from jax import lax
from jax.experimental import pallas as pl
from jax.experimental.pallas import tpu as pltpu

F32 = jnp.float32
BF16 = jnp.bfloat16
RMS_EPS = 1e-6
ROPE_THETA = 10000.0
LOG2E = math.log2(math.e)
PAGE_SIZE = 128
N_MIXERS = 3

SUBLANES = 8
LANES = 128
VMEM_LIMIT_BYTES = 56 * 1024 * 1024

SSD_CONV = 4
FFN_CONV = 3
CONV_CHUNK = 1024
ATTN_QUERY_TILE = 256
ATTN_KEY_BLOCK = 512
SAMPLE_KEY_PAGES = 16


def _cparams(n_axes=1):
    return pltpu.CompilerParams(dimension_semantics=("arbitrary",) * n_axes,
                                vmem_limit_bytes=VMEM_LIMIT_BYTES)


def _full(shape):
    nd = len(shape)
    return pl.BlockSpec(shape, lambda i: (0,) * nd)


def _layer_of(stacked, layer):
    nd = stacked.ndim - 1
    return pl.BlockSpec((None,) + stacked.shape[1:], lambda i: (layer,) + (0,) * nd)


def _row_tile(n, pref):
    t = min(n, pref)
    while n % t or t % SUBLANES:
        t -= SUBLANES
    assert t > 0
    return t


def _rms(x, w):
    return x * lax.rsqrt(jnp.mean(x * x, axis=-1, keepdims=True) + RMS_EPS) * w


def _sigmoid(x):
    return 0.5 * jnp.tanh(0.5 * x) + 0.5


def _silu(x):
    h = 0.5 * x
    return h * jnp.tanh(h) + h


def _softplus(x):
    return jnp.maximum(x, 0.0) + jnp.log1p(jnp.exp(-jnp.abs(x)))


def _dot(a, b):
    return jnp.dot(a, b, preferred_element_type=F32)


def _dot_nt(a, b):
    return lax.dot_general(a, b, (((1,), (1,)), ((), ())), preferred_element_type=F32)


def _dot_tn(a, b):
    return lax.dot_general(a, b, (((0,), (0,)), ((), ())), preferred_element_type=F32)


def _shifted_rows(x, prev8, s, seg):
    if s == 0:
        return x
    rows = x.shape[0]
    rolled = pltpu.roll(x, s, axis=0)
    if seg == rows:
        head = jnp.where(lax.broadcasted_iota(jnp.int32, (SUBLANES, x.shape[1]), 0) < s,
                         pltpu.roll(prev8, s, axis=0), rolled[:SUBLANES])
        if rows == SUBLANES:
            return head
        return jnp.concatenate([head, rolled[SUBLANES:]], axis=0)
    assert seg == SUBLANES
    hist = pltpu.roll(prev8, (rows + s - SUBLANES) % rows, axis=0) if rows > SUBLANES else pltpu.roll(prev8, s, axis=0)
    t = lax.broadcasted_iota(jnp.int32, x.shape, 0) % SUBLANES
    return jnp.where(t < s, hist, rolled)


def _causal_conv(x, prev8, w, b, seg):
    k = w.shape[0]
    acc = b
    for j in range(k):
        acc = acc + _shifted_rows(x, prev8, k - 1 - j, seg) * w[j:j + 1, :]
    return acc


def _carried_conv(hist_ref, cols, x, w, b):
    acc = _causal_conv(x, hist_ref[:, cols], w, b, x.shape[0])
    hist_ref[:, cols] = x[x.shape[0] - SUBLANES:, :]
    return acc


def _col_chunks(n, width):
    return [slice(lo, min(lo + width, n)) for lo in range(0, n, width)]


def _ssd_in_kernel(x_ref, prev_ref, nw_ref, wz_ref, wx_ref, wdt_ref, cw_ref, cb_ref, dtb_ref,
                   z_ref, xact_ref, dt_ref, convnew_ref, hist_ref, *, seg, tiles_per_seq):
    tm = x_ref.shape[0]
    u = _rms(x_ref[...], nw_ref[...]).astype(BF16)
    if seg == tm:
        @pl.when(pl.program_id(0) % tiles_per_seq == 0)
        def _():
            hist_ref[...] = prev_ref[...]

    for cols in _col_chunks(wx_ref.shape[1], CONV_CHUNK):
        xbc = _dot(u, wx_ref[:, cols])
        if seg == tm:
            conv = _carried_conv(hist_ref, cols, xbc, cw_ref[:, cols], cb_ref[:, cols])
            convnew_ref[:, cols] = xbc[tm - SUBLANES:, :]
        else:
            conv = _causal_conv(xbc, prev_ref[:, cols], cw_ref[:, cols], cb_ref[:, cols], seg)
            convnew_ref[:, cols] = xbc
        xact_ref[:, cols] = _silu(conv)
    z_ref[...] = _dot(u, wz_ref[...])
    dt_ref[...] = _softplus(_dot(u, wdt_ref[...]) + dtb_ref[...])


def _ssd_in(x, prev8, nw, wz, wx, wdt, cw, cb, dtb, *, nb, seqlen):
    t, d = x.shape
    di, cd, dp = wz.shape[1], wx.shape[1], wdt.shape[1]
    if seqlen == SUBLANES:
        tm = _row_tile(t, 256)
        seg, tps = SUBLANES, 1
        hist_spec = pl.BlockSpec((tm, cd), lambda i: (i, 0))
    else:
        tm = _row_tile(seqlen, 256)
        seg, tps = tm, seqlen // tm
        hist_spec = pl.BlockSpec((SUBLANES, cd), lambda i: (i // tps, 0))
    row = lambda n: pl.BlockSpec((tm, n), lambda i: (i, 0))
    consts = [nw, wz, wx, wdt, cw, cb, dtb]
    return pl.pallas_call(
        functools.partial(_ssd_in_kernel, seg=seg, tiles_per_seq=tps),
        grid=(t // tm,),
        in_specs=[row(d), hist_spec] + [_full(a.shape) for a in consts],
        out_specs=[row(di), row(cd), row(dp), hist_spec],
        out_shape=[jax.ShapeDtypeStruct((t, di), F32), jax.ShapeDtypeStruct((t, cd), F32),
                   jax.ShapeDtypeStruct((t, dp), F32), jax.ShapeDtypeStruct((nb * SUBLANES, cd), F32)],
        scratch_shapes=[pltpu.VMEM((SUBLANES, cd), F32)],
        compiler_params=_cparams(),
        name="ssd_in",
    )(x, prev8, *consts)


def _cumsum_rows(x):
    rows = x.shape[0]
    row = lax.broadcasted_iota(jnp.int32, x.shape, 0)
    k = 1
    while k < rows:
        x = x + jnp.where(row >= k, pltpu.roll(x, k, axis=0), 0.0)
        k *= 2
    return x


def _ssd_scan_kernel(xact_ref, dt_ref, h0_ref, alog_ref, dsk_ref, expand_ref, *rest,
                     lt, q, groups, hpg, hd, ds, nc, n_prev):
    hprev_ref = rest[0] if n_prev else None
    y_ref, hfin_ref, xpad_ref, dtpad_ref, h_ref = rest[1 if n_prev else 0:]
    c = pl.program_id(1)
    di = groups * hpg * hd
    gn = groups * ds

    @pl.when(c == 0)
    def _():
        h_ref[...] = h0_ref[0, 0]

    if lt < q:
        xpad_ref[...] = jnp.zeros_like(xpad_ref)
        xpad_ref[0:lt, :] = xact_ref[...]
        dtpad_ref[...] = jnp.zeros_like(dtpad_ref)
        dtpad_ref[0:lt, :] = dt_ref[...]
        xact = xpad_ref[...]
        dt = dtpad_ref[...]
    else:
        xact = xact_ref[...]
        dt = dt_ref[...]
    xs = xact[:, :di]
    bm = xact[:, di:di + gn]
    cm = xact[:, di + gn:]
    a = -jnp.exp(alog_ref[...])
    cs = _cumsum_rows(dt * a) * LOG2E
    cs_last = cs[q - 1:q, :]
    nh = groups * hpg
    assert 3 * nh <= LANES
    ecs = jnp.exp2(cs)
    hi = ecs.astype(BF16).astype(F32)
    mid = (ecs - hi).astype(BF16).astype(F32)
    lo = ecs - hi - mid
    lane = lax.broadcasted_iota(jnp.int32, ecs.shape, 1)
    packed = jnp.where(lane < nh, hi, jnp.where(lane < 2 * nh, pltpu.roll(mid, nh, axis=1), pltpu.roll(lo, 2 * nh, axis=1)))
    ecs_x = _dot(packed.astype(BF16), expand_ref[...])
    e_last = jnp.exp2(cs_last)
    cs_t = cs.T
    dt_t = dt.T
    w_t = (jnp.exp2(cs_last - cs) * dt).T
    causal = (lax.broadcasted_iota(jnp.int32, (q, q), 0) >= lax.broadcasted_iota(jnp.int32, (q, q), 1))
    even = lax.broadcasted_iota(jnp.int32, (q, 2 * hd), 1) < hd
    assert 2 * hd == LANES and hpg % 2 == 0
    y_pairs = []
    for g in range(groups):
        bg = bm[:, g * ds:(g + 1) * ds].astype(BF16)
        cg = cm[:, g * ds:(g + 1) * ds].astype(BF16)
        cb = _dot_nt(cg, bg)
        hg = h_ref[g * hpg:(g + 1) * hpg].reshape(hpg * hd, ds)
        yoff = _dot_nt(cg, hg.astype(BF16))
        xs_g = xs[:, g * hpg * hd:(g + 1) * hpg * hd]
        for k in range(hpg // 2):
            pair = xs_g[:, k * LANES:(k + 1) * LANES].astype(BF16)
            yd = []
            for h in (g * hpg + 2 * k, g * hpg + 2 * k + 1):
                seg = cs[:, h:h + 1] - cs_t[h:h + 1, :]
                m = cb * jnp.exp2(jnp.where(causal, seg, -jnp.inf)) * dt_t[h:h + 1, :]
                yd.append(_dot(m.astype(BF16), pair))
            lo = g * hpg * hd + k * LANES
            y_pairs.append(jnp.where(even, yd[0], yd[1]) + yoff[:, k * LANES:(k + 1) * LANES] * ecs_x[:, lo:lo + LANES])
        xs_gt = xs_g.T
        xw = [xs_gt[r * hd:(r + 1) * hd, :] * w_t[g * hpg + r:g * hpg + r + 1, :] for r in range(hpg)]
        s_g = _dot(jnp.concatenate(xw, axis=0).astype(BF16), bg)
        for r in range(hpg):
            h = g * hpg + r
            h_ref[h] = h_ref[h] * e_last[:, h:h + 1] + s_g[r * hd:(r + 1) * hd, :]
    y = jnp.concatenate(y_pairs, axis=1) + dsk_ref[...] * xs
    y_ref[...] = y[0:lt, :]

    @pl.when(c == nc - 1)
    def _():
        for k in range(n_prev):
            hfin_ref[k, 0] = hprev_ref[k, 0]
        hfin_ref[n_prev, 0] = h_ref[...]


def _ssd_scan(xact, dt, h0, layer, alog, dsk, h_prev, *, nb, seqlen, groups):
    _, _, nh, hd, ds = h0.shape
    n_prev = 0 if h_prev is None else h_prev.shape[0]
    state_blk = lambda n: pl.BlockSpec((n, 1, nh, hd, ds), lambda b, c: (0, b, 0, 0, 0))
    piece_row = jnp.arange(LANES)[:, None]
    expand = ((jnp.arange(nh * hd)[None, :] // hd == piece_row % nh) & (piece_row < 3 * nh)).astype(BF16)
    cd = xact.shape[1]
    hpg = nh // groups
    di = nh * hd
    if seqlen >= 256:
        lt = q = 256
    else:
        lt, q = seqlen, LANES
    assert seqlen % lt == 0 and lt % SUBLANES == 0
    nc = seqlen // lt
    pad_rows = q if lt < q else SUBLANES
    kern = functools.partial(_ssd_scan_kernel, lt=lt, q=q, groups=groups, hpg=hpg, hd=hd, ds=ds, nc=nc,
                             n_prev=n_prev)
    row = lambda n: pl.BlockSpec((lt, n), lambda b, c: (b * nc + c, 0))
    par = lambda a: pl.BlockSpec(a.shape, lambda b, c: (0,) * a.ndim)
    return pl.pallas_call(
        kern,
        grid=(nb, nc),
        in_specs=[row(cd), row(dt.shape[1]),
                  pl.BlockSpec((1, 1, nh, hd, ds), lambda b, c: (layer, b, 0, 0, 0)),
                  par(alog), par(dsk), par(expand)] + ([state_blk(n_prev)] if n_prev else []),
        out_specs=[row(di), state_blk(n_prev + 1)],
        out_shape=[jax.ShapeDtypeStruct((nb * seqlen, di), F32),
                   jax.ShapeDtypeStruct((n_prev + 1, nb, nh, hd, ds), F32)],
        scratch_shapes=[pltpu.VMEM((pad_rows, cd), F32), pltpu.VMEM((pad_rows, dt.shape[1]), F32),
                        pltpu.VMEM((nh, hd, ds), F32)],
        compiler_params=_cparams(2),
        name="ssd_scan",
    )(xact, dt, h0, alog, dsk, expand, *([h_prev] if n_prev else []))


def _ssd_out_kernel(y_ref, z_ref, x_ref, gw_ref, wo_ref, pw_ref, o_ref, *, groups):
    yg = y_ref[...] * _silu(z_ref[...])
    gs = yg.shape[1] // groups
    parts = []
    for g in range(groups):
        blk = yg[:, g * gs:(g + 1) * gs]
        parts.append(blk * lax.rsqrt(jnp.mean(blk * blk, axis=-1, keepdims=True) + RMS_EPS))
    yn = (jnp.concatenate(parts, axis=1) * gw_ref[...]).astype(BF16)
    o_ref[...] = x_ref[...] + _rms(_dot(yn, wo_ref[...]), pw_ref[...])


def _ssd_out(y, z, x, gw, wo, pw, *, groups):
    t, d = x.shape
    tm = _row_tile(t, 256)
    row = lambda n: pl.BlockSpec((tm, n), lambda i: (i, 0))
    return pl.pallas_call(
        functools.partial(_ssd_out_kernel, groups=groups),
        grid=(t // tm,),
        in_specs=[row(y.shape[1]), row(z.shape[1]), row(d), _full(gw.shape), _full(wo.shape), _full(pw.shape)],
        out_specs=row(d),
        out_shape=jax.ShapeDtypeStruct((t, d), F32),
        compiler_params=_cparams(),
        name="ssd_out",
    )(y, z, x, gw, wo, pw)


def _ffn_kernel(x_ref, prev_ref, nw_ref, wg_ref, wu_ref, cw_ref, cb_ref, wd_ref, pw_ref,
                o_ref, convnew_ref, hist_ref, *, seg, tiles_per_seq):
    x = x_ref[...]
    tm = x.shape[0]
    u = _rms(x, nw_ref[...]).astype(BF16)
    if seg == tm:
        @pl.when(pl.program_id(0) % tiles_per_seq == 0)
        def _():
            hist_ref[...] = prev_ref[...]

    out = None
    for cols in _col_chunks(wg_ref.shape[1], CONV_CHUNK):
        g = _dot(u, wg_ref[:, cols])
        up = _dot(u, wu_ref[:, cols])
        if seg == tm:
            gc = _carried_conv(hist_ref, cols, g, cw_ref[:, cols], cb_ref[:, cols])
            convnew_ref[:, cols] = g[tm - SUBLANES:, :]
        else:
            gc = _causal_conv(g, prev_ref[:, cols], cw_ref[:, cols], cb_ref[:, cols], seg)
            convnew_ref[:, cols] = g
        part = _dot((_silu(gc) * up).astype(BF16), wd_ref[cols, :])
        out = part if out is None else out + part
    o_ref[...] = x + _rms(out, pw_ref[...])


def _ffn(x, prev8, nw, wg, wu, cw, cb, wd, pw, layer, *, nb, seqlen):
    t, d = x.shape
    f = wg.shape[2]
    if seqlen == SUBLANES:
        tm = _row_tile(t, 256)
        seg, tps = SUBLANES, 1
        prev_spec = pl.BlockSpec((tm, f), lambda i: (i, 0))
        new_spec = pl.BlockSpec((tm, f), lambda i: (i, 0))
    else:
        tm = _row_tile(seqlen, 512)
        seg, tps = tm, seqlen // tm
        prev_spec = pl.BlockSpec((SUBLANES, f), lambda i: (i // tps, 0))
        new_spec = pl.BlockSpec((SUBLANES, f), lambda i: (i // tps, 0))
    row = pl.BlockSpec((tm, d), lambda i: (i, 0))
    return pl.pallas_call(
        functools.partial(_ffn_kernel, seg=seg, tiles_per_seq=tps),
        grid=(t // tm,),
        in_specs=[row, prev_spec, _full(nw.shape), _layer_of(wg, layer), _layer_of(wu, layer), _full(cw.shape),
                  _full(cb.shape), _layer_of(wd, layer), _full(pw.shape)],
        out_specs=[row, new_spec],
        out_shape=[jax.ShapeDtypeStruct((t, d), F32), jax.ShapeDtypeStruct((nb * SUBLANES, f), F32)],
        scratch_shapes=[pltpu.VMEM((SUBLANES, f), F32)],
        compiler_params=_cparams(),
        name="conv_ffn",
    )(x, prev8, nw, wg, wu, cw, cb, wd, pw)


def _gelu(x):
    return 0.5 * x * (1.0 + lax.erf(x * (1.0 / math.sqrt(2.0))))


def _s5_token_pitch(nst):
    groups = -(-nst // SUBLANES)
    return SUBLANES * (groups + 1 - groups % 2)


def _s5_kernel(x_ref, h0r_ref, h0i_ref, nw_ref, wbr_ref, wbi_ref, lamr_ref, lami_ref, wcr_ref, wci_ref,
               dsk_ref, wv_ref, wg_ref, pw_ref, o_ref, hr_out_ref, hi_out_ref, sr_ref, si_ref, cr_ref, ci_ref,
               *, long_seq, tiles_per_seq, kb, nst):
    x = x_ref[...]
    tm, d = x.shape
    u = _rms(x, nw_ref[...])
    ub = u.astype(BF16)
    kw = d // kb
    st_per_kb = nst // kb
    if long_seq:
        i = pl.program_id(0)

        @pl.when(i % tiles_per_seq == 0)
        def _():
            cr_ref[...] = h0r_ref[0]
            ci_ref[...] = h0i_ref[0]

        pitch = _s5_token_pitch(nst)
        for b in range(kb):
            blk = ub[:, b * kw:(b + 1) * kw]
            pr = _dot(blk, wbr_ref[b])
            pi = _dot(blk, wbi_ref[b])
            for j in range(st_per_kb):
                st = b * st_per_kb + j
                sr_ref[pl.ds(st, tm, stride=pitch), :] = pr[:, j * LANES:(j + 1) * LANES]
                si_ref[pl.ds(st, tm, stride=pitch), :] = pi[:, j * LANES:(j + 1) * LANES]
        lr = lamr_ref[...]
        li = lami_ref[...]

        def step(t, carry):
            hr, hi = carry
            off = pl.multiple_of(t * pitch, SUBLANES)
            br = sr_ref[pl.ds(off, nst), :]
            bi = si_ref[pl.ds(off, nst), :]
            nr = lr * hr - li * hi + br
            ni = lr * hi + li * hr + bi
            sr_ref[pl.ds(off, nst), :] = nr
            si_ref[pl.ds(off, nst), :] = ni
            return nr, ni

        hr, hi = lax.fori_loop(0, tm, step, (cr_ref[...], ci_ref[...]), unroll=8)
        cr_ref[...] = hr
        ci_ref[...] = hi
        hr_out_ref[0] = hr
        hi_out_ref[0] = hi
        load_r = lambda st: sr_ref[pl.ds(st, tm, stride=pitch), :]
        load_i = lambda st: si_ref[pl.ds(st, tm, stride=pitch), :]
    else:
        nseq = tm // SUBLANES
        for b in range(kb):
            blk = ub[:, b * kw:(b + 1) * kw]
            pr = _dot(blk, wbr_ref[b])
            pi = _dot(blk, wbi_ref[b])
            for j in range(st_per_kb):
                sr_ref[b * st_per_kb + j] = pr[:, j * LANES:(j + 1) * LANES]
                si_ref[b * st_per_kb + j] = pi[:, j * LANES:(j + 1) * LANES]
        for st in range(nst):
            lanes = slice(st * LANES, (st + 1) * LANES)
            lr = lamr_ref[:, lanes]
            li = lami_ref[:, lanes]
            hr = h0r_ref[:, lanes]
            hi = h0i_ref[:, lanes]
            for t in range(SUBLANES):
                rows_t = pl.ds(t, nseq, stride=SUBLANES)
                br = sr_ref[st, rows_t, :]
                bi = si_ref[st, rows_t, :]
                hr, hi = lr * hr - li * hi + br, lr * hi + li * hr + bi
                sr_ref[st, rows_t, :] = hr
                si_ref[st, rows_t, :] = hi
            hr_out_ref[:, lanes] = hr
            hi_out_ref[:, lanes] = hi
        load_r = lambda st: sr_ref[st]
        load_i = lambda st: si_ref[st]

    ys = []
    for b in range(kb):
        hr_b = jnp.concatenate([load_r(b * st_per_kb + j) for j in range(st_per_kb)], axis=1).astype(BF16)
        hi_b = jnp.concatenate([load_i(b * st_per_kb + j) for j in range(st_per_kb)], axis=1).astype(BF16)
        ys.append(_dot(hr_b, wcr_ref[b]) - _dot(hi_b, wci_ref[b]))
    y = jnp.concatenate(ys, axis=1) + dsk_ref[...] * u
    gl = _gelu(y).astype(BF16)
    val = _dot(gl, wv_ref[...])
    gate = _dot(gl, wg_ref[...])
    out = val * _sigmoid(gate)
    o_ref[...] = x + _rms(out, pw_ref[...])


def _s5(x, h0r, h0i, nw, wbr, wbi, lamr, lami, wcr, wci, dsk, wv, wg, pw, *, nb, seqlen):
    t, d = x.shape
    kb = wbr.shape[0]
    nst = kb * wbr.shape[2] // LANES
    long_seq = seqlen != SUBLANES
    if long_seq:
        tm = _row_tile(seqlen, 128)
        tps = seqlen // tm
        st_spec = pl.BlockSpec((1, nst, LANES), lambda i: (i // tps, 0, 0))
        st_shape = jax.ShapeDtypeStruct((nb, nst, LANES), F32)
        rows = tm * _s5_token_pitch(nst)
        scratch = [pltpu.VMEM((rows, LANES), F32), pltpu.VMEM((rows, LANES), F32),
                   pltpu.VMEM((nst, LANES), F32), pltpu.VMEM((nst, LANES), F32)]
    else:
        tm = _row_tile(t, 256)
        tps = 1
        ns = tm // SUBLANES
        st_spec = pl.BlockSpec((ns, nst * LANES), lambda i: (i, 0))
        st_shape = jax.ShapeDtypeStruct((nb, nst * LANES), F32)
        scratch = [pltpu.VMEM((nst, tm, LANES), F32), pltpu.VMEM((nst, tm, LANES), F32),
                   pltpu.VMEM((SUBLANES, LANES), F32), pltpu.VMEM((SUBLANES, LANES), F32)]
    row = pl.BlockSpec((tm, d), lambda i: (i, 0))
    consts = [nw, wbr, wbi, lamr, lami, wcr, wci, dsk, wv, wg, pw]
    return pl.pallas_call(
        functools.partial(_s5_kernel, long_seq=long_seq, tiles_per_seq=tps, kb=kb, nst=nst),
        grid=(t // tm,),
        in_specs=[row, st_spec, st_spec] + [_full(a.shape) for a in consts],
        out_specs=[row, st_spec, st_spec],
        out_shape=[jax.ShapeDtypeStruct((t, d), F32), st_shape, st_shape],
        scratch_shapes=scratch,
        compiler_params=_cparams(),
        name="s5",
    )(x, h0r, h0i, *consts)


def _mla_proj_kernel(x_ref, cosq_ref, sinq_ref, cosk_ref, sink_ref, nw_ref, wdq_ref, qn_ref, wqn_ref,
                     wqr_ref, wqrs_ref, wukt_ref, wkv_ref, wkr_ref, wkrs_ref, kvn_ref,
                     qlat_ref, qrope_ref, ckv_ref, krope_ref, *key_refs, heads, nope, rope):
    u = _rms(x_ref[...], nw_ref[...]).astype(BF16)
    cq = _rms(_dot(u, wdq_ref[...]), qn_ref[...]).astype(BF16)
    qn = _dot(cq, wqn_ref[...])
    qr = _dot(cq, wqr_ref[...]) * cosq_ref[...] + _dot(cq, wqrs_ref[...]) * sinq_ref[...]
    ckv = _rms(_dot(u, wkv_ref[...]), kvn_ref[...])
    kr = _dot(u, wkr_ref[...]) * cosk_ref[...] + _dot(u, wkrs_ref[...]) * sink_ref[...]
    ckv_ref[...] = ckv
    krope_ref[...] = kr
    for h in range(heads):
        qlat_ref[h] = _dot(qn[:, h * nope:(h + 1) * nope].astype(BF16), wukt_ref[h]).astype(qlat_ref.dtype)
        qrope_ref[h] = qr[:, h * rope:(h + 1) * rope].astype(qrope_ref.dtype)
    if key_refs:
        ckvb_ref, kropeb_ref, ckvt_ref = key_refs
        ckvb_ref[...] = ckv.astype(BF16)
        kropeb_ref[...] = kr.astype(BF16)
        ckvt_ref[0] = ckv.T.astype(BF16)


def _mla_proj(x, tabs, nw, wdq, qn, wqn, wqr, wqrs, wukt, wkv, wkr, wkrs, kvn, *, pos_tiles, tm, for_prompt):
    t, d = x.shape
    heads, nope, lora = wukt.shape
    rope = wkr.shape[1]
    cosq, sinq, cosk, sink = tabs
    row = lambda n: pl.BlockSpec((tm, n), lambda i: (i, 0))
    tab = lambda n: pl.BlockSpec((tm, n), lambda i: (i % pos_tiles, 0))
    consts = [nw, wdq, qn, wqn, wqr, wqrs, wukt, wkv, wkr, wkrs, kvn]
    qdtype = BF16 if for_prompt else F32
    out_specs = [pl.BlockSpec((heads, tm, lora), lambda i: (0, i, 0)),
                 pl.BlockSpec((heads, tm, rope), lambda i: (0, i, 0)), row(lora), row(rope)]
    out_shape = [jax.ShapeDtypeStruct((heads, t, lora), qdtype), jax.ShapeDtypeStruct((heads, t, rope), qdtype),
                 jax.ShapeDtypeStruct((t, lora), F32), jax.ShapeDtypeStruct((t, rope), F32)]
    if for_prompt:
        out_specs += [row(lora), row(rope), pl.BlockSpec((1, lora, tm), lambda i: (i, 0, 0))]
        out_shape += [jax.ShapeDtypeStruct((t, lora), BF16), jax.ShapeDtypeStruct((t, rope), BF16),
                      jax.ShapeDtypeStruct((t // tm, lora, tm), BF16)]
    return pl.pallas_call(
        functools.partial(_mla_proj_kernel, heads=heads, nope=nope, rope=rope),
        grid=(t // tm,),
        in_specs=[row(d), tab(heads * rope), tab(heads * rope), tab(rope), tab(rope)]
        + [_full(a.shape) for a in consts],
        out_specs=out_specs,
        out_shape=out_shape,
        compiler_params=_cparams(),
        name="mla_proj",
    )(x, cosq, sinq, cosk, sink, *consts)


def _mla_prompt_kernel(ql_ref, qr_ref, kc_ref, kr_ref, vt_ref, o_ref, m_ref, l_ref, acc_ref,
                       *, heads, tq, tk, scale):
    qi = pl.program_id(1)
    n_full = (qi * tq) // tk
    n_diag = max(1, tq // tk)
    sub = tk // vt_ref.shape[2]
    m_ref[...] = jnp.full_like(m_ref, -jnp.inf)
    l_ref[...] = jnp.zeros_like(l_ref)
    acc_ref[...] = jnp.zeros_like(acc_ref)
    scale2 = scale * math.log2(math.e)

    nqh = heads * tq
    ql = ql_ref[...].reshape(nqh, ql_ref.shape[2])
    qr = qr_ref[...].reshape(nqh, qr_ref.shape[2])

    def block(ki, masked):
        off = pl.multiple_of(ki * tk, tk)
        kc = kc_ref[pl.ds(off, tk), :]
        kr = kr_ref[pl.ds(off, tk), :]
        s = (_dot_nt(kc, ql) + _dot_nt(kr, qr)) * scale2
        if masked:
            key = off + lax.broadcasted_iota(jnp.int32, (tk, nqh), 0)
            qry = qi * tq + lax.broadcasted_iota(jnp.int32, (tk, nqh), 1) % tq
            s = jnp.where(key <= qry, s, -jnp.inf)
        m_prev = m_ref[...]
        m_new = jnp.maximum(m_prev, jnp.max(s, axis=0, keepdims=True))
        alpha = jnp.exp2(m_prev - m_new)
        p = jnp.exp2(s - m_new)
        l_ref[...] = alpha * l_ref[...] + jnp.sum(p, axis=0, keepdims=True)
        vt = jnp.concatenate([vt_ref[ki * sub + j] for j in range(sub)], axis=1)
        acc_ref[...] = alpha * acc_ref[...] + _dot(vt, p.astype(BF16))
        m_ref[...] = m_new

    def body(ki, carry):
        block(ki, False)
        return carry

    lax.fori_loop(0, n_full, body, 0)
    for j in range(n_diag):
        block(n_full + j, True)
    o_t = acc_ref[...] / l_ref[...]
    for h in range(heads):
        o_ref[h] = o_t[:, h * tq:(h + 1) * tq].T.astype(BF16)


def _mla_prompt_attn(qlat, qrope, ckvb, kropeb, ckv_t, *, nb, seqlen, scale):
    heads, t, lora = qlat.shape
    rope = qrope.shape[2]
    tile = ckv_t.shape[2]
    tq = ATTN_QUERY_TILE if seqlen % ATTN_QUERY_TILE == 0 else tile
    tk = ATTN_KEY_BLOCK if seqlen % ATTN_KEY_BLOCK == 0 else tile
    assert tq % tile == 0 and tk % tile == 0 and (tq % tk == 0 or tk % tq == 0)
    nq = seqlen // tq
    return pl.pallas_call(
        functools.partial(_mla_prompt_kernel, heads=heads, tq=tq, tk=tk, scale=scale),
        grid=(nb, nq),
        in_specs=[pl.BlockSpec((heads, tq, lora), lambda b, i: (0, b * nq + i, 0)),
                  pl.BlockSpec((heads, tq, rope), lambda b, i: (0, b * nq + i, 0)),
                  pl.BlockSpec((seqlen, lora), lambda b, i: (b, 0)),
                  pl.BlockSpec((seqlen, rope), lambda b, i: (b, 0)),
                  pl.BlockSpec((seqlen // tile, lora, tile), lambda b, i: (b, 0, 0))],
        out_specs=pl.BlockSpec((heads, tq, lora), lambda b, i: (0, b * nq + i, 0)),
        out_shape=jax.ShapeDtypeStruct((heads, t, lora), BF16),
        scratch_shapes=[pltpu.VMEM((1, heads * tq), F32), pltpu.VMEM((1, heads * tq), F32),
                        pltpu.VMEM((lora, heads * tq), F32)],
        compiler_params=_cparams(2),
        name="mla_prompt_attn",
    )(qlat, qrope, ckvb, kropeb, ckv_t)


def _mla_sample_kernel(pt_ref, ql_ref, qr_ref, kc_ref, kr_ref, lat_hbm, rope_hbm, o_ref,
                       lat_buf, rope_buf, kcb_ref, sem_lat, sem_rope, *, layer, npages, chunk, scale):
    b = pl.program_id(0)
    slot = b % 2

    def page_copies(seq, dst_slot, p):
        page = pt_ref[seq, p]
        return (pltpu.make_async_copy(lat_hbm.at[layer, page], lat_buf.at[dst_slot, p], sem_lat.at[dst_slot]),
                pltpu.make_async_copy(rope_hbm.at[layer, page], rope_buf.at[dst_slot, p], sem_rope.at[dst_slot]))

    def start_all(seq, dst_slot):
        def body(p, carry):
            for cp in page_copies(seq, dst_slot, p):
                cp.start()
            return carry
        lax.fori_loop(0, npages, body, 0)

    @pl.when(b == 0)
    def _():
        start_all(0, 0)

    @pl.when(b + 1 < pl.num_programs(0))
    def _():
        start_all(b + 1, 1 - slot)

    def wait_body(p, carry):
        for cp in page_copies(b, slot, p):
            cp.wait()
        return carry

    lax.fori_loop(0, npages, wait_body, 0)

    rows = ql_ref.shape[0] * ql_ref.shape[1]
    lora = ql_ref.shape[2]
    ql = ql_ref[...].reshape(rows, lora).astype(BF16)
    qr = qr_ref[...].reshape(rows, qr_ref.shape[2]).astype(BF16)

    ntok = kc_ref.shape[0]
    pad = PAGE_SIZE - ntok
    kc_own = jnp.concatenate([kc_ref[...], jnp.zeros((pad, lora), F32)], axis=0).astype(BF16)
    kr_own = jnp.concatenate([kr_ref[...], jnp.zeros((pad, kr_ref.shape[1]), F32)], axis=0).astype(BF16)
    s_own = (_dot_nt(ql, kc_own) + _dot_nt(qr, kr_own)) * scale
    q_tok = lax.broadcasted_iota(jnp.int32, s_own.shape, 0) % ntok
    k_tok = lax.broadcasted_iota(jnp.int32, s_own.shape, 1)
    scores = [jnp.where(k_tok <= q_tok, s_own, -jnp.inf)]
    ckeys = chunk * PAGE_SIZE
    for c in range(npages // chunk):
        keys = slice(c * ckeys, (c + 1) * ckeys)
        kcb_ref[keys, :] = lat_buf[slot, c * chunk:(c + 1) * chunk].reshape(ckeys, lora).astype(BF16)
        kr_t = jnp.concatenate([rope_buf[slot, c * chunk + k] for k in range(chunk)], axis=1).astype(BF16)
        scores.append((_dot_nt(ql, kcb_ref[keys, :]) + _dot(qr, kr_t)) * scale)
    m = functools.reduce(jnp.maximum, [jnp.max(s, axis=-1, keepdims=True) for s in scores])
    p = jnp.exp(scores[0] - m)
    l = jnp.sum(p, axis=-1, keepdims=True)
    acc = _dot(p.astype(BF16), kc_own)
    for c in range(npages // chunk):
        p = jnp.exp(scores[c + 1] - m)
        l = l + jnp.sum(p, axis=-1, keepdims=True)
        acc = acc + _dot(p.astype(BF16), kcb_ref[c * ckeys:(c + 1) * ckeys, :])
    o_ref[...] = (acc / l).reshape(o_ref.shape)


def _mla_sample_attn(qlat, qrope, ckv, krope, pool_lat, pool_rope_t, layer, page_table, *, seqlen, scale):
    heads, t, lora = qlat.shape
    rope = qrope.shape[2]
    nb, npages = page_table.shape
    chunk = math.gcd(npages, SAMPLE_KEY_PAGES)
    assert seqlen == SUBLANES
    grid_spec = pltpu.PrefetchScalarGridSpec(
        num_scalar_prefetch=1,
        grid=(nb,),
        in_specs=[pl.BlockSpec((heads, seqlen, lora), lambda b, pt: (0, b, 0)),
                  pl.BlockSpec((heads, seqlen, rope), lambda b, pt: (0, b, 0)),
                  pl.BlockSpec((seqlen, lora), lambda b, pt: (b, 0)),
                  pl.BlockSpec((seqlen, rope), lambda b, pt: (b, 0)),
                  pl.BlockSpec(memory_space=pl.ANY), pl.BlockSpec(memory_space=pl.ANY)],
        out_specs=pl.BlockSpec((heads, seqlen, lora), lambda b, pt: (0, b, 0)),
        scratch_shapes=[pltpu.VMEM((2, npages, PAGE_SIZE, lora), F32), pltpu.VMEM((2, npages, rope, PAGE_SIZE), F32),
                        pltpu.VMEM((npages * PAGE_SIZE, lora), BF16),
                        pltpu.SemaphoreType.DMA((2,)), pltpu.SemaphoreType.DMA((2,))],
    )
    return pl.pallas_call(
        functools.partial(_mla_sample_kernel, layer=layer, npages=npages, chunk=chunk, scale=scale),
        grid_spec=grid_spec,
        out_shape=jax.ShapeDtypeStruct((heads, t, lora), F32),
        compiler_params=_cparams(1),
        name="mla_sample_attn",
    )(page_table, qlat, qrope, ckv, krope, pool_lat, pool_rope_t)


def _mla_out_kernel(o_ref, x_ref, wuv_ref, wo_ref, pw_ref, out_ref, *, heads):
    parts = [_dot(o_ref[h].astype(BF16), wuv_ref[h]) for h in range(heads)]
    o = jnp.concatenate(parts, axis=1).astype(BF16)
    out_ref[...] = x_ref[...] + _rms(_dot(o, wo_ref[...]), pw_ref[...])


def _mla_out(o, x, wuv, wo, pw):
    heads, t, lora = o.shape
    d = x.shape[1]
    tm = _row_tile(t, 256)
    row = pl.BlockSpec((tm, d), lambda i: (i, 0))
    return pl.pallas_call(
        functools.partial(_mla_out_kernel, heads=heads),
        grid=(t // tm,),
        in_specs=[pl.BlockSpec((heads, tm, lora), lambda i: (0, i, 0)), row,
                  _full(wuv.shape), _full(wo.shape), _full(pw.shape)],
        out_specs=row,
        out_shape=jax.ShapeDtypeStruct((t, d), F32),
        compiler_params=_cparams(),
        name="mla_out",
    )(o, x, wuv, wo, pw)


def _pad_lanes(a, width=LANES):
    return jnp.pad(a, [(0, 0)] * (a.ndim - 1) + [(0, width - a.shape[-1])])


def _hist8(prev):
    return jnp.pad(prev, ((0, 0), (SUBLANES - prev.shape[1], 0), (0, 0)))


def _rope_tables(pos, rope, heads):
    half = rope // 2
    inv = ROPE_THETA ** (-jnp.arange(half, dtype=F32) / half)

    def table(width):
        col = jnp.arange(width)
        ang = pos.astype(F32)[:, None] * inv[col % half][None, :]
        sign = jnp.where(col % rope < half, -1.0, 1.0).astype(F32)
        return jnp.cos(ang), jnp.sin(ang) * sign[None, :]

    return table(heads * rope) + table(rope)


def _swap_halves(w, rope):
    k, n = w.shape
    w = w.reshape(k, n // rope, 2, rope // 2)
    return w[:, :, ::-1, :].reshape(k, n)


def _s5_params(a_re, a_im, log_step, b_re, b_im, c_re, c_im, d_model):
    g, s = a_re.shape
    grp = d_model // g
    delta = jnp.exp(log_step)[:, None]
    mag = jnp.exp(a_re * delta)
    ang = a_im * delta
    lam_re, lam_im = mag * jnp.cos(ang), mag * jnp.sin(ang)
    den = a_re * a_re + a_im * a_im
    nr, ni = lam_re - 1.0, lam_im
    f_re = (nr * a_re + ni * a_im) / den
    f_im = (ni * a_re - nr * a_im) / den
    bb_re = f_re[..., None] * b_re - f_im[..., None] * b_im
    bb_im = f_re[..., None] * b_im + f_im[..., None] * b_re
    kw = 256 if d_model % 256 == 0 else d_model
    kb = d_model // kw
    gpb = kw // grp

    def block_diag_in(bb):
        bb = bb.reshape(kb, gpb, s, grp)
        eye = jnp.eye(gpb, dtype=F32)
        w = jnp.einsum("kgsc,gh->kgchs", bb, eye)
        return w.reshape(kb, gpb * grp, gpb * s).astype(BF16)

    def block_diag_out(cc):
        cc = cc.reshape(kb, gpb, grp, s)
        eye = jnp.eye(gpb, dtype=F32)
        w = jnp.einsum("kgcs,gh->kgshc", cc, eye)
        return w.reshape(kb, gpb * s, gpb * grp).astype(BF16)

    nst = g * s // LANES
    return dict(wbr=block_diag_in(bb_re), wbi=block_diag_in(bb_im),
                wcr=block_diag_out(c_re), wci=block_diag_out(c_im),
                lam_re=lam_re.reshape(nst, LANES), lam_im=lam_im.reshape(nst, LANES))


def kernel(x_prompt, x_sample, state_ssd, state_ssd_conv, state_s5_re, state_s5_im, cache_mla_latent, cache_mla_krope, page_table, state_ffn_conv, norm_mix_pre, norm_mix_post, norm_ffn_pre, norm_ffn_post, ssd_w_in, ssd_conv_w, ssd_conv_b, ssd_dt_bias, ssd_a_log, ssd_d, ssd_norm, ssd_w_out, s5_a_re, s5_a_im, s5_log_step, s5_b_re, s5_b_im, s5_c_re, s5_c_im, s5_d, s5_w_val, s5_w_gate, mla_w_dq, mla_q_norm, mla_w_uq, mla_w_dkv, mla_kv_norm, mla_w_uk, mla_w_uv, mla_w_o, ffn_w_gate, ffn_w_up, ffn_conv_w, ffn_conv_b, ffn_w_down):
    b_p, l_p, d = x_prompt.shape
    b_s, l_s, _ = x_sample.shape
    depth = norm_mix_pre.shape[0]
    past_len = page_table.shape[1] * PAGE_SIZE
    groups_of = {"p": (b_p, l_p), "s": (b_s, l_s)}
    hid = {"p": x_prompt.reshape(b_p * l_p, d), "s": x_sample.reshape(b_s * l_s, d)}
    outs = {k: {n: [] for n in ("ssd_c", "s5r", "s5i", "lat", "kr", "ffc")} for k in ("p", "s")}
    ssd_h = {"p": None, "s": None}

    _, _, nh, hd, ds = state_ssd.shape
    di = nh * hd
    cd = state_ssd_conv.shape[-1]
    ssd_groups = (cd - di) // (2 * ds)
    s5_g, s5_s = s5_a_re.shape[1:]
    nst = s5_g * s5_s // LANES
    lora, heads, nope = mla_w_uk.shape[1:]
    rope = cache_mla_krope.shape[-1]
    mla_scale = 1.0 / math.sqrt(nope + rope)
    row1 = lambda a: a.reshape(1, -1)
    ffn_wg, ffn_wu, ffn_wd = ffn_w_gate.astype(BF16), ffn_w_up.astype(BF16), ffn_w_down.astype(BF16)

    for i in range(depth):
        kind, j = i % N_MIXERS, i // N_MIXERS
        if kind == 0:
            w_in = ssd_w_in[j].astype(BF16)
            wz, wx, wdt = w_in[:, :di], w_in[:, di:di + cd], _pad_lanes(w_in[:, di + cd:])
            dtb, alog = (_pad_lanes(row1(a[j])) for a in (ssd_dt_bias, ssd_a_log))
            dsk = jnp.repeat(ssd_d[j], hd).reshape(1, di)
            wo = ssd_w_out[j].astype(BF16)
            for k, (nb, sl) in groups_of.items():
                if k == "p":
                    prev8 = jnp.zeros((nb * SUBLANES, cd), F32)
                    h0, h0_layer = jnp.zeros((1, nb, nh, hd, ds), F32), 0
                else:
                    prev8 = _hist8(state_ssd_conv[j]).reshape(nb * SUBLANES, cd)
                    h0, h0_layer = state_ssd, j
                z, xact, dt, cnew = _ssd_in(hid[k], prev8, row1(norm_mix_pre[i]), wz, wx, wdt, ssd_conv_w[j],
                                            row1(ssd_conv_b[j]), dtb, nb=nb, seqlen=sl)
                y, ssd_h[k] = _ssd_scan(xact, dt, h0, h0_layer, alog, dsk, ssd_h[k], nb=nb, seqlen=sl,
                                        groups=ssd_groups)
                hid[k] = _ssd_out(y, z, hid[k], row1(ssd_norm[j]), wo, row1(norm_mix_post[i]), groups=ssd_groups)
                outs[k]["ssd_c"].append(cnew.reshape(nb, SUBLANES, cd)[:, SUBLANES - (SSD_CONV - 1):, :])
        elif kind == 1:
            sp = _s5_params(s5_a_re[j], s5_a_im[j], s5_log_step[j], s5_b_re[j], s5_b_im[j], s5_c_re[j], s5_c_im[j], d)
            wv, wg = s5_w_val[j].astype(BF16), s5_w_gate[j].astype(BF16)
            for k, (nb, sl) in groups_of.items():
                if k == "p":
                    h0r = h0i = jnp.zeros((nb, nst, LANES), F32)
                    lamr, lami = sp["lam_re"], sp["lam_im"]
                else:
                    h0r, h0i = state_s5_re[j].reshape(nb, nst * LANES), state_s5_im[j].reshape(nb, nst * LANES)
                    lamr, lami = sp["lam_re"].reshape(1, -1), sp["lam_im"].reshape(1, -1)
                hid[k], hr, hi = _s5(hid[k], h0r, h0i, row1(norm_mix_pre[i]), sp["wbr"], sp["wbi"], lamr, lami,
                                     sp["wcr"], sp["wci"], row1(s5_d[j]), wv, wg, row1(norm_mix_post[i]),
                                     nb=nb, seqlen=sl)
                outs[k]["s5r"].append(hr.reshape(nb, s5_g, s5_s))
                outs[k]["s5i"].append(hi.reshape(nb, s5_g, s5_s))
        else:
            w_uq = mla_w_uq[j].reshape(-1, heads, nope + rope)
            wqn = w_uq[:, :, :nope].reshape(-1, heads * nope).astype(BF16)
            wqr = w_uq[:, :, nope:].reshape(-1, heads * rope).astype(BF16)
            wqrs = _swap_halves(wqr, rope)
            wkv = mla_w_dkv[j][:, :lora].astype(BF16)
            wkr = mla_w_dkv[j][:, lora:].astype(BF16)
            wkrs = _swap_halves(wkr, rope)
            wukt = jnp.transpose(mla_w_uk[j], (1, 2, 0)).astype(BF16)
            wuv = jnp.transpose(mla_w_uv[j], (1, 0, 2)).astype(BF16)
            wdq, wo = mla_w_dq[j].astype(BF16), mla_w_o[j].astype(BF16)
            for k, (nb, sl) in groups_of.items():
                t = nb * sl
                if k == "p":
                    tm = _row_tile(sl, 256)
                    tabs = _rope_tables(jnp.arange(sl), rope, heads)
                    pos_tiles = sl // tm
                else:
                    tm = _row_tile(t, 256)
                    tabs = _rope_tables(past_len + jnp.arange(tm) % sl, rope, heads)
                    pos_tiles = 1
                proj = _mla_proj(
                    hid[k], tabs, row1(norm_mix_pre[i]), wdq, row1(mla_q_norm[j]), wqn, wqr, wqrs, wukt, wkv, wkr, wkrs,
                    row1(mla_kv_norm[j]), pos_tiles=pos_tiles, tm=tm, for_prompt=k == "p")
                qlat, qrope, ckv, kr = proj[:4]
                if k == "p":
                    o = _mla_prompt_attn(qlat, qrope, *proj[4:], nb=nb, seqlen=sl, scale=mla_scale)
                else:
                    o = _mla_sample_attn(qlat, qrope, ckv, kr, cache_mla_latent, jnp.swapaxes(cache_mla_krope, 2, 3),
                                         j, page_table, seqlen=sl, scale=mla_scale)
                hid[k] = _mla_out(o, hid[k], wuv, wo, row1(norm_mix_post[i]))
                outs[k]["lat"].append(ckv.reshape(nb, sl, lora))
                outs[k]["kr"].append(kr.reshape(nb, sl, rope))
        f = ffn_wg.shape[2]
        for k, (nb, sl) in groups_of.items():
            if k == "p":
                prev8 = jnp.zeros((nb * SUBLANES, f), F32)
            else:
                prev8 = _hist8(state_ffn_conv[i]).reshape(nb * SUBLANES, f)
            hid[k], cnew = _ffn(hid[k], prev8, row1(norm_ffn_pre[i]), ffn_wg, ffn_wu, ffn_conv_w[i],
                                row1(ffn_conv_b[i]), ffn_wd, row1(norm_ffn_post[i]), i, nb=nb, seqlen=sl)
            outs[k]["ffc"].append(cnew.reshape(nb, SUBLANES, f)[:, SUBLANES - (FFN_CONV - 1):, :])

    res = [hid["p"].reshape(b_p, l_p, d), hid["s"].reshape(b_s, l_s, d)]
    for k in ("p", "s"):
        res += [ssd_h[k]] + [jnp.stack(outs[k][n]) for n in ("ssd_c", "s5r", "s5i", "lat", "kr", "ffc")]
    return tuple(res)
```

```python
import functools
import math

import jax
import jax.numpy as jnp
from jax import lax
from jax.experimental import pallas as pl
from jax.experimental.pallas import tpu as pltpu

F32 = jnp.float32
BF16 = jnp.bfloat16
RMS_EPS = 1e-6
ROPE_THETA = 10000.0
LOG2E = math.log2(math.e)
PAGE_SIZE = 128
N_MIXERS = 3

SUBLANES = 8
LANES = 128
VMEM_LIMIT_BYTES = 56 * 1024 * 1024

SSD_CONV = 4
FFN_CONV = 3
CONV_CHUNK = 1024
ATTN_QUERY_TILE = 256
ATTN_KEY_BLOCK = 512
SAMPLE_KEY_PAGES = 16
SHORT_SEQ_CHUNK = 16


def _cparams(n_axes=1):
    return pltpu.CompilerParams(dimension_semantics=("arbitrary",) * n_axes,
                                vmem_limit_bytes=VMEM_LIMIT_BYTES)


def _full(shape):
    nd = len(shape)
    return pl.BlockSpec(shape, lambda i: (0,) * nd)


def _layer_of(stacked, layer):
    nd = stacked.ndim - 1
    return pl.BlockSpec((None,) + stacked.shape[1:], lambda i: (layer,) + (0,) * nd)


def _row_tile(n, pref):
    t = min(n, pref)
    while n % t or t % SUBLANES:
        t -= SUBLANES
    assert t > 0
    return t


def _rms(x, w):
    return x * lax.rsqrt(jnp.mean(x * x, axis=-1, keepdims=True) + RMS_EPS) * w


def _sigmoid(x):
    return 0.5 * jnp.tanh(0.5 * x) + 0.5


def _silu(x):
    h = 0.5 * x
    return h * jnp.tanh(h) + h


def _softplus(x):
    return jnp.maximum(x, 0.0) + jnp.log1p(jnp.exp(-jnp.abs(x)))


def _dot(a, b):
    return jnp.dot(a, b, preferred_element_type=F32)


def _dot_nt(a, b):
    return lax.dot_general(a, b, (((1,), (1,)), ((), ())), preferred_element_type=F32)


def _dot_tn(a, b):
    return lax.dot_general(a, b, (((0,), (0,)), ((), ())), preferred_element_type=F32)


def _shifted_rows(x, prev8, s, seg):
    if s == 0:
        return x
    rows = x.shape[0]
    rolled = pltpu.roll(x, s, axis=0)
    if seg == rows:
        head = jnp.where(lax.broadcasted_iota(jnp.int32, (SUBLANES, x.shape[1]), 0) < s,
                         pltpu.roll(prev8, s, axis=0), rolled[:SUBLANES])
        if rows == SUBLANES:
            return head
        return jnp.concatenate([head, rolled[SUBLANES:]], axis=0)
    assert seg == SUBLANES
    hist = pltpu.roll(prev8, (rows + s - SUBLANES) % rows, axis=0) if rows > SUBLANES else pltpu.roll(prev8, s, axis=0)
    t = lax.broadcasted_iota(jnp.int32, x.shape, 0) % SUBLANES
    return jnp.where(t < s, hist, rolled)


def _causal_conv(x, prev8, w, b, seg):
    k = w.shape[0]
    acc = b
    for j in range(k):
        acc = acc + _shifted_rows(x, prev8, k - 1 - j, seg) * w[j:j + 1, :]
    return acc


def _carried_conv(hist_ref, cols, x, w, b):
    acc = _causal_conv(x, hist_ref[:, cols], w, b, x.shape[0])
    hist_ref[:, cols] = x[x.shape[0] - SUBLANES:, :]
    return acc


def _col_chunks(n, width):
    return [slice(lo, min(lo + width, n)) for lo in range(0, n, width)]


def _ssd_in_kernel(x_ref, prev_ref, nw_ref, wz_ref, wx_ref, wdt_ref, cw_ref, cb_ref, dtb_ref,
                   z_ref, xact_ref, dt_ref, convnew_ref, hist_ref, *, seg, tiles_per_seq):
    tm = x_ref.shape[0]
    u = _rms(x_ref[...], nw_ref[...]).astype(BF16)
    if seg == tm:
        @pl.when(pl.program_id(0) % tiles_per_seq == 0)
        def _():
            hist_ref[...] = prev_ref[...]

    for cols in _col_chunks(wx_ref.shape[1], CONV_CHUNK):
        xbc = _dot(u, wx_ref[:, cols])
        if seg == tm:
            conv = _carried_conv(hist_ref, cols, xbc, cw_ref[:, cols], cb_ref[:, cols])
            convnew_ref[:, cols] = xbc[tm - SUBLANES:, :]
        else:
            conv = _causal_conv(xbc, prev_ref[:, cols], cw_ref[:, cols], cb_ref[:, cols], seg)
            convnew_ref[:, cols] = xbc
        xact_ref[:, cols] = _silu(conv)
    z_ref[...] = _dot(u, wz_ref[...])
    dt_ref[...] = _softplus(_dot(u, wdt_ref[...]) + dtb_ref[...])


def _ssd_in(x, prev8, nw, wz, wx, wdt, cw, cb, dtb, *, nb, seqlen):
    t, d = x.shape
    di, cd, dp = wz.shape[1], wx.shape[1], wdt.shape[1]
    if seqlen == SUBLANES:
        tm = _row_tile(t, 256)
        seg, tps = SUBLANES, 1
        hist_spec = pl.BlockSpec((tm, cd), lambda i: (i, 0))
    else:
        tm = _row_tile(seqlen, 256)
        seg, tps = tm, seqlen // tm
        hist_spec = pl.BlockSpec((SUBLANES, cd), lambda i: (i // tps, 0))
    row = lambda n: pl.BlockSpec((tm, n), lambda i: (i, 0))
    consts = [nw, wz, wx, wdt, cw, cb, dtb]
    return pl.pallas_call(
        functools.partial(_ssd_in_kernel, seg=seg, tiles_per_seq=tps),
        grid=(t // tm,),
        in_specs=[row(d), hist_spec] + [_full(a.shape) for a in consts],
        out_specs=[row(di), row(cd), row(dp), hist_spec],
        out_shape=[jax.ShapeDtypeStruct((t, di), F32), jax.ShapeDtypeStruct((t, cd), F32),
                   jax.ShapeDtypeStruct((t, dp), F32), jax.ShapeDtypeStruct((nb * SUBLANES, cd), F32)],
        scratch_shapes=[pltpu.VMEM((SUBLANES, cd), F32)],
        compiler_params=_cparams(),
        name="ssd_in",
    )(x, prev8, *consts)


def _cumsum_rows(x):
    rows = x.shape[0]
    row = lax.broadcasted_iota(jnp.int32, x.shape, 0)
    k = 1
    while k < rows:
        x = x + jnp.where(row >= k, pltpu.roll(x, k, axis=0), 0.0)
        k *= 2
    return x


def _ssd_scan_kernel(xact_ref, dt_ref, h0_ref, alog_ref, dsk_ref, expand_ref, *rest,
                     lt, q, groups, hpg, hd, ds, nc, n_prev):
    hprev_ref = rest[0] if n_prev else None
    y_ref, hfin_ref, xpad_ref, dtpad_ref, h_ref = rest[1 if n_prev else 0:]
    c = pl.program_id(1)
    di = groups * hpg * hd
    gn = groups * ds

    @pl.when(c == 0)
    def _():
        h_ref[...] = h0_ref[0, 0]

    if lt < q:
        xpad_ref[...] = jnp.zeros_like(xpad_ref)
        xpad_ref[0:lt, :] = xact_ref[...]
        dtpad_ref[...] = jnp.zeros_like(dtpad_ref)
        dtpad_ref[0:lt, :] = dt_ref[...]
        xact = xpad_ref[...]
        dt = dtpad_ref[...]
    else:
        xact = xact_ref[...]
        dt = dt_ref[...]
    xs = xact[:, :di]
    bm = xact[:, di:di + gn]
    cm = xact[:, di + gn:]
    a = -jnp.exp(alog_ref[...])
    cs = _cumsum_rows(dt * a) * LOG2E
    cs_last = cs[q - 1:q, :]
    nh = groups * hpg
    assert 3 * nh <= LANES
    ecs = jnp.exp2(cs)
    hi = ecs.astype(BF16).astype(F32)
    mid = (ecs - hi).astype(BF16).astype(F32)
    lo = ecs - hi - mid
    lane = lax.broadcasted_iota(jnp.int32, ecs.shape, 1)
    packed = jnp.where(lane < nh, hi, jnp.where(lane < 2 * nh, pltpu.roll(mid, nh, axis=1), pltpu.roll(lo, 2 * nh, axis=1)))
    ecs_x = _dot(packed.astype(BF16), expand_ref[...])
    e_last = jnp.exp2(cs_last)
    cs_t = cs.T
    dt_t = dt.T
    w_t = (jnp.exp2(cs_last - cs) * dt).T
    causal = (lax.broadcasted_iota(jnp.int32, (q, q), 0) >= lax.broadcasted_iota(jnp.int32, (q, q), 1))
    even = lax.broadcasted_iota(jnp.int32, (q, 2 * hd), 1) < hd
    assert 2 * hd == LANES and hpg % 2 == 0
    y_pairs = []
    for g in range(groups):
        bg = bm[:, g * ds:(g + 1) * ds].astype(BF16)
        cg = cm[:, g * ds:(g + 1) * ds].astype(BF16)
        cb = _dot_nt(cg, bg)
        hg = h_ref[g * hpg:(g + 1) * hpg].reshape(hpg * hd, ds)
        yoff = _dot_nt(cg, hg.astype(BF16))
        xs_g = xs[:, g * hpg * hd:(g + 1) * hpg * hd]
        for k in range(hpg // 2):
            pair = xs_g[:, k * LANES:(k + 1) * LANES].astype(BF16)
            yd = []
            for h in (g * hpg + 2 * k, g * hpg + 2 * k + 1):
                seg = cs[:, h:h + 1] - cs_t[h:h + 1, :]
                m = cb * jnp.exp2(jnp.where(causal, seg, -jnp.inf)) * dt_t[h:h + 1, :]
                yd.append(_dot(m.astype(BF16), pair))
            lo = g * hpg * hd + k * LANES
            y_pairs.append(jnp.where(even, yd[0], yd[1]) + yoff[:, k * LANES:(k + 1) * LANES] * ecs_x[:, lo:lo + LANES])
        xs_gt = xs_g.T
        xw = [xs_gt[r * hd:(r + 1) * hd, :] * w_t[g * hpg + r:g * hpg + r + 1, :] for r in range(hpg)]
        s_g = _dot(jnp.concatenate(xw, axis=0).astype(BF16), bg)
        for r in range(hpg):
            h = g * hpg + r
            h_ref[h] = h_ref[h] * e_last[:, h:h + 1] + s_g[r * hd:(r + 1) * hd, :]
    y = jnp.concatenate(y_pairs, axis=1) + dsk_ref[...] * xs
    y_ref[...] = y[0:lt, :]

    @pl.when(c == nc - 1)
    def _():
        for k in range(n_prev):
            hfin_ref[k, 0] = hprev_ref[k, 0]
        hfin_ref[n_prev, 0] = h_ref[...]


def _ssd_scan(xact, dt, h0, layer, alog, dsk, h_prev, *, nb, seqlen, groups):
    _, _, nh, hd, ds = h0.shape
    n_prev = 0 if h_prev is None else h_prev.shape[0]
    state_blk = lambda n: pl.BlockSpec((n, 1, nh, hd, ds), lambda b, c: (0, b, 0, 0, 0))
    piece_row = jnp.arange(LANES)[:, None]
    expand = ((jnp.arange(nh * hd)[None, :] // hd == piece_row % nh) & (piece_row < 3 * nh)).astype(BF16)
    cd = xact.shape[1]
    hpg = nh // groups
    di = nh * hd
    if seqlen >= 256:
        lt = q = 256
    else:
        lt, q = seqlen, SHORT_SEQ_CHUNK
    assert seqlen % lt == 0 and lt % SUBLANES == 0
    nc = seqlen // lt
    pad_rows = q if lt < q else SUBLANES
    kern = functools.partial(_ssd_scan_kernel, lt=lt, q=q, groups=groups, hpg=hpg, hd=hd, ds=ds, nc=nc,
                             n_prev=n_prev)
    row = lambda n: pl.BlockSpec((lt, n), lambda b, c: (b * nc + c, 0))
    par = lambda a: pl.BlockSpec(a.shape, lambda b, c: (0,) * a.ndim)
    return pl.pallas_call(
        kern,
        grid=(nb, nc),
        in_specs=[row(cd), row(dt.shape[1]),
                  pl.BlockSpec((1, 1, nh, hd, ds), lambda b, c: (layer, b, 0, 0, 0)),
                  par(alog), par(dsk), par(expand)] + ([state_blk(n_prev)] if n_prev else []),
        out_specs=[row(di), state_blk(n_prev + 1)],
        out_shape=[jax.ShapeDtypeStruct((nb * seqlen, di), F32),
                   jax.ShapeDtypeStruct((n_prev + 1, nb, nh, hd, ds), F32)],
        scratch_shapes=[pltpu.VMEM((pad_rows, cd), F32), pltpu.VMEM((pad_rows, dt.shape[1]), F32),
                        pltpu.VMEM((nh, hd, ds), F32)],
        compiler_params=_cparams(2),
        name="ssd_scan",
    )(xact, dt, h0, alog, dsk, expand, *([h_prev] if n_prev else []))


def _ssd_out_kernel(y_ref, z_ref, x_ref, gw_ref, wo_ref, pw_ref, o_ref, *, groups):
    yg = y_ref[...] * _silu(z_ref[...])
    gs = yg.shape[1] // groups
    parts = []
    for g in range(groups):
        blk = yg[:, g * gs:(g + 1) * gs]
        parts.append(blk * lax.rsqrt(jnp.mean(blk * blk, axis=-1, keepdims=True) + RMS_EPS))
    yn = (jnp.concatenate(parts, axis=1) * gw_ref[...]).astype(BF16)
    o_ref[...] = x_ref[...] + _rms(_dot(yn, wo_ref[...]), pw_ref[...])


def _ssd_out(y, z, x, gw, wo, pw, *, groups):
    t, d = x.shape
    tm = _row_tile(t, 256)
    row = lambda n: pl.BlockSpec((tm, n), lambda i: (i, 0))
    return pl.pallas_call(
        functools.partial(_ssd_out_kernel, groups=groups),
        grid=(t // tm,),
        in_specs=[row(y.shape[1]), row(z.shape[1]), row(d), _full(gw.shape), _full(wo.shape), _full(pw.shape)],
        out_specs=row(d),
        out_shape=jax.ShapeDtypeStruct((t, d), F32),
        compiler_params=_cparams(),
        name="ssd_out",
    )(y, z, x, gw, wo, pw)


def _ffn_kernel(x_ref, prev_ref, nw_ref, wg_ref, wu_ref, cw_ref, cb_ref, wd_ref, pw_ref,
                o_ref, convnew_ref, hist_ref, *, seg, tiles_per_seq):
    x = x_ref[...]
    tm = x.shape[0]
    u = _rms(x, nw_ref[...]).astype(BF16)
    if seg == tm:
        @pl.when(pl.program_id(0) % tiles_per_seq == 0)
        def _():
            hist_ref[...] = prev_ref[...]

    out = None
    for cols in _col_chunks(wg_ref.shape[1], CONV_CHUNK):
        g = _dot(u, wg_ref[:, cols])
        up = _dot(u, wu_ref[:, cols])
        if seg == tm:
            gc = _carried_conv(hist_ref, cols, g, cw_ref[:, cols], cb_ref[:, cols])
            convnew_ref[:, cols] = g[tm - SUBLANES:, :]
        else:
            gc = _causal_conv(g, prev_ref[:, cols], cw_ref[:, cols], cb_ref[:, cols], seg)
            convnew_ref[:, cols] = g
        part = _dot((_silu(gc) * up).astype(BF16), wd_ref[cols, :])
        out = part if out is None else out + part
    o_ref[...] = x + _rms(out, pw_ref[...])


def _ffn(x, prev8, nw, wg, wu, cw, cb, wd, pw, layer, *, nb, seqlen):
    t, d = x.shape
    f = wg.shape[2]
    if seqlen == SUBLANES:
        tm = _row_tile(t, 256)
        seg, tps = SUBLANES, 1
        prev_spec = pl.BlockSpec((tm, f), lambda i: (i, 0))
        new_spec = pl.BlockSpec((tm, f), lambda i: (i, 0))
    else:
        tm = _row_tile(seqlen, 512)
        seg, tps = tm, seqlen // tm
        prev_spec = pl.BlockSpec((SUBLANES, f), lambda i: (i // tps, 0))
        new_spec = pl.BlockSpec((SUBLANES, f), lambda i: (i // tps, 0))
    row = pl.BlockSpec((tm, d), lambda i: (i, 0))
    return pl.pallas_call(
        functools.partial(_ffn_kernel, seg=seg, tiles_per_seq=tps),
        grid=(t // tm,),
        in_specs=[row, prev_spec, _full(nw.shape), _layer_of(wg, layer), _layer_of(wu, layer), _full(cw.shape),
                  _full(cb.shape), _layer_of(wd, layer), _full(pw.shape)],
        out_specs=[row, new_spec],
        out_shape=[jax.ShapeDtypeStruct((t, d), F32), jax.ShapeDtypeStruct((nb * SUBLANES, f), F32)],
        scratch_shapes=[pltpu.VMEM((SUBLANES, f), F32)],
        compiler_params=_cparams(),
        name="conv_ffn",
    )(x, prev8, nw, wg, wu, cw, cb, wd, pw)


def _gelu(x):
    return 0.5 * x * (1.0 + lax.erf(x * (1.0 / math.sqrt(2.0))))


def _s5_token_pitch(nst):
    groups = -(-nst // SUBLANES)
    return SUBLANES * (groups + 1 - groups % 2)


def _s5_kernel(x_ref, h0r_ref, h0i_ref, nw_ref, wbr_ref, wbi_ref, lamr_ref, lami_ref, wcr_ref, wci_ref,
               dsk_ref, wv_ref, wg_ref, pw_ref, o_ref, hr_out_ref, hi_out_ref, sr_ref, si_ref, cr_ref, ci_ref,
               *, long_seq, tiles_per_seq, kb, nst):
    x = x_ref[...]
    tm, d = x.shape
    u = _rms(x, nw_ref[...])
    ub = u.astype(BF16)
    kw = d // kb
    st_per_kb = nst // kb
    if long_seq:
        i = pl.program_id(0)

        @pl.when(i % tiles_per_seq == 0)
        def _():
            cr_ref[...] = h0r_ref[0]
            ci_ref[...] = h0i_ref[0]

        pitch = _s5_token_pitch(nst)
        for b in range(kb):
            blk = ub[:, b * kw:(b + 1) * kw]
            pr = _dot(blk, wbr_ref[b])
            pi = _dot(blk, wbi_ref[b])
            for j in range(st_per_kb):
                st = b * st_per_kb + j
                sr_ref[pl.ds(st, tm, stride=pitch), :] = pr[:, j * LANES:(j + 1) * LANES]
                si_ref[pl.ds(st, tm, stride=pitch), :] = pi[:, j * LANES:(j + 1) * LANES]
        lr = lamr_ref[...]
        li = lami_ref[...]

        def step(t, carry):
            hr, hi = carry
            off = pl.multiple_of(t * pitch, SUBLANES)
            br = sr_ref[pl.ds(off, nst), :]
            bi = si_ref[pl.ds(off, nst), :]
            nr = lr * hr - li * hi + br
            ni = lr * hi + li * hr + bi
            sr_ref[pl.ds(off, nst), :] = nr
            si_ref[pl.ds(off, nst), :] = ni
            return nr, ni

        hr, hi = lax.fori_loop(0, tm, step, (cr_ref[...], ci_ref[...]), unroll=8)
        cr_ref[...] = hr
        ci_ref[...] = hi
        hr_out_ref[0] = hr
        hi_out_ref[0] = hi
        load_r = lambda st: sr_ref[pl.ds(st, tm, stride=pitch), :]
        load_i = lambda st: si_ref[pl.ds(st, tm, stride=pitch), :]
    else:
        nseq = tm // SUBLANES
        for b in range(kb):
            blk = ub[:, b * kw:(b + 1) * kw]
            pr = _dot(blk, wbr_ref[b])
            pi = _dot(blk, wbi_ref[b])
            for j in range(st_per_kb):
                sr_ref[b * st_per_kb + j] = pr[:, j * LANES:(j + 1) * LANES]
                si_ref[b * st_per_kb + j] = pi[:, j * LANES:(j + 1) * LANES]
        for st in range(nst):
            lanes = slice(st * LANES, (st + 1) * LANES)
            lr = lamr_ref[:, lanes]
            li = lami_ref[:, lanes]
            hr = h0r_ref[:, lanes]
            hi = h0i_ref[:, lanes]
            for t in range(SUBLANES):
                rows_t = pl.ds(t, nseq, stride=SUBLANES)
                br = sr_ref[st, rows_t, :]
                bi = si_ref[st, rows_t, :]
                hr, hi = lr * hr - li * hi + br, lr * hi + li * hr + bi
                sr_ref[st, rows_t, :] = hr
                si_ref[st, rows_t, :] = hi
            hr_out_ref[:, lanes] = hr
            hi_out_ref[:, lanes] = hi
        load_r = lambda st: sr_ref[st]
        load_i = lambda st: si_ref[st]

    ys = []
    for b in range(kb):
        hr_b = jnp.concatenate([load_r(b * st_per_kb + j) for j in range(st_per_kb)], axis=1).astype(BF16)
        hi_b = jnp.concatenate([load_i(b * st_per_kb + j) for j in range(st_per_kb)], axis=1).astype(BF16)
        ys.append(_dot(hr_b, wcr_ref[b]) - _dot(hi_b, wci_ref[b]))
    y = jnp.concatenate(ys, axis=1) + dsk_ref[...] * u
    gl = _gelu(y).astype(BF16)
    val = _dot(gl, wv_ref[...])
    gate = _dot(gl, wg_ref[...])
    out = val * _sigmoid(gate)
    o_ref[...] = x + _rms(out, pw_ref[...])


def _s5(x, h0r, h0i, nw, wbr, wbi, lamr, lami, wcr, wci, dsk, wv, wg, pw, *, nb, seqlen):
    t, d = x.shape
    kb = wbr.shape[0]
    nst = kb * wbr.shape[2] // LANES
    long_seq = seqlen != SUBLANES
    if long_seq:
        tm = _row_tile(seqlen, 256)
        tps = seqlen // tm
        st_spec = pl.BlockSpec((1, nst, LANES), lambda i: (i // tps, 0, 0))
        st_shape = jax.ShapeDtypeStruct((nb, nst, LANES), F32)
        rows = tm * _s5_token_pitch(nst)
        scratch = [pltpu.VMEM((rows, LANES), F32), pltpu.VMEM((rows, LANES), F32),
                   pltpu.VMEM((nst, LANES), F32), pltpu.VMEM((nst, LANES), F32)]
    else:
        tm = _row_tile(t, 256)
        tps = 1
        ns = tm // SUBLANES
        st_spec = pl.BlockSpec((ns, nst * LANES), lambda i: (i, 0))
        st_shape = jax.ShapeDtypeStruct((nb, nst * LANES), F32)
        scratch = [pltpu.VMEM((nst, tm, LANES), F32), pltpu.VMEM((nst, tm, LANES), F32),
                   pltpu.VMEM((SUBLANES, LANES), F32), pltpu.VMEM((SUBLANES, LANES), F32)]
    row = pl.BlockSpec((tm, d), lambda i: (i, 0))
    consts = [nw, wbr, wbi, lamr, lami, wcr, wci, dsk, wv, wg, pw]
    return pl.pallas_call(
        functools.partial(_s5_kernel, long_seq=long_seq, tiles_per_seq=tps, kb=kb, nst=nst),
        grid=(t // tm,),
        in_specs=[row, st_spec, st_spec] + [_full(a.shape) for a in consts],
        out_specs=[row, st_spec, st_spec],
        out_shape=[jax.ShapeDtypeStruct((t, d), F32), st_shape, st_shape],
        scratch_shapes=scratch,
        compiler_params=_cparams(),
        name="s5",
    )(x, h0r, h0i, *consts)


def _mla_proj_kernel(x_ref, cos_ref, sin_ref, nw_ref, wdq_ref, qn_ref, wqn_ref,
                     wqr_ref, wqrs_ref, wukt_ref, wkv_ref, wkr_ref, wkrs_ref, kvn_ref,
                     qlat_ref, qrope_ref, ckv_ref, krope_ref, *key_refs, heads, nope, rope):
    u = _rms(x_ref[...], nw_ref[...]).astype(BF16)
    cq = _rms(_dot(u, wdq_ref[...]), qn_ref[...]).astype(BF16)
    qn = _dot(cq, wqn_ref[...])
    reps = heads * rope // LANES
    qr = (_dot(cq, wqr_ref[...]) * jnp.tile(cos_ref[...], (1, reps))
          + _dot(cq, wqrs_ref[...]) * jnp.tile(sin_ref[...], (1, reps)))
    ckv = _rms(_dot(u, wkv_ref[...]), kvn_ref[...])
    kr = _dot(u, wkr_ref[...]) * cos_ref[:, :rope] + _dot(u, wkrs_ref[...]) * sin_ref[:, :rope]
    ckv_ref[...] = ckv
    krope_ref[...] = kr
    for h in range(heads):
        qlat_ref[h] = _dot(qn[:, h * nope:(h + 1) * nope].astype(BF16), wukt_ref[h]).astype(qlat_ref.dtype)
        qrope_ref[h] = qr[:, h * rope:(h + 1) * rope].astype(qrope_ref.dtype)
    if key_refs:
        ckvb_ref, kropeb_ref, ckvt_ref = key_refs
        ckvb_ref[...] = ckv.astype(BF16)
        kropeb_ref[...] = kr.astype(BF16)
        ckvt_ref[0] = ckv.T.astype(BF16)


def _mla_proj(x, tabs, nw, wdq, qn, wqn, wqr, wqrs, wukt, wkv, wkr, wkrs, kvn, *, pos_tiles, tm, for_prompt):
    t, d = x.shape
    heads, nope, lora = wukt.shape
    rope = wkr.shape[1]
    cos_tab, sin_tab = tabs
    assert (heads * rope) % LANES == 0 and LANES % rope == 0
    row = lambda n: pl.BlockSpec((tm, n), lambda i: (i, 0))
    tab = lambda n: pl.BlockSpec((tm, n), lambda i: (i % pos_tiles, 0))
    consts = [nw, wdq, qn, wqn, wqr, wqrs, wukt, wkv, wkr, wkrs, kvn]
    qdtype = BF16 if for_prompt else F32
    out_specs = [pl.BlockSpec((heads, tm, lora), lambda i: (0, i, 0)),
                 pl.BlockSpec((heads, tm, rope), lambda i: (0, i, 0)), row(lora), row(rope)]
    out_shape = [jax.ShapeDtypeStruct((heads, t, lora), qdtype), jax.ShapeDtypeStruct((heads, t, rope), qdtype),
                 jax.ShapeDtypeStruct((t, lora), F32), jax.ShapeDtypeStruct((t, rope), F32)]
    if for_prompt:
        out_specs += [row(lora), row(rope), pl.BlockSpec((1, lora, tm), lambda i: (i, 0, 0))]
        out_shape += [jax.ShapeDtypeStruct((t, lora), BF16), jax.ShapeDtypeStruct((t, rope), BF16),
                      jax.ShapeDtypeStruct((t // tm, lora, tm), BF16)]
    return pl.pallas_call(
        functools.partial(_mla_proj_kernel, heads=heads, nope=nope, rope=rope),
        grid=(t // tm,),
        in_specs=[row(d), tab(LANES), tab(LANES)]
        + [_full(a.shape) for a in consts],
        out_specs=out_specs,
        out_shape=out_shape,
        compiler_params=_cparams(),
        name="mla_proj",
    )(x, cos_tab, sin_tab, *consts)


def _mla_prompt_kernel(ql_ref, qr_ref, kc_ref, kr_ref, vt_ref, o_ref, m_ref, l_ref, acc_ref,
                       *, heads, tq, tk, scale):
    qi = pl.program_id(1)
    n_full = (qi * tq) // tk
    n_diag = max(1, tq // tk)
    sub = tk // vt_ref.shape[2]
    m_ref[...] = jnp.full_like(m_ref, -jnp.inf)
    l_ref[...] = jnp.zeros_like(l_ref)
    acc_ref[...] = jnp.zeros_like(acc_ref)
    scale2 = scale * math.log2(math.e)

    nqh = heads * tq
    ql = ql_ref[...].reshape(nqh, ql_ref.shape[2])
    qr = qr_ref[...].reshape(nqh, qr_ref.shape[2])

    def block(ki, masked):
        off = pl.multiple_of(ki * tk, tk)
        kc = kc_ref[pl.ds(off, tk), :]
        kr = kr_ref[pl.ds(off, tk), :]
        s = (_dot_nt(kc, ql) + _dot_nt(kr, qr)) * scale2
        if masked:
            key = off + lax.broadcasted_iota(jnp.int32, (tk, nqh), 0)
            qry = qi * tq + lax.broadcasted_iota(jnp.int32, (tk, nqh), 1) % tq
            s = jnp.where(key <= qry, s, -jnp.inf)
        m_prev = m_ref[...]
        m_new = jnp.maximum(m_prev, jnp.max(s, axis=0, keepdims=True))
        alpha = jnp.exp2(m_prev - m_new)
        p = jnp.exp2(s - m_new)
        l_ref[...] = alpha * l_ref[...] + jnp.sum(p, axis=0, keepdims=True)
        vt = jnp.concatenate([vt_ref[ki * sub + j] for j in range(sub)], axis=1)
        acc_ref[...] = alpha * acc_ref[...] + _dot(vt, p.astype(BF16))
        m_ref[...] = m_new

    def body(ki, carry):
        block(ki, False)
        return carry

    lax.fori_loop(0, n_full, body, 0)
    for j in range(n_diag):
        block(n_full + j, True)
    o_t = acc_ref[...] / l_ref[...]
    for h in range(heads):
        o_ref[h] = o_t[:, h * tq:(h + 1) * tq].T.astype(BF16)


def _mla_prompt_attn(qlat, qrope, ckvb, kropeb, ckv_t, *, nb, seqlen, scale):
    heads, t, lora = qlat.shape
    rope = qrope.shape[2]
    tile = ckv_t.shape[2]
    tq = ATTN_QUERY_TILE if seqlen % ATTN_QUERY_TILE == 0 else tile
    tk = ATTN_KEY_BLOCK if seqlen % ATTN_KEY_BLOCK == 0 else tile
    assert tq % tile == 0 and tk % tile == 0 and (tq % tk == 0 or tk % tq == 0)
    nq = seqlen // tq
    return pl.pallas_call(
        functools.partial(_mla_prompt_kernel, heads=heads, tq=tq, tk=tk, scale=scale),
        grid=(nb, nq),
        in_specs=[pl.BlockSpec((heads, tq, lora), lambda b, i: (0, b * nq + i, 0)),
                  pl.BlockSpec((heads, tq, rope), lambda b, i: (0, b * nq + i, 0)),
                  pl.BlockSpec((seqlen, lora), lambda b, i: (b, 0)),
                  pl.BlockSpec((seqlen, rope), lambda b, i: (b, 0)),
                  pl.BlockSpec((seqlen // tile, lora, tile), lambda b, i: (b, 0, 0))],
        out_specs=pl.BlockSpec((heads, tq, lora), lambda b, i: (0, b * nq + i, 0)),
        out_shape=jax.ShapeDtypeStruct((heads, t, lora), BF16),
        scratch_shapes=[pltpu.VMEM((1, heads * tq), F32), pltpu.VMEM((1, heads * tq), F32),
                        pltpu.VMEM((lora, heads * tq), F32)],
        compiler_params=_cparams(2),
        name="mla_prompt_attn",
    )(qlat, qrope, ckvb, kropeb, ckv_t)


def _mla_sample_kernel(pt_ref, ql_ref, qr_ref, kc_ref, kr_ref, lat_hbm, rope_hbm, o_ref,
                       lat_buf, rope_buf, kcb_ref, sem_lat, sem_rope, *, layer, npages, chunk, scale):
    b = pl.program_id(0)
    slot = b % 2

    def page_copies(seq, dst_slot, p):
        page = pt_ref[seq, p]
        return (pltpu.make_async_copy(lat_hbm.at[layer, page], lat_buf.at[dst_slot, p], sem_lat.at[dst_slot]),
                pltpu.make_async_copy(rope_hbm.at[layer, page], rope_buf.at[dst_slot, p], sem_rope.at[dst_slot]))

    def start_all(seq, dst_slot):
        def body(p, carry):
            for cp in page_copies(seq, dst_slot, p):
                cp.start()
            return carry
        lax.fori_loop(0, npages, body, 0, unroll=math.gcd(npages, 8))

    @pl.when(b == 0)
    def _():
        start_all(0, 0)

    @pl.when(b + 1 < pl.num_programs(0))
    def _():
        start_all(b + 1, 1 - slot)

    def wait_body(p, carry):
        for cp in page_copies(b, slot, p):
            cp.wait()
        return carry

    lax.fori_loop(0, npages, wait_body, 0, unroll=math.gcd(npages, 8))

    rows = ql_ref.shape[0] * ql_ref.shape[1]
    lora = ql_ref.shape[2]
    ql = ql_ref[...].reshape(rows, lora).astype(BF16)
    qr = qr_ref[...].reshape(rows, qr_ref.shape[2]).astype(BF16)

    ntok = kc_ref.shape[0]
    pad = PAGE_SIZE - ntok
    kc_own = jnp.concatenate([kc_ref[...], jnp.zeros((pad, lora), F32)], axis=0).astype(BF16)
    kr_own = jnp.concatenate([kr_ref[...], jnp.zeros((pad, kr_ref.shape[1]), F32)], axis=0).astype(BF16)
    s_own = (_dot_nt(ql, kc_own) + _dot_nt(qr, kr_own)) * scale
    q_tok = lax.broadcasted_iota(jnp.int32, s_own.shape, 0) % ntok
    k_tok = lax.broadcasted_iota(jnp.int32, s_own.shape, 1)
    scores = [jnp.where(k_tok <= q_tok, s_own, -jnp.inf)]
    ckeys = chunk * PAGE_SIZE
    for c in range(npages // chunk):
        keys = slice(c * ckeys, (c + 1) * ckeys)
        kcb_ref[keys, :] = lat_buf[slot, c * chunk:(c + 1) * chunk].reshape(ckeys, lora).astype(BF16)
        kr_t = jnp.concatenate([rope_buf[slot, c * chunk + k] for k in range(chunk)], axis=1).astype(BF16)
        scores.append((_dot_nt(ql, kcb_ref[keys, :]) + _dot(qr, kr_t)) * scale)
    m = functools.reduce(jnp.maximum, [jnp.max(s, axis=-1, keepdims=True) for s in scores])
    p = jnp.exp(scores[0] - m)
    l = jnp.sum(p, axis=-1, keepdims=True)
    acc = _dot(p.astype(BF16), kc_own)
    for c in range(npages // chunk):
        p = jnp.exp(scores[c + 1] - m)
        l = l + jnp.sum(p, axis=-1, keepdims=True)
        acc = acc + _dot(p.astype(BF16), kcb_ref[c * ckeys:(c + 1) * ckeys, :])
    o_ref[...] = (acc / l).reshape(o_ref.shape)


def _mla_sample_attn(qlat, qrope, ckv, krope, pool_lat, pool_rope_t, layer, page_table, *, seqlen, scale):
    heads, t, lora = qlat.shape
    rope = qrope.shape[2]
    nb, npages = page_table.shape
    chunk = math.gcd(npages, SAMPLE_KEY_PAGES)
    assert seqlen == SUBLANES
    grid_spec = pltpu.PrefetchScalarGridSpec(
        num_scalar_prefetch=1,
        grid=(nb,),
        in_specs=[pl.BlockSpec((heads, seqlen, lora), lambda b, pt: (0, b, 0)),
                  pl.BlockSpec((heads, seqlen, rope), lambda b, pt: (0, b, 0)),
                  pl.BlockSpec((seqlen, lora), lambda b, pt: (b, 0)),
                  pl.BlockSpec((seqlen, rope), lambda b, pt: (b, 0)),
                  pl.BlockSpec(memory_space=pl.ANY), pl.BlockSpec(memory_space=pl.ANY)],
        out_specs=pl.BlockSpec((heads, seqlen, lora), lambda b, pt: (0, b, 0)),
        scratch_shapes=[pltpu.VMEM((2, npages, PAGE_SIZE, lora), F32), pltpu.VMEM((2, npages, rope, PAGE_SIZE), F32),
                        pltpu.VMEM((npages * PAGE_SIZE, lora), BF16),
                        pltpu.SemaphoreType.DMA((2,)), pltpu.SemaphoreType.DMA((2,))],
    )
    return pl.pallas_call(
        functools.partial(_mla_sample_kernel, layer=layer, npages=npages, chunk=chunk, scale=scale),
        grid_spec=grid_spec,
        out_shape=jax.ShapeDtypeStruct((heads, t, lora), F32),
        compiler_params=_cparams(1),
        name="mla_sample_attn",
    )(page_table, qlat, qrope, ckv, krope, pool_lat, pool_rope_t)


def _mla_out_kernel(o_ref, x_ref, wuv_ref, wo_ref, pw_ref, out_ref, *, heads):
    parts = [_dot(o_ref[h].astype(BF16), wuv_ref[h]) for h in range(heads)]
    o = jnp.concatenate(parts, axis=1).astype(BF16)
    out_ref[...] = x_ref[...] + _rms(_dot(o, wo_ref[...]), pw_ref[...])


def _mla_out(o, x, wuv, wo, pw):
    heads, t, lora = o.shape
    d = x.shape[1]
    tm = _row_tile(t, 256)
    row = pl.BlockSpec((tm, d), lambda i: (i, 0))
    return pl.pallas_call(
        functools.partial(_mla_out_kernel, heads=heads),
        grid=(t // tm,),
        in_specs=[pl.BlockSpec((heads, tm, lora), lambda i: (0, i, 0)), row,
                  _full(wuv.shape), _full(wo.shape), _full(pw.shape)],
        out_specs=row,
        out_shape=jax.ShapeDtypeStruct((t, d), F32),
        compiler_params=_cparams(),
        name="mla_out",
    )(o, x, wuv, wo, pw)


def _pad_lanes(a, width=LANES):
    return jnp.pad(a, [(0, 0)] * (a.ndim - 1) + [(0, width - a.shape[-1])])


def _hist8(prev):
    return jnp.pad(prev, ((0, 0), (SUBLANES - prev.shape[1], 0), (0, 0)))


def _rope_tables(pos, rope, heads):
    half = rope // 2
    inv = ROPE_THETA ** (-jnp.arange(half, dtype=F32) / half)

    col = jnp.arange(LANES)
    ang = pos.astype(F32)[:, None] * inv[col % half][None, :]
    sign = jnp.where(col % rope < half, -1.0, 1.0).astype(F32)
    return jnp.cos(ang), jnp.sin(ang) * sign[None, :]


def _swap_halves(w, rope):
    k, n = w.shape
    w = w.reshape(k, n // rope, 2, rope // 2)
    return w[:, :, ::-1, :].reshape(k, n)


def _s5_params(a_re, a_im, log_step, b_re, b_im, c_re, c_im, d_model):
    g, s = a_re.shape
    grp = d_model // g
    delta = jnp.exp(log_step)[:, None]
    mag = jnp.exp(a_re * delta)
    ang = a_im * delta
    lam_re, lam_im = mag * jnp.cos(ang), mag * jnp.sin(ang)
    den = a_re * a_re + a_im * a_im
    nr, ni = lam_re - 1.0, lam_im
    f_re = (nr * a_re + ni * a_im) / den
    f_im = (ni * a_re - nr * a_im) / den
    bb_re = f_re[..., None] * b_re - f_im[..., None] * b_im
    bb_im = f_re[..., None] * b_im + f_im[..., None] * b_re
    kw = 256 if d_model % 256 == 0 else d_model
    kb = d_model // kw
    gpb = kw // grp

    def block_diag_in(bb):
        bb = bb.reshape(kb, gpb, s, grp)
        eye = jnp.eye(gpb, dtype=F32)
        w = jnp.einsum("kgsc,gh->kgchs", bb, eye)
        return w.reshape(kb, gpb * grp, gpb * s).astype(BF16)

    def block_diag_out(cc):
        cc = cc.reshape(kb, gpb, grp, s)
        eye = jnp.eye(gpb, dtype=F32)
        w = jnp.einsum("kgcs,gh->kgshc", cc, eye)
        return w.reshape(kb, gpb * s, gpb * grp).astype(BF16)

    nst = g * s // LANES
    return dict(wbr=block_diag_in(bb_re), wbi=block_diag_in(bb_im),
                wcr=block_diag_out(c_re), wci=block_diag_out(c_im),
                lam_re=lam_re.reshape(nst, LANES), lam_im=lam_im.reshape(nst, LANES))


def kernel(x_prompt, x_sample, state_ssd, state_ssd_conv, state_s5_re, state_s5_im, cache_mla_latent, cache_mla_krope, page_table, state_ffn_conv, norm_mix_pre, norm_mix_post, norm_ffn_pre, norm_ffn_post, ssd_w_in, ssd_conv_w, ssd_conv_b, ssd_dt_bias, ssd_a_log, ssd_d, ssd_norm, ssd_w_out, s5_a_re, s5_a_im, s5_log_step, s5_b_re, s5_b_im, s5_c_re, s5_c_im, s5_d, s5_w_val, s5_w_gate, mla_w_dq, mla_q_norm, mla_w_uq, mla_w_dkv, mla_kv_norm, mla_w_uk, mla_w_uv, mla_w_o, ffn_w_gate, ffn_w_up, ffn_conv_w, ffn_conv_b, ffn_w_down):
    b_p, l_p, d = x_prompt.shape
    b_s, l_s, _ = x_sample.shape
    depth = norm_mix_pre.shape[0]
    past_len = page_table.shape[1] * PAGE_SIZE
    groups_of = {"p": (b_p, l_p), "s": (b_s, l_s)}
    hid = {"p": x_prompt.reshape(b_p * l_p, d), "s": x_sample.reshape(b_s * l_s, d)}
    outs = {k: {n: [] for n in ("ssd_c", "s5r", "s5i", "lat", "kr", "ffc")} for k in ("p", "s")}
    ssd_h = {"p": None, "s": None}

    _, _, nh, hd, ds = state_ssd.shape
    di = nh * hd
    cd = state_ssd_conv.shape[-1]
    ssd_groups = (cd - di) // (2 * ds)
    s5_g, s5_s = s5_a_re.shape[1:]
    nst = s5_g * s5_s // LANES
    lora, heads, nope = mla_w_uk.shape[1:]
    rope = cache_mla_krope.shape[-1]
    mla_scale = 1.0 / math.sqrt(nope + rope)
    row1 = lambda a: a.reshape(1, -1)
    ffn_wg, ffn_wu, ffn_wd = ffn_w_gate.astype(BF16), ffn_w_up.astype(BF16), ffn_w_down.astype(BF16)

    for i in range(depth):
        kind, j = i % N_MIXERS, i // N_MIXERS
        if kind == 0:
            wz, wx = ssd_w_in[j, :, :di].astype(BF16), ssd_w_in[j, :, di:di + cd].astype(BF16)
            wdt = _pad_lanes(ssd_w_in[j, :, di + cd:]).astype(BF16)
            dtb, alog = (_pad_lanes(row1(a[j])) for a in (ssd_dt_bias, ssd_a_log))
            dsk = jnp.repeat(ssd_d[j], hd).reshape(1, di)
            wo = ssd_w_out[j].astype(BF16)
            for k, (nb, sl) in groups_of.items():
                if k == "p":
                    prev8 = jnp.zeros((nb * SUBLANES, cd), F32)
                    h0, h0_layer = jnp.zeros((1, nb, nh, hd, ds), F32), 0
                else:
                    prev8 = _hist8(state_ssd_conv[j]).reshape(nb * SUBLANES, cd)
                    h0, h0_layer = state_ssd, j
                z, xact, dt, cnew = _ssd_in(hid[k], prev8, row1(norm_mix_pre[i]), wz, wx, wdt, ssd_conv_w[j],
                                            row1(ssd_conv_b[j]), dtb, nb=nb, seqlen=sl)
                y, ssd_h[k] = _ssd_scan(xact, dt, h0, h0_layer, alog, dsk, ssd_h[k], nb=nb, seqlen=sl,
                                        groups=ssd_groups)
                hid[k] = _ssd_out(y, z, hid[k], row1(ssd_norm[j]), wo, row1(norm_mix_post[i]), groups=ssd_groups)
                outs[k]["ssd_c"].append(cnew.reshape(nb, SUBLANES, cd)[:, SUBLANES - (SSD_CONV - 1):, :])
        elif kind == 1:
            sp = _s5_params(s5_a_re[j], s5_a_im[j], s5_log_step[j], s5_b_re[j], s5_b_im[j], s5_c_re[j], s5_c_im[j], d)
            wv, wg = s5_w_val[j].astype(BF16), s5_w_gate[j].astype(BF16)
            for k, (nb, sl) in groups_of.items():
                if k == "p":
                    h0r = h0i = jnp.zeros((nb, nst, LANES), F32)
                    lamr, lami = sp["lam_re"], sp["lam_im"]
                else:
                    h0r, h0i = state_s5_re[j].reshape(nb, nst * LANES), state_s5_im[j].reshape(nb, nst * LANES)
                    lamr, lami = sp["lam_re"].reshape(1, -1), sp["lam_im"].reshape(1, -1)
                hid[k], hr, hi = _s5(hid[k], h0r, h0i, row1(norm_mix_pre[i]), sp["wbr"], sp["wbi"], lamr, lami,
                                     sp["wcr"], sp["wci"], row1(s5_d[j]), wv, wg, row1(norm_mix_post[i]),
                                     nb=nb, seqlen=sl)
                outs[k]["s5r"].append(hr.reshape(nb, s5_g, s5_s))
                outs[k]["s5i"].append(hi.reshape(nb, s5_g, s5_s))
        else:
            w_uq = mla_w_uq[j].reshape(-1, heads, nope + rope)
            wqn = w_uq[:, :, :nope].reshape(-1, heads * nope).astype(BF16)
            wqr = w_uq[:, :, nope:].reshape(-1, heads * rope).astype(BF16)
            wqrs = _swap_halves(wqr, rope)
            wkv = mla_w_dkv[j][:, :lora].astype(BF16)
            wkr = mla_w_dkv[j][:, lora:].astype(BF16)
            wkrs = _swap_halves(wkr, rope)
            wukt = jnp.transpose(mla_w_uk[j], (1, 2, 0)).astype(BF16)
            wuv = jnp.transpose(mla_w_uv[j], (1, 0, 2)).astype(BF16)
            wdq, wo = mla_w_dq[j].astype(BF16), mla_w_o[j].astype(BF16)
            for k, (nb, sl) in groups_of.items():
                t = nb * sl
                if k == "p":
                    tm = _row_tile(sl, 256)
                    tabs = _rope_tables(jnp.arange(sl), rope, heads)
                    pos_tiles = sl // tm
                else:
                    tm = _row_tile(t, 256)
                    tabs = _rope_tables(past_len + jnp.arange(tm) % sl, rope, heads)
                    pos_tiles = 1
                proj = _mla_proj(
                    hid[k], tabs, row1(norm_mix_pre[i]), wdq, row1(mla_q_norm[j]), wqn, wqr, wqrs, wukt, wkv, wkr, wkrs,
                    row1(mla_kv_norm[j]), pos_tiles=pos_tiles, tm=tm, for_prompt=k == "p")
                qlat, qrope, ckv, kr = proj[:4]
                if k == "p":
                    o = _mla_prompt_attn(qlat, qrope, *proj[4:], nb=nb, seqlen=sl, scale=mla_scale)
                else:
                    o = _mla_sample_attn(qlat, qrope, ckv, kr, cache_mla_latent, jnp.swapaxes(cache_mla_krope, 2, 3),
                                         j, page_table, seqlen=sl, scale=mla_scale)
                hid[k] = _mla_out(o, hid[k], wuv, wo, row1(norm_mix_post[i]))
                outs[k]["lat"].append(ckv.reshape(nb, sl, lora))
                outs[k]["kr"].append(kr.reshape(nb, sl, rope))
        f = ffn_wg.shape[2]
        for k, (nb, sl) in groups_of.items():
            if k == "p":
                prev8 = jnp.zeros((nb * SUBLANES, f), F32)
            else:
                prev8 = _hist8(state_ffn_conv[i]).reshape(nb * SUBLANES, f)
            hid[k], cnew = _ffn(hid[k], prev8, row1(norm_ffn_pre[i]), ffn_wg, ffn_wu, ffn_conv_w[i],
                                row1(ffn_conv_b[i]), ffn_wd, row1(norm_ffn_post[i]), i, nb=nb, seqlen=sl)
            outs[k]["ffc"].append(cnew.reshape(nb, SUBLANES, f)[:, SUBLANES - (FFN_CONV - 1):, :])

    res = [hid["p"].reshape(b_p, l_p, d), hid["s"].reshape(b_s, l_s, d)]
    for k in ("p", "s"):
        res += [ssd_h[k]] + [jnp.stack(outs[k][n]) for n in ("ssd_c", "s5r", "s5i", "lat", "kr", "ffc")]
    return tuple(res)
```

```python
import functools
import math

import jax
import jax.numpy as jnp
from jax import lax
from jax.experimental import pallas as pl
from jax.experimental.pallas import tpu as pltpu

F32 = jnp.float32
BF16 = jnp.bfloat16
RMS_EPS = 1e-6
ROPE_THETA = 10000.0
LOG2E = math.log2(math.e)
PAGE_SIZE = 128
N_MIXERS = 3

SUBLANES = 8
LANES = 128
VMEM_LIMIT_BYTES = 56 * 1024 * 1024

SSD_CONV = 4
FFN_CONV = 3
CONV_CHUNK = 1024
ATTN_QUERY_TILE = 256
ATTN_KEY_BLOCK = 512
SAMPLE_KEY_PAGES = 16
SHORT_SEQ_CHUNK = 16


def _cparams(n_axes=1):
    return pltpu.CompilerParams(dimension_semantics=("arbitrary",) * n_axes,
                                vmem_limit_bytes=VMEM_LIMIT_BYTES)


def _full(shape):
    nd = len(shape)
    return pl.BlockSpec(shape, lambda i: (0,) * nd)


def _layer_of(stacked, layer):
    nd = stacked.ndim - 1
    return pl.BlockSpec((None,) + stacked.shape[1:], lambda i: (layer,) + (0,) * nd)


def _row_tile(n, pref):
    t = min(n, pref)
    while n % t or t % SUBLANES:
        t -= SUBLANES
    assert t > 0
    return t


def _stream_dtype(seqlen):
    return F32 if seqlen == SUBLANES else BF16


def _rms(x, w):
    return x * lax.rsqrt(jnp.mean(x * x, axis=-1, keepdims=True) + RMS_EPS) * w


def _sigmoid(x):
    return 0.5 * jnp.tanh(0.5 * x) + 0.5


def _silu(x):
    h = 0.5 * x
    return h * jnp.tanh(h) + h


def _softplus(x):
    return jnp.maximum(x, 0.0) + jnp.log1p(jnp.exp(-jnp.abs(x)))


def _dot(a, b):
    return jnp.dot(a, b, preferred_element_type=F32)


def _dot_nt(a, b):
    return lax.dot_general(a, b, (((1,), (1,)), ((), ())), preferred_element_type=F32)


def _dot_tn(a, b):
    return lax.dot_general(a, b, (((0,), (0,)), ((), ())), preferred_element_type=F32)


def _shifted_rows(x, prev8, s, seg):
    if s == 0:
        return x
    rows = x.shape[0]
    rolled = pltpu.roll(x, s, axis=0)
    if seg == rows:
        head = jnp.where(lax.broadcasted_iota(jnp.int32, (SUBLANES, x.shape[1]), 0) < s,
                         pltpu.roll(prev8, s, axis=0), rolled[:SUBLANES])
        if rows == SUBLANES:
            return head
        return jnp.concatenate([head, rolled[SUBLANES:]], axis=0)
    assert seg == SUBLANES
    hist = pltpu.roll(prev8, (rows + s - SUBLANES) % rows, axis=0) if rows > SUBLANES else pltpu.roll(prev8, s, axis=0)
    t = lax.broadcasted_iota(jnp.int32, x.shape, 0) % SUBLANES
    return jnp.where(t < s, hist, rolled)


def _causal_conv(x, prev8, w, b, seg):
    k = w.shape[0]
    acc = b
    for j in range(k):
        acc = acc + _shifted_rows(x, prev8, k - 1 - j, seg) * w[j:j + 1, :]
    return acc


def _carried_conv(hist_ref, cols, x, w, b):
    acc = _causal_conv(x, hist_ref[:, cols], w, b, x.shape[0])
    hist_ref[:, cols] = x[x.shape[0] - SUBLANES:, :]
    return acc


def _col_chunks(n, width):
    return [slice(lo, min(lo + width, n)) for lo in range(0, n, width)]


def _ssd_in_kernel(x_ref, prev_ref, nw_ref, wz_ref, wx_ref, wdt_ref, cw_ref, cb_ref, dtb_ref,
                   z_ref, xact_ref, dt_ref, convnew_ref, hist_ref, *, seg, tiles_per_seq):
    tm = x_ref.shape[0]
    u = _rms(x_ref[...], nw_ref[...]).astype(BF16)
    if seg == tm:
        @pl.when(pl.program_id(0) % tiles_per_seq == 0)
        def _():
            hist_ref[...] = prev_ref[...]

    for cols in _col_chunks(wx_ref.shape[1], CONV_CHUNK):
        xbc = _dot(u, wx_ref[:, cols])
        if seg == tm:
            conv = _carried_conv(hist_ref, cols, xbc, cw_ref[:, cols], cb_ref[:, cols])
            convnew_ref[:, cols] = xbc[tm - SUBLANES:, :]
        else:
            conv = _causal_conv(xbc, prev_ref[:, cols], cw_ref[:, cols], cb_ref[:, cols], seg)
            convnew_ref[:, cols] = xbc
        xact_ref[:, cols] = _silu(conv)
    z_ref[...] = _dot(u, wz_ref[...]).astype(z_ref.dtype)
    dt_ref[...] = _softplus(_dot(u, wdt_ref[...]) + dtb_ref[...])


def _ssd_in(x, prev8, nw, wz, wx, wdt, cw, cb, dtb, *, nb, seqlen):
    t, d = x.shape
    di, cd, dp = wz.shape[1], wx.shape[1], wdt.shape[1]
    if seqlen == SUBLANES:
        tm = _row_tile(t, 256)
        seg, tps = SUBLANES, 1
        hist_spec = pl.BlockSpec((tm, cd), lambda i: (i, 0))
    else:
        tm = _row_tile(seqlen, 256)
        seg, tps = tm, seqlen // tm
        hist_spec = pl.BlockSpec((SUBLANES, cd), lambda i: (i // tps, 0))
    row = lambda n: pl.BlockSpec((tm, n), lambda i: (i, 0))
    consts = [nw, wz, wx, wdt, cw, cb, dtb]
    return pl.pallas_call(
        functools.partial(_ssd_in_kernel, seg=seg, tiles_per_seq=tps),
        grid=(t // tm,),
        in_specs=[row(d), hist_spec] + [_full(a.shape) for a in consts],
        out_specs=[row(di), row(cd), row(dp), hist_spec],
        out_shape=[jax.ShapeDtypeStruct((t, di), _stream_dtype(seqlen)), jax.ShapeDtypeStruct((t, cd), F32),
                   jax.ShapeDtypeStruct((t, dp), F32), jax.ShapeDtypeStruct((nb * SUBLANES, cd), F32)],
        scratch_shapes=[pltpu.VMEM((SUBLANES, cd), F32)],
        compiler_params=_cparams(),
        name="ssd_in",
    )(x, prev8, *consts)


def _cumsum_rows(x):
    rows = x.shape[0]
    row = lax.broadcasted_iota(jnp.int32, x.shape, 0)
    k = 1
    while k < rows:
        x = x + jnp.where(row >= k, pltpu.roll(x, k, axis=0), 0.0)
        k *= 2
    return x


def _ssd_scan_kernel(xact_ref, dt_ref, h0_ref, alog_ref, dsk_ref, expand_ref, *rest,
                     lt, q, groups, hpg, hd, ds, nc, n_prev):
    hprev_ref = rest[0] if n_prev else None
    y_ref, hfin_ref, xpad_ref, dtpad_ref, h_ref = rest[1 if n_prev else 0:]
    c = pl.program_id(1)
    di = groups * hpg * hd
    gn = groups * ds

    @pl.when(c == 0)
    def _():
        h_ref[...] = h0_ref[0, 0]

    if lt < q:
        xpad_ref[...] = jnp.zeros_like(xpad_ref)
        xpad_ref[0:lt, :] = xact_ref[...]
        dtpad_ref[...] = jnp.zeros_like(dtpad_ref)
        dtpad_ref[0:lt, :] = dt_ref[...]
        xact = xpad_ref[...]
        dt = dtpad_ref[...]
    else:
        xact = xact_ref[...]
        dt = dt_ref[...]
    xs = xact[:, :di]
    bm = xact[:, di:di + gn]
    cm = xact[:, di + gn:]
    a = -jnp.exp(alog_ref[...])
    cs = _cumsum_rows(dt * a) * LOG2E
    cs_last = cs[q - 1:q, :]
    nh = groups * hpg
    assert 3 * nh <= LANES
    ecs = jnp.exp2(cs)
    hi = ecs.astype(BF16).astype(F32)
    mid = (ecs - hi).astype(BF16).astype(F32)
    lo = ecs - hi - mid
    lane = lax.broadcasted_iota(jnp.int32, ecs.shape, 1)
    packed = jnp.where(lane < nh, hi, jnp.where(lane < 2 * nh, pltpu.roll(mid, nh, axis=1), pltpu.roll(lo, 2 * nh, axis=1)))
    ecs_x = _dot(packed.astype(BF16), expand_ref[...])
    e_last = jnp.exp2(cs_last)
    cs_t = cs.T
    dt_t = dt.T
    w_t = (jnp.exp2(cs_last - cs) * dt).T
    causal = (lax.broadcasted_iota(jnp.int32, (q, q), 0) >= lax.broadcasted_iota(jnp.int32, (q, q), 1))
    even = lax.broadcasted_iota(jnp.int32, (q, 2 * hd), 1) < hd
    assert 2 * hd == LANES and hpg % 2 == 0
    y_pairs = []
    for g in range(groups):
        bg = bm[:, g * ds:(g + 1) * ds].astype(BF16)
        cg = cm[:, g * ds:(g + 1) * ds].astype(BF16)
        cb = _dot_nt(cg, bg)
        hg = h_ref[g * hpg:(g + 1) * hpg].reshape(hpg * hd, ds)
        yoff = _dot_nt(cg, hg.astype(BF16))
        xs_g = xs[:, g * hpg * hd:(g + 1) * hpg * hd]
        for k in range(hpg // 2):
            pair = xs_g[:, k * LANES:(k + 1) * LANES].astype(BF16)
            yd = []
            for h in (g * hpg + 2 * k, g * hpg + 2 * k + 1):
                seg = cs[:, h:h + 1] - cs_t[h:h + 1, :]
                m = cb * jnp.exp2(jnp.where(causal, seg, -jnp.inf)) * dt_t[h:h + 1, :]
                yd.append(_dot(m.astype(BF16), pair))
            lo = g * hpg * hd + k * LANES
            y_pairs.append(jnp.where(even, yd[0], yd[1]) + yoff[:, k * LANES:(k + 1) * LANES] * ecs_x[:, lo:lo + LANES])
        xs_gt = xs_g.T
        xw = [xs_gt[r * hd:(r + 1) * hd, :] * w_t[g * hpg + r:g * hpg + r + 1, :] for r in range(hpg)]
        s_g = _dot(jnp.concatenate(xw, axis=0).astype(BF16), bg)
        for r in range(hpg):
            h = g * hpg + r
            h_ref[h] = h_ref[h] * e_last[:, h:h + 1] + s_g[r * hd:(r + 1) * hd, :]
    y = jnp.concatenate(y_pairs, axis=1) + dsk_ref[...] * xs
    y_ref[...] = y[0:lt, :].astype(y_ref.dtype)

    @pl.when(c == nc - 1)
    def _():
        for k in range(n_prev):
            hfin_ref[k, 0] = hprev_ref[k, 0]
        hfin_ref[n_prev, 0] = h_ref[...]


def _ssd_scan(xact, dt, h0, layer, alog, dsk, h_prev, *, nb, seqlen, groups):
    _, _, nh, hd, ds = h0.shape
    n_prev = 0 if h_prev is None else h_prev.shape[0]
    state_blk = lambda n: pl.BlockSpec((n, 1, nh, hd, ds), lambda b, c: (0, b, 0, 0, 0))
    piece_row = jnp.arange(LANES)[:, None]
    expand = ((jnp.arange(nh * hd)[None, :] // hd == piece_row % nh) & (piece_row < 3 * nh)).astype(BF16)
    cd = xact.shape[1]
    hpg = nh // groups
    di = nh * hd
    if seqlen >= 256:
        lt = q = 256
    else:
        lt, q = seqlen, SHORT_SEQ_CHUNK
    assert seqlen % lt == 0 and lt % SUBLANES == 0
    nc = seqlen // lt
    pad_rows = q if lt < q else SUBLANES
    kern = functools.partial(_ssd_scan_kernel, lt=lt, q=q, groups=groups, hpg=hpg, hd=hd, ds=ds, nc=nc,
                             n_prev=n_prev)
    row = lambda n: pl.BlockSpec((lt, n), lambda b, c: (b * nc + c, 0))
    par = lambda a: pl.BlockSpec(a.shape, lambda b, c: (0,) * a.ndim)
    return pl.pallas_call(
        kern,
        grid=(nb, nc),
        in_specs=[row(cd), row(dt.shape[1]),
                  pl.BlockSpec((1, 1, nh, hd, ds), lambda b, c: (layer, b, 0, 0, 0)),
                  par(alog), par(dsk), par(expand)] + ([state_blk(n_prev)] if n_prev else []),
        out_specs=[row(di), state_blk(n_prev + 1)],
        out_shape=[jax.ShapeDtypeStruct((nb * seqlen, di), _stream_dtype(seqlen)),
                   jax.ShapeDtypeStruct((n_prev + 1, nb, nh, hd, ds), F32)],
        scratch_shapes=[pltpu.VMEM((pad_rows, cd), F32), pltpu.VMEM((pad_rows, dt.shape[1]), F32),
                        pltpu.VMEM((nh, hd, ds), F32)],
        compiler_params=_cparams(2),
        name="ssd_scan",
    )(xact, dt, h0, alog, dsk, expand, *([h_prev] if n_prev else []))


def _ssd_out_kernel(y_ref, z_ref, x_ref, gw_ref, wo_ref, pw_ref, o_ref, *, groups):
    yg = y_ref[...].astype(F32) * _silu(z_ref[...].astype(F32))
    gs = yg.shape[1] // groups
    parts = []
    for g in range(groups):
        blk = yg[:, g * gs:(g + 1) * gs]
        parts.append(blk * lax.rsqrt(jnp.mean(blk * blk, axis=-1, keepdims=True) + RMS_EPS))
    yn = (jnp.concatenate(parts, axis=1) * gw_ref[...]).astype(BF16)
    o_ref[...] = x_ref[...] + _rms(_dot(yn, wo_ref[...]), pw_ref[...])


def _ssd_out(y, z, x, gw, wo, pw, *, groups):
    t, d = x.shape
    tm = _row_tile(t, 256)
    row = lambda n: pl.BlockSpec((tm, n), lambda i: (i, 0))
    return pl.pallas_call(
        functools.partial(_ssd_out_kernel, groups=groups),
        grid=(t // tm,),
        in_specs=[row(y.shape[1]), row(z.shape[1]), row(d), _full(gw.shape), _full(wo.shape), _full(pw.shape)],
        out_specs=row(d),
        out_shape=jax.ShapeDtypeStruct((t, d), F32),
        compiler_params=_cparams(),
        name="ssd_out",
    )(y, z, x, gw, wo, pw)


def _ffn_kernel(x_ref, prev_ref, nw_ref, wg_ref, wu_ref, cw_ref, cb_ref, wd_ref, pw_ref,
                o_ref, convnew_ref, hist_ref, *, seg, tiles_per_seq):
    x = x_ref[...]
    tm = x.shape[0]
    u = _rms(x, nw_ref[...]).astype(BF16)
    if seg == tm:
        @pl.when(pl.program_id(0) % tiles_per_seq == 0)
        def _():
            hist_ref[...] = prev_ref[...]

    out = None
    for cols in _col_chunks(wg_ref.shape[1], CONV_CHUNK):
        g = _dot(u, wg_ref[:, cols])
        up = _dot(u, wu_ref[:, cols])
        if seg == tm:
            gc = _carried_conv(hist_ref, cols, g, cw_ref[:, cols], cb_ref[:, cols])
            convnew_ref[:, cols] = g[tm - SUBLANES:, :]
        else:
            gc = _causal_conv(g, prev_ref[:, cols], cw_ref[:, cols], cb_ref[:, cols], seg)
            convnew_ref[:, cols] = g
        part = _dot((_silu(gc) * up).astype(BF16), wd_ref[cols, :])
        out = part if out is None else out + part
    o_ref[...] = x + _rms(out, pw_ref[...])


def _ffn(x, prev8, nw, wg, wu, cw, cb, wd, pw, layer, *, nb, seqlen):
    t, d = x.shape
    f = wg.shape[2]
    if seqlen == SUBLANES:
        tm = _row_tile(t, 256)
        seg, tps = SUBLANES, 1
        prev_spec = pl.BlockSpec((tm, f), lambda i: (i, 0))
        new_spec = pl.BlockSpec((tm, f), lambda i: (i, 0))
    else:
        tm = _row_tile(seqlen, 512)
        seg, tps = tm, seqlen // tm
        prev_spec = pl.BlockSpec((SUBLANES, f), lambda i: (i // tps, 0))
        new_spec = pl.BlockSpec((SUBLANES, f), lambda i: (i // tps, 0))
    row = pl.BlockSpec((tm, d), lambda i: (i, 0))
    return pl.pallas_call(
        functools.partial(_ffn_kernel, seg=seg, tiles_per_seq=tps),
        grid=(t // tm,),
        in_specs=[row, prev_spec, _full(nw.shape), _layer_of(wg, layer), _layer_of(wu, layer), _full(cw.shape),
                  _full(cb.shape), _layer_of(wd, layer), _full(pw.shape)],
        out_specs=[row, new_spec],
        out_shape=[jax.ShapeDtypeStruct((t, d), F32), jax.ShapeDtypeStruct((nb * SUBLANES, f), F32)],
        scratch_shapes=[pltpu.VMEM((SUBLANES, f), F32)],
        compiler_params=_cparams(),
        name="conv_ffn",
    )(x, prev8, nw, wg, wu, cw, cb, wd, pw)


def _gelu(x):
    return 0.5 * x * (1.0 + lax.erf(x * (1.0 / math.sqrt(2.0))))


def _s5_token_pitch(nst):
    groups = -(-nst // SUBLANES)
    return SUBLANES * (groups + 1 - groups % 2)


def _s5_kernel(x_ref, h0r_ref, h0i_ref, nw_ref, wbr_ref, wbi_ref, lamr_ref, lami_ref, wcr_ref, wci_ref,
               dsk_ref, wv_ref, wg_ref, pw_ref, o_ref, hr_out_ref, hi_out_ref, sr_ref, si_ref, cr_ref, ci_ref,
               *, long_seq, tiles_per_seq, kb, nst):
    x = x_ref[...]
    tm, d = x.shape
    u = _rms(x, nw_ref[...])
    ub = u.astype(BF16)
    kw = d // kb
    st_per_kb = nst // kb
    if long_seq:
        i = pl.program_id(0)

        @pl.when(i % tiles_per_seq == 0)
        def _():
            cr_ref[...] = h0r_ref[0]
            ci_ref[...] = h0i_ref[0]

        pitch = _s5_token_pitch(nst)
        for b in range(kb):
            blk = ub[:, b * kw:(b + 1) * kw]
            pr = _dot(blk, wbr_ref[b])
            pi = _dot(blk, wbi_ref[b])
            for j in range(st_per_kb):
                st = b * st_per_kb + j
                sr_ref[pl.ds(st, tm, stride=pitch), :] = pr[:, j * LANES:(j + 1) * LANES]
                si_ref[pl.ds(st, tm, stride=pitch), :] = pi[:, j * LANES:(j + 1) * LANES]
        lr = lamr_ref[...]
        li = lami_ref[...]

        def step(t, carry):
            hr, hi = carry
            off = pl.multiple_of(t * pitch, SUBLANES)
            br = sr_ref[pl.ds(off, nst), :]
            bi = si_ref[pl.ds(off, nst), :]
            nr = lr * hr - li * hi + br
            ni = lr * hi + li * hr + bi
            sr_ref[pl.ds(off, nst), :] = nr
            si_ref[pl.ds(off, nst), :] = ni
            return nr, ni

        hr, hi = lax.fori_loop(0, tm, step, (cr_ref[...], ci_ref[...]), unroll=8)
        cr_ref[...] = hr
        ci_ref[...] = hi
        hr_out_ref[0] = hr
        hi_out_ref[0] = hi
        load_r = lambda st: sr_ref[pl.ds(st, tm, stride=pitch), :]
        load_i = lambda st: si_ref[pl.ds(st, tm, stride=pitch), :]
    else:
        nseq = tm // SUBLANES
        for b in range(kb):
            blk = ub[:, b * kw:(b + 1) * kw]
            pr = _dot(blk, wbr_ref[b])
            pi = _dot(blk, wbi_ref[b])
            for j in range(st_per_kb):
                sr_ref[b * st_per_kb + j] = pr[:, j * LANES:(j + 1) * LANES]
                si_ref[b * st_per_kb + j] = pi[:, j * LANES:(j + 1) * LANES]
        for st in range(nst):
            lanes = slice(st * LANES, (st + 1) * LANES)
            lr = lamr_ref[:, lanes]
            li = lami_ref[:, lanes]
            hr = h0r_ref[:, lanes]
            hi = h0i_ref[:, lanes]
            for t in range(SUBLANES):
                rows_t = pl.ds(t, nseq, stride=SUBLANES)
                br = sr_ref[st, rows_t, :]
                bi = si_ref[st, rows_t, :]
                hr, hi = lr * hr - li * hi + br, lr * hi + li * hr + bi
                sr_ref[st, rows_t, :] = hr
                si_ref[st, rows_t, :] = hi
            hr_out_ref[:, lanes] = hr
            hi_out_ref[:, lanes] = hi
        load_r = lambda st: sr_ref[st]
        load_i = lambda st: si_ref[st]

    ys = []
    for b in range(kb):
        hr_b = jnp.concatenate([load_r(b * st_per_kb + j) for j in range(st_per_kb)], axis=1).astype(BF16)
        hi_b = jnp.concatenate([load_i(b * st_per_kb + j) for j in range(st_per_kb)], axis=1).astype(BF16)
        ys.append(_dot(hr_b, wcr_ref[b]) - _dot(hi_b, wci_ref[b]))
    y = jnp.concatenate(ys, axis=1) + dsk_ref[...] * u
    gl = _gelu(y).astype(BF16)
    val = _dot(gl, wv_ref[...])
    gate = _dot(gl, wg_ref[...])
    out = val * _sigmoid(gate)
    o_ref[...] = x + _rms(out, pw_ref[...])


def _s5(x, h0r, h0i, nw, wbr, wbi, lamr, lami, wcr, wci, dsk, wv, wg, pw, *, nb, seqlen):
    t, d = x.shape
    kb = wbr.shape[0]
    nst = kb * wbr.shape[2] // LANES
    long_seq = seqlen != SUBLANES
    if long_seq:
        tm = _row_tile(seqlen, 256)
        tps = seqlen // tm
        st_spec = pl.BlockSpec((1, nst, LANES), lambda i: (i // tps, 0, 0))
        st_shape = jax.ShapeDtypeStruct((nb, nst, LANES), F32)
        rows = tm * _s5_token_pitch(nst)
        scratch = [pltpu.VMEM((rows, LANES), F32), pltpu.VMEM((rows, LANES), F32),
                   pltpu.VMEM((nst, LANES), F32), pltpu.VMEM((nst, LANES), F32)]
    else:
        tm = _row_tile(t, 256)
        tps = 1
        ns = tm // SUBLANES
        st_spec = pl.BlockSpec((ns, nst * LANES), lambda i: (i, 0))
        st_shape = jax.ShapeDtypeStruct((nb, nst * LANES), F32)
        scratch = [pltpu.VMEM((nst, tm, LANES), F32), pltpu.VMEM((nst, tm, LANES), F32),
                   pltpu.VMEM((SUBLANES, LANES), F32), pltpu.VMEM((SUBLANES, LANES), F32)]
    row = pl.BlockSpec((tm, d), lambda i: (i, 0))
    consts = [nw, wbr, wbi, lamr, lami, wcr, wci, dsk, wv, wg, pw]
    return pl.pallas_call(
        functools.partial(_s5_kernel, long_seq=long_seq, tiles_per_seq=tps, kb=kb, nst=nst),
        grid=(t // tm,),
        in_specs=[row, st_spec, st_spec] + [_full(a.shape) for a in consts],
        out_specs=[row, st_spec, st_spec],
        out_shape=[jax.ShapeDtypeStruct((t, d), F32), st_shape, st_shape],
        scratch_shapes=scratch,
        compiler_params=_cparams(),
        name="s5",
    )(x, h0r, h0i, *consts)


def _mla_proj_kernel(x_ref, cos_ref, sin_ref, nw_ref, wdq_ref, qn_ref, wqn_ref,
                     wqr_ref, wqrs_ref, wukt_ref, wkv_ref, wkr_ref, wkrs_ref, kvn_ref,
                     qlat_ref, qrope_ref, ckv_ref, krope_ref, *key_refs, heads, nope, rope, qscale):
    u = _rms(x_ref[...], nw_ref[...]).astype(BF16)
    cq = _rms(_dot(u, wdq_ref[...]), qn_ref[...]).astype(BF16)
    qn = _dot(cq, wqn_ref[...])
    reps = heads * rope // LANES
    qr = (_dot(cq, wqr_ref[...]) * jnp.tile(cos_ref[...], (1, reps))
          + _dot(cq, wqrs_ref[...]) * jnp.tile(sin_ref[...], (1, reps)))
    ckv = _rms(_dot(u, wkv_ref[...]), kvn_ref[...])
    kr = _dot(u, wkr_ref[...]) * cos_ref[:, :rope] + _dot(u, wkrs_ref[...]) * sin_ref[:, :rope]
    ckv_ref[...] = ckv
    krope_ref[...] = kr
    for h in range(heads):
        ql = _dot(qn[:, h * nope:(h + 1) * nope].astype(BF16), wukt_ref[h])
        qlat_ref[h] = (ql * qscale).astype(qlat_ref.dtype)
        qrope_ref[h] = (qr[:, h * rope:(h + 1) * rope] * qscale).astype(qrope_ref.dtype)
    if key_refs:
        ckvb_ref, kropeb_ref, ckvt_ref = key_refs
        ckvb_ref[...] = ckv.astype(BF16)
        kropeb_ref[...] = kr.astype(BF16)
        ckvt_ref[0] = ckv.T.astype(BF16)


def _mla_proj(x, tabs, nw, wdq, qn, wqn, wqr, wqrs, wukt, wkv, wkr, wkrs, kvn, *, pos_tiles, tm, for_prompt,
              qscale):
    t, d = x.shape
    heads, nope, lora = wukt.shape
    rope = wkr.shape[1]
    cos_tab, sin_tab = tabs
    assert (heads * rope) % LANES == 0 and LANES % rope == 0
    row = lambda n: pl.BlockSpec((tm, n), lambda i: (i, 0))
    tab = lambda n: pl.BlockSpec((tm, n), lambda i: (i % pos_tiles, 0))
    consts = [nw, wdq, qn, wqn, wqr, wqrs, wukt, wkv, wkr, wkrs, kvn]
    qdtype = BF16 if for_prompt else F32
    out_specs = [pl.BlockSpec((heads, tm, lora), lambda i: (0, i, 0)),
                 pl.BlockSpec((heads, tm, rope), lambda i: (0, i, 0)), row(lora), row(rope)]
    out_shape = [jax.ShapeDtypeStruct((heads, t, lora), qdtype), jax.ShapeDtypeStruct((heads, t, rope), qdtype),
                 jax.ShapeDtypeStruct((t, lora), F32), jax.ShapeDtypeStruct((t, rope), F32)]
    if for_prompt:
        out_specs += [row(lora), row(rope), pl.BlockSpec((1, lora, tm), lambda i: (i, 0, 0))]
        out_shape += [jax.ShapeDtypeStruct((t, lora), BF16), jax.ShapeDtypeStruct((t, rope), BF16),
                      jax.ShapeDtypeStruct((t // tm, lora, tm), BF16)]
    return pl.pallas_call(
        functools.partial(_mla_proj_kernel, heads=heads, nope=nope, rope=rope, qscale=qscale),
        grid=(t // tm,),
        in_specs=[row(d), tab(LANES), tab(LANES)]
        + [_full(a.shape) for a in consts],
        out_specs=out_specs,
        out_shape=out_shape,
        compiler_params=_cparams(),
        name="mla_proj",
    )(x, cos_tab, sin_tab, *consts)


def _mla_prompt_kernel(ql_ref, qr_ref, kc_ref, kr_ref, vt_ref, o_ref, m_ref, l_ref, acc_ref,
                       *, heads, tq, tk):
    qi = pl.program_id(1)
    n_full = (qi * tq) // tk
    n_diag = max(1, tq // tk)
    sub = tk // vt_ref.shape[2]
    m_ref[...] = jnp.full_like(m_ref, -jnp.inf)
    l_ref[...] = jnp.zeros_like(l_ref)
    acc_ref[...] = jnp.zeros_like(acc_ref)

    nqh = heads * tq
    ql = ql_ref[...].reshape(nqh, ql_ref.shape[2])
    qr = qr_ref[...].reshape(nqh, qr_ref.shape[2])

    def block(ki, masked):
        off = pl.multiple_of(ki * tk, tk)
        kc = kc_ref[pl.ds(off, tk), :]
        kr = kr_ref[pl.ds(off, tk), :]
        s = _dot_nt(kc, ql) + _dot_nt(kr, qr)
        if masked:
            key = off + lax.broadcasted_iota(jnp.int32, (tk, nqh), 0)
            qry = qi * tq + lax.broadcasted_iota(jnp.int32, (tk, nqh), 1) % tq
            s = jnp.where(key <= qry, s, -jnp.inf)
        m_prev = m_ref[...]
        m_new = jnp.maximum(m_prev, jnp.max(s, axis=0, keepdims=True))
        alpha = jnp.exp2(m_prev - m_new)
        p = jnp.exp2(s - m_new)
        l_ref[...] = alpha * l_ref[...] + jnp.sum(p, axis=0, keepdims=True)
        vt = jnp.concatenate([vt_ref[ki * sub + j] for j in range(sub)], axis=1)
        acc_ref[...] = alpha * acc_ref[...] + _dot(vt, p.astype(BF16))
        m_ref[...] = m_new

    def body(ki, carry):
        block(ki, False)
        return carry

    lax.fori_loop(0, n_full, body, 0)
    for j in range(n_diag):
        block(n_full + j, True)
    o_t = acc_ref[...] / l_ref[...]
    for h in range(heads):
        o_ref[h] = o_t[:, h * tq:(h + 1) * tq].T.astype(BF16)


def _mla_prompt_attn(qlat, qrope, ckvb, kropeb, ckv_t, *, nb, seqlen):
    heads, t, lora = qlat.shape
    rope = qrope.shape[2]
    tile = ckv_t.shape[2]
    tq = ATTN_QUERY_TILE if seqlen % ATTN_QUERY_TILE == 0 else tile
    tk = ATTN_KEY_BLOCK if seqlen % ATTN_KEY_BLOCK == 0 else tile
    assert tq % tile == 0 and tk % tile == 0 and (tq % tk == 0 or tk % tq == 0)
    nq = seqlen // tq
    return pl.pallas_call(
        functools.partial(_mla_prompt_kernel, heads=heads, tq=tq, tk=tk),
        grid=(nb, nq),
        in_specs=[pl.BlockSpec((heads, tq, lora), lambda b, i: (0, b * nq + i, 0)),
                  pl.BlockSpec((heads, tq, rope), lambda b, i: (0, b * nq + i, 0)),
                  pl.BlockSpec((seqlen, lora), lambda b, i: (b, 0)),
                  pl.BlockSpec((seqlen, rope), lambda b, i: (b, 0)),
                  pl.BlockSpec((seqlen // tile, lora, tile), lambda b, i: (b, 0, 0))],
        out_specs=pl.BlockSpec((heads, tq, lora), lambda b, i: (0, b * nq + i, 0)),
        out_shape=jax.ShapeDtypeStruct((heads, t, lora), BF16),
        scratch_shapes=[pltpu.VMEM((1, heads * tq), F32), pltpu.VMEM((1, heads * tq), F32),
                        pltpu.VMEM((lora, heads * tq), F32)],
        compiler_params=_cparams(2),
        name="mla_prompt_attn",
    )(qlat, qrope, ckvb, kropeb, ckv_t)


def _mla_sample_kernel(pt_ref, ql_ref, qr_ref, kc_ref, kr_ref, lat_hbm, rope_hbm, o_ref,
                       lat_buf, rope_buf, kcb_ref, sem_lat, sem_rope, *, layer, npages, chunk, scale):
    b = pl.program_id(0)
    slot = b % 2

    def page_copies(seq, dst_slot, p):
        page = pt_ref[seq, p]
        return (pltpu.make_async_copy(lat_hbm.at[layer, page], lat_buf.at[dst_slot, p], sem_lat.at[dst_slot]),
                pltpu.make_async_copy(rope_hbm.at[layer, page], rope_buf.at[dst_slot, p], sem_rope.at[dst_slot]))

    def start_all(seq, dst_slot):
        def body(p, carry):
            for cp in page_copies(seq, dst_slot, p):
                cp.start()
            return carry
        lax.fori_loop(0, npages, body, 0, unroll=math.gcd(npages, 8))

    @pl.when(b == 0)
    def _():
        start_all(0, 0)

    @pl.when(b + 1 < pl.num_programs(0))
    def _():
        start_all(b + 1, 1 - slot)

    def wait_body(p, carry):
        for cp in page_copies(b, slot, p):
            cp.wait()
        return carry

    lax.fori_loop(0, npages, wait_body, 0, unroll=math.gcd(npages, 8))

    rows = ql_ref.shape[0] * ql_ref.shape[1]
    lora = ql_ref.shape[2]
    ql = ql_ref[...].reshape(rows, lora).astype(BF16)
    qr = qr_ref[...].reshape(rows, qr_ref.shape[2]).astype(BF16)

    ntok = kc_ref.shape[0]
    pad = PAGE_SIZE - ntok
    kc_own = jnp.concatenate([kc_ref[...], jnp.zeros((pad, lora), F32)], axis=0).astype(BF16)
    kr_own = jnp.concatenate([kr_ref[...], jnp.zeros((pad, kr_ref.shape[1]), F32)], axis=0).astype(BF16)
    s_own = (_dot_nt(ql, kc_own) + _dot_nt(qr, kr_own)) * scale
    q_tok = lax.broadcasted_iota(jnp.int32, s_own.shape, 0) % ntok
    k_tok = lax.broadcasted_iota(jnp.int32, s_own.shape, 1)
    scores = [jnp.where(k_tok <= q_tok, s_own, -jnp.inf)]
    ckeys = chunk * PAGE_SIZE
    for c in range(npages // chunk):
        keys = slice(c * ckeys, (c + 1) * ckeys)
        kcb_ref[keys, :] = lat_buf[slot, c * chunk:(c + 1) * chunk].reshape(ckeys, lora).astype(BF16)
        kr_t = jnp.concatenate([rope_buf[slot, c * chunk + k] for k in range(chunk)], axis=1).astype(BF16)
        scores.append((_dot_nt(ql, kcb_ref[keys, :]) + _dot(qr, kr_t)) * scale)
    m = functools.reduce(jnp.maximum, [jnp.max(s, axis=-1, keepdims=True) for s in scores])
    p = jnp.exp(scores[0] - m)
    l = jnp.sum(p, axis=-1, keepdims=True)
    acc = _dot(p.astype(BF16), kc_own)
    for c in range(npages // chunk):
        p = jnp.exp(scores[c + 1] - m)
        l = l + jnp.sum(p, axis=-1, keepdims=True)
        acc = acc + _dot(p.astype(BF16), kcb_ref[c * ckeys:(c + 1) * ckeys, :])
    o_ref[...] = (acc / l).reshape(o_ref.shape)


def _mla_sample_attn(qlat, qrope, ckv, krope, pool_lat, pool_rope_t, layer, page_table, *, seqlen, scale):
    heads, t, lora = qlat.shape
    rope = qrope.shape[2]
    nb, npages = page_table.shape
    chunk = math.gcd(npages, SAMPLE_KEY_PAGES)
    assert seqlen == SUBLANES
    grid_spec = pltpu.PrefetchScalarGridSpec(
        num_scalar_prefetch=1,
        grid=(nb,),
        in_specs=[pl.BlockSpec((heads, seqlen, lora), lambda b, pt: (0, b, 0)),
                  pl.BlockSpec((heads, seqlen, rope), lambda b, pt: (0, b, 0)),
                  pl.BlockSpec((seqlen, lora), lambda b, pt: (b, 0)),
                  pl.BlockSpec((seqlen, rope), lambda b, pt: (b, 0)),
                  pl.BlockSpec(memory_space=pl.ANY), pl.BlockSpec(memory_space=pl.ANY)],
        out_specs=pl.BlockSpec((heads, seqlen, lora), lambda b, pt: (0, b, 0)),
        scratch_shapes=[pltpu.VMEM((2, npages, PAGE_SIZE, lora), F32), pltpu.VMEM((2, npages, rope, PAGE_SIZE), F32),
                        pltpu.VMEM((npages * PAGE_SIZE, lora), BF16),
                        pltpu.SemaphoreType.DMA((2,)), pltpu.SemaphoreType.DMA((2,))],
    )
    return pl.pallas_call(
        functools.partial(_mla_sample_kernel, layer=layer, npages=npages, chunk=chunk, scale=scale),
        grid_spec=grid_spec,
        out_shape=jax.ShapeDtypeStruct((heads, t, lora), F32),
        compiler_params=_cparams(1),
        name="mla_sample_attn",
    )(page_table, qlat, qrope, ckv, krope, pool_lat, pool_rope_t)


def _mla_out_kernel(o_ref, x_ref, wuv_ref, wo_ref, pw_ref, out_ref, *, heads):
    parts = [_dot(o_ref[h].astype(BF16), wuv_ref[h]) for h in range(heads)]
    o = jnp.concatenate(parts, axis=1).astype(BF16)
    out_ref[...] = x_ref[...] + _rms(_dot(o, wo_ref[...]), pw_ref[...])


def _mla_out(o, x, wuv, wo, pw):
    heads, t, lora = o.shape
    d = x.shape[1]
    tm = _row_tile(t, 256)
    row = pl.BlockSpec((tm, d), lambda i: (i, 0))
    return pl.pallas_call(
        functools.partial(_mla_out_kernel, heads=heads),
        grid=(t // tm,),
        in_specs=[pl.BlockSpec((heads, tm, lora), lambda i: (0, i, 0)), row,
                  _full(wuv.shape), _full(wo.shape), _full(pw.shape)],
        out_specs=row,
        out_shape=jax.ShapeDtypeStruct((t, d), F32),
        compiler_params=_cparams(),
        name="mla_out",
    )(o, x, wuv, wo, pw)


def _pad_lanes(a, width=LANES):
    return jnp.pad(a, [(0, 0)] * (a.ndim - 1) + [(0, width - a.shape[-1])])


def _hist8(prev):
    return jnp.pad(prev, ((0, 0), (SUBLANES - prev.shape[1], 0), (0, 0)))


def _rope_tables(pos, rope, heads):
    half = rope // 2
    inv = ROPE_THETA ** (-jnp.arange(half, dtype=F32) / half)

    col = jnp.arange(LANES)
    ang = pos.astype(F32)[:, None] * inv[col % half][None, :]
    sign = jnp.where(col % rope < half, -1.0, 1.0).astype(F32)
    return jnp.cos(ang), jnp.sin(ang) * sign[None, :]


def _swap_halves(w, rope):
    k, n = w.shape
    w = w.reshape(k, n // rope, 2, rope // 2)
    return w[:, :, ::-1, :].reshape(k, n)


def _s5_params(a_re, a_im, log_step, b_re, b_im, c_re, c_im, d_model):
    g, s = a_re.shape
    grp = d_model // g
    delta = jnp.exp(log_step)[:, None]
    mag = jnp.exp(a_re * delta)
    ang = a_im * delta
    lam_re, lam_im = mag * jnp.cos(ang), mag * jnp.sin(ang)
    den = a_re * a_re + a_im * a_im
    nr, ni = lam_re - 1.0, lam_im
    f_re = (nr * a_re + ni * a_im) / den
    f_im = (ni * a_re - nr * a_im) / den
    bb_re = f_re[..., None] * b_re - f_im[..., None] * b_im
    bb_im = f_re[..., None] * b_im + f_im[..., None] * b_re
    kw = 256 if d_model % 256 == 0 else d_model
    kb = d_model // kw
    gpb = kw // grp

    def block_diag_in(bb):
        bb = bb.reshape(kb, gpb, s, grp)
        eye = jnp.eye(gpb, dtype=F32)
        w = jnp.einsum("kgsc,gh->kgchs", bb, eye)
        return w.reshape(kb, gpb * grp, gpb * s).astype(BF16)

    def block_diag_out(cc):
        cc = cc.reshape(kb, gpb, grp, s)
        eye = jnp.eye(gpb, dtype=F32)
        w = jnp.einsum("kgcs,gh->kgshc", cc, eye)
        return w.reshape(kb, gpb * s, gpb * grp).astype(BF16)

    nst = g * s // LANES
    return dict(wbr=block_diag_in(bb_re), wbi=block_diag_in(bb_im),
                wcr=block_diag_out(c_re), wci=block_diag_out(c_im),
                lam_re=lam_re.reshape(nst, LANES), lam_im=lam_im.reshape(nst, LANES))


def kernel(x_prompt, x_sample, state_ssd, state_ssd_conv, state_s5_re, state_s5_im, cache_mla_latent, cache_mla_krope, page_table, state_ffn_conv, norm_mix_pre, norm_mix_post, norm_ffn_pre, norm_ffn_post, ssd_w_in, ssd_conv_w, ssd_conv_b, ssd_dt_bias, ssd_a_log, ssd_d, ssd_norm, ssd_w_out, s5_a_re, s5_a_im, s5_log_step, s5_b_re, s5_b_im, s5_c_re, s5_c_im, s5_d, s5_w_val, s5_w_gate, mla_w_dq, mla_q_norm, mla_w_uq, mla_w_dkv, mla_kv_norm, mla_w_uk, mla_w_uv, mla_w_o, ffn_w_gate, ffn_w_up, ffn_conv_w, ffn_conv_b, ffn_w_down):
    b_p, l_p, d = x_prompt.shape
    b_s, l_s, _ = x_sample.shape
    depth = norm_mix_pre.shape[0]
    past_len = page_table.shape[1] * PAGE_SIZE
    groups_of = {"p": (b_p, l_p), "s": (b_s, l_s)}
    hid = {"p": x_prompt.reshape(b_p * l_p, d), "s": x_sample.reshape(b_s * l_s, d)}
    outs = {k: {n: [] for n in ("ssd_c", "s5r", "s5i", "lat", "kr", "ffc")} for k in ("p", "s")}
    ssd_h = {"p": None, "s": None}

    _, _, nh, hd, ds = state_ssd.shape
    di = nh * hd
    cd = state_ssd_conv.shape[-1]
    ssd_groups = (cd - di) // (2 * ds)
    s5_g, s5_s = s5_a_re.shape[1:]
    nst = s5_g * s5_s // LANES
    lora, heads, nope = mla_w_uk.shape[1:]
    rope = cache_mla_krope.shape[-1]
    mla_scale = 1.0 / math.sqrt(nope + rope)
    row1 = lambda a: a.reshape(1, -1)
    ffn_wg, ffn_wu, ffn_wd = ffn_w_gate.astype(BF16), ffn_w_up.astype(BF16), ffn_w_down.astype(BF16)

    for i in range(depth):
        kind, j = i % N_MIXERS, i // N_MIXERS
        if kind == 0:
            wz, wx = ssd_w_in[j, :, :di].astype(BF16), ssd_w_in[j, :, di:di + cd].astype(BF16)
            wdt = _pad_lanes(ssd_w_in[j, :, di + cd:]).astype(BF16)
            dtb, alog = (_pad_lanes(row1(a[j])) for a in (ssd_dt_bias, ssd_a_log))
            dsk = jnp.repeat(ssd_d[j], hd).reshape(1, di)
            wo = ssd_w_out[j].astype(BF16)
            for k, (nb, sl) in groups_of.items():
                if k == "p":
                    prev8 = jnp.zeros((nb * SUBLANES, cd), F32)
                    h0, h0_layer = jnp.zeros((1, nb, nh, hd, ds), F32), 0
                else:
                    prev8 = _hist8(state_ssd_conv[j]).reshape(nb * SUBLANES, cd)
                    h0, h0_layer = state_ssd, j
                z, xact, dt, cnew = _ssd_in(hid[k], prev8, row1(norm_mix_pre[i]), wz, wx, wdt, ssd_conv_w[j],
                                            row1(ssd_conv_b[j]), dtb, nb=nb, seqlen=sl)
                y, ssd_h[k] = _ssd_scan(xact, dt, h0, h0_layer, alog, dsk, ssd_h[k], nb=nb, seqlen=sl,
                                        groups=ssd_groups)
                hid[k] = _ssd_out(y, z, hid[k], row1(ssd_norm[j]), wo, row1(norm_mix_post[i]), groups=ssd_groups)
                outs[k]["ssd_c"].append(cnew.reshape(nb, SUBLANES, cd)[:, SUBLANES - (SSD_CONV - 1):, :])
        elif kind == 1:
            sp = _s5_params(s5_a_re[j], s5_a_im[j], s5_log_step[j], s5_b_re[j], s5_b_im[j], s5_c_re[j], s5_c_im[j], d)
            wv, wg = s5_w_val[j].astype(BF16), s5_w_gate[j].astype(BF16)
            for k, (nb, sl) in groups_of.items():
                if k == "p":
                    h0r = h0i = jnp.zeros((nb, nst, LANES), F32)
                    lamr, lami = sp["lam_re"], sp["lam_im"]
                else:
                    h0r, h0i = state_s5_re[j].reshape(nb, nst * LANES), state_s5_im[j].reshape(nb, nst * LANES)
                    lamr, lami = sp["lam_re"].reshape(1, -1), sp["lam_im"].reshape(1, -1)
                hid[k], hr, hi = _s5(hid[k], h0r, h0i, row1(norm_mix_pre[i]), sp["wbr"], sp["wbi"], lamr, lami,
                                     sp["wcr"], sp["wci"], row1(s5_d[j]), wv, wg, row1(norm_mix_post[i]),
                                     nb=nb, seqlen=sl)
                outs[k]["s5r"].append(hr.reshape(nb, s5_g, s5_s))
                outs[k]["s5i"].append(hi.reshape(nb, s5_g, s5_s))
        else:
            w_uq = mla_w_uq[j].reshape(-1, heads, nope + rope)
            wqn = w_uq[:, :, :nope].reshape(-1, heads * nope).astype(BF16)
            wqr = w_uq[:, :, nope:].reshape(-1, heads * rope).astype(BF16)
            wqrs = _swap_halves(wqr, rope)
            wkv = mla_w_dkv[j][:, :lora].astype(BF16)
            wkr = mla_w_dkv[j][:, lora:].astype(BF16)
            wkrs = _swap_halves(wkr, rope)
            wukt = jnp.transpose(mla_w_uk[j], (1, 2, 0)).astype(BF16)
            wuv = jnp.transpose(mla_w_uv[j], (1, 0, 2)).astype(BF16)
            wdq, wo = mla_w_dq[j].astype(BF16), mla_w_o[j].astype(BF16)
            for k, (nb, sl) in groups_of.items():
                t = nb * sl
                if k == "p":
                    tm = _row_tile(sl, 256)
                    tabs = _rope_tables(jnp.arange(sl), rope, heads)
                    pos_tiles = sl // tm
                else:
                    tm = _row_tile(t, 256)
                    tabs = _rope_tables(past_len + jnp.arange(tm) % sl, rope, heads)
                    pos_tiles = 1
                proj = _mla_proj(
                    hid[k], tabs, row1(norm_mix_pre[i]), wdq, row1(mla_q_norm[j]), wqn, wqr, wqrs, wukt, wkv, wkr, wkrs,
                    row1(mla_kv_norm[j]), pos_tiles=pos_tiles, tm=tm, for_prompt=k == "p",
                    qscale=mla_scale * LOG2E if k == "p" else 1.0)
                qlat, qrope, ckv, kr = proj[:4]
                if k == "p":
                    o = _mla_prompt_attn(qlat, qrope, *proj[4:], nb=nb, seqlen=sl)
                else:
                    o = _mla_sample_attn(qlat, qrope, ckv, kr, cache_mla_latent, jnp.swapaxes(cache_mla_krope, 2, 3),
                                         j, page_table, seqlen=sl, scale=mla_scale)
                hid[k] = _mla_out(o, hid[k], wuv, wo, row1(norm_mix_post[i]))
                outs[k]["lat"].append(ckv.reshape(nb, sl, lora))
                outs[k]["kr"].append(kr.reshape(nb, sl, rope))
        f = ffn_wg.shape[2]
        for k, (nb, sl) in groups_of.items():
            if k == "p":
                prev8 = jnp.zeros((nb * SUBLANES, f), F32)
            else:
                prev8 = _hist8(state_ffn_conv[i]).reshape(nb * SUBLANES, f)
            hid[k], cnew = _ffn(hid[k], prev8, row1(norm_ffn_pre[i]), ffn_wg, ffn_wu, ffn_conv_w[i],
                                row1(ffn_conv_b[i]), ffn_wd, row1(norm_ffn_post[i]), i, nb=nb, seqlen=sl)
            outs[k]["ffc"].append(cnew.reshape(nb, SUBLANES, f)[:, SUBLANES - (FFN_CONV - 1):, :])

    res = [hid["p"].reshape(b_p, l_p, d), hid["s"].reshape(b_s, l_s, d)]
    for k in ("p", "s"):
        res += [ssd_h[k]] + [jnp.stack(outs[k][n]) for n in ("ssd_c", "s5r", "s5i", "lat", "kr", "ffc")]
    return tuple(res)
```

```python
import functools
import math

import jax
import jax.numpy as jnp
from jax import lax
from jax.experimental import pallas as pl
from jax.experimental.pallas import tpu as pltpu

F32 = jnp.float32
BF16 = jnp.bfloat16
RMS_EPS = 1e-6
ROPE_THETA = 10000.0
LOG2E = math.log2(math.e)
PAGE_SIZE = 128
N_MIXERS = 3

SUBLANES = 8
LANES = 128
VMEM_LIMIT_BYTES = 56 * 1024 * 1024

SSD_CONV = 4
FFN_CONV = 3
CONV_CHUNK = 1024
ATTN_QUERY_TILE = 256
ATTN_KEY_BLOCK = 512
SAMPLE_KEY_PAGES = 16
SHORT_SEQ_CHUNK = 16


def _cparams(n_axes=1):
    return pltpu.CompilerParams(dimension_semantics=("arbitrary",) * n_axes,
                                vmem_limit_bytes=VMEM_LIMIT_BYTES)


def _full(shape):
    nd = len(shape)
    return pl.BlockSpec(shape, lambda i: (0,) * nd)


def _layer_of(stacked, layer):
    nd = stacked.ndim - 1
    return pl.BlockSpec((None,) + stacked.shape[1:], lambda i: (layer,) + (0,) * nd)


def _row_tile(n, pref):
    t = min(n, pref)
    while n % t or t % SUBLANES:
        t -= SUBLANES
    assert t > 0
    return t


def _stream_dtype(seqlen):
    return F32 if seqlen == SUBLANES else BF16


def _rms(x, w):
    return x * lax.rsqrt(jnp.mean(x * x, axis=-1, keepdims=True) + RMS_EPS) * w


def _sigmoid(x):
    return 0.5 * jnp.tanh(0.5 * x) + 0.5


def _silu(x):
    h = 0.5 * x
    return h * jnp.tanh(h) + h


def _softplus(x):
    return jnp.maximum(x, 0.0) + jnp.log1p(jnp.exp(-jnp.abs(x)))


def _dot(a, b):
    return jnp.dot(a, b, preferred_element_type=F32)


def _dot_nt(a, b):
    return lax.dot_general(a, b, (((1,), (1,)), ((), ())), preferred_element_type=F32)


def _dot_tn(a, b):
    return lax.dot_general(a, b, (((0,), (0,)), ((), ())), preferred_element_type=F32)


def _shifted_rows(x, prev8, s, seg):
    if s == 0:
        return x
    rows = x.shape[0]
    rolled = pltpu.roll(x, s, axis=0)
    if seg == rows:
        head = jnp.where(lax.broadcasted_iota(jnp.int32, (SUBLANES, x.shape[1]), 0) < s,
                         pltpu.roll(prev8, s, axis=0), rolled[:SUBLANES])
        if rows == SUBLANES:
            return head
        return jnp.concatenate([head, rolled[SUBLANES:]], axis=0)
    assert seg == SUBLANES
    hist = pltpu.roll(prev8, (rows + s - SUBLANES) % rows, axis=0) if rows > SUBLANES else pltpu.roll(prev8, s, axis=0)
    t = lax.broadcasted_iota(jnp.int32, x.shape, 0) % SUBLANES
    return jnp.where(t < s, hist, rolled)


def _causal_conv(x, prev8, w, b, seg):
    k = w.shape[0]
    acc = b
    for j in range(k):
        acc = acc + _shifted_rows(x, prev8, k - 1 - j, seg) * w[j:j + 1, :]
    return acc


def _carried_conv(hist_ref, cols, x, w, b):
    acc = _causal_conv(x, hist_ref[:, cols], w, b, x.shape[0])
    hist_ref[:, cols] = x[x.shape[0] - SUBLANES:, :]
    return acc


def _col_chunks(n, width):
    return [slice(lo, min(lo + width, n)) for lo in range(0, n, width)]


def _ssd_in_kernel(x_ref, prev_ref, nw_ref, wz_ref, wx_ref, wdt_ref, cw_ref, cb_ref, dtb_ref,
                   z_ref, xact_ref, dt_ref, convnew_ref, hist_ref, *, seg, tiles_per_seq):
    tm = x_ref.shape[0]
    u = _rms(x_ref[...], nw_ref[...]).astype(BF16)
    if seg == tm:
        @pl.when(pl.program_id(0) % tiles_per_seq == 0)
        def _():
            hist_ref[...] = prev_ref[...]

    for cols in _col_chunks(wx_ref.shape[1], CONV_CHUNK):
        xbc = _dot(u, wx_ref[:, cols])
        if seg == tm:
            conv = _carried_conv(hist_ref, cols, xbc, cw_ref[:, cols], cb_ref[:, cols])
            convnew_ref[:, cols] = xbc[tm - SUBLANES:, :]
        else:
            conv = _causal_conv(xbc, prev_ref[:, cols], cw_ref[:, cols], cb_ref[:, cols], seg)
            convnew_ref[:, cols] = xbc
        xact_ref[:, cols] = _silu(conv)
    z_ref[...] = _dot(u, wz_ref[...]).astype(z_ref.dtype)
    dt_ref[...] = _softplus(_dot(u, wdt_ref[...]) + dtb_ref[...])


def _ssd_in(x, prev8, nw, wz, wx, wdt, cw, cb, dtb, *, nb, seqlen):
    t, d = x.shape
    di, cd, dp = wz.shape[1], wx.shape[1], wdt.shape[1]
    if seqlen == SUBLANES:
        tm = _row_tile(t, 256)
        seg, tps = SUBLANES, 1
        hist_spec = pl.BlockSpec((tm, cd), lambda i: (i, 0))
    else:
        tm = _row_tile(seqlen, 256)
        seg, tps = tm, seqlen // tm
        hist_spec = pl.BlockSpec((SUBLANES, cd), lambda i: (i // tps, 0))
    row = lambda n: pl.BlockSpec((tm, n), lambda i: (i, 0))
    consts = [nw, wz, wx, wdt, cw, cb, dtb]
    return pl.pallas_call(
        functools.partial(_ssd_in_kernel, seg=seg, tiles_per_seq=tps),
        grid=(t // tm,),
        in_specs=[row(d), hist_spec] + [_full(a.shape) for a in consts],
        out_specs=[row(di), row(cd), row(dp), hist_spec],
        out_shape=[jax.ShapeDtypeStruct((t, di), _stream_dtype(seqlen)), jax.ShapeDtypeStruct((t, cd), F32),
                   jax.ShapeDtypeStruct((t, dp), F32), jax.ShapeDtypeStruct((nb * SUBLANES, cd), F32)],
        scratch_shapes=[pltpu.VMEM((SUBLANES, cd), F32)],
        compiler_params=_cparams(),
        name="ssd_in",
    )(x, prev8, *consts)


def _cumsum_rows(x):
    rows = x.shape[0]
    row = lax.broadcasted_iota(jnp.int32, x.shape, 0)
    k = 1
    while k < rows:
        x = x + jnp.where(row >= k, pltpu.roll(x, k, axis=0), 0.0)
        k *= 2
    return x


def _ssd_scan_kernel(xact_ref, dt_ref, h0_ref, alog_ref, dsk_ref, expand_ref, *rest,
                     lt, q, groups, hpg, hd, ds, nc, n_prev):
    hprev_ref = rest[0] if n_prev else None
    y_ref, hfin_ref, xpad_ref, dtpad_ref, h_ref = rest[1 if n_prev else 0:]
    c = pl.program_id(1)
    di = groups * hpg * hd
    gn = groups * ds

    @pl.when(c == 0)
    def _():
        h_ref[...] = h0_ref[0, 0]

    if lt < q:
        xpad_ref[...] = jnp.zeros_like(xpad_ref)
        xpad_ref[0:lt, :] = xact_ref[...]
        dtpad_ref[...] = jnp.zeros_like(dtpad_ref)
        dtpad_ref[0:lt, :] = dt_ref[...]
        xact = xpad_ref[...]
        dt = dtpad_ref[...]
    else:
        xact = xact_ref[...]
        dt = dt_ref[...]
    xs = xact[:, :di]
    bm = xact[:, di:di + gn]
    cm = xact[:, di + gn:]
    a = -jnp.exp(alog_ref[...])
    cs = _cumsum_rows(dt * a) * LOG2E
    cs_last = cs[q - 1:q, :]
    nh = groups * hpg
    assert 3 * nh <= LANES
    ecs = jnp.exp2(cs)
    hi = ecs.astype(BF16).astype(F32)
    mid = (ecs - hi).astype(BF16).astype(F32)
    lo = ecs - hi - mid
    lane = lax.broadcasted_iota(jnp.int32, ecs.shape, 1)
    packed = jnp.where(lane < nh, hi, jnp.where(lane < 2 * nh, pltpu.roll(mid, nh, axis=1), pltpu.roll(lo, 2 * nh, axis=1)))
    ecs_x = _dot(packed.astype(BF16), expand_ref[...])
    e_last = jnp.exp2(cs_last)
    cs_t = cs.T
    dt_t = dt.T
    w_t = (jnp.exp2(cs_last - cs) * dt).T
    causal = (lax.broadcasted_iota(jnp.int32, (q, q), 0) >= lax.broadcasted_iota(jnp.int32, (q, q), 1))
    even = lax.broadcasted_iota(jnp.int32, (q, 2 * hd), 1) < hd
    assert 2 * hd == LANES and hpg % 2 == 0
    y_pairs = []
    for g in range(groups):
        bg = bm[:, g * ds:(g + 1) * ds].astype(BF16)
        cg = cm[:, g * ds:(g + 1) * ds].astype(BF16)
        cb = _dot_nt(cg, bg)
        hg = h_ref[g * hpg:(g + 1) * hpg].reshape(hpg * hd, ds)
        yoff = _dot_nt(cg, hg.astype(BF16))
        xs_g = xs[:, g * hpg * hd:(g + 1) * hpg * hd]
        for k in range(hpg // 2):
            pair = xs_g[:, k * LANES:(k + 1) * LANES].astype(BF16)
            yd = []
            for h in (g * hpg + 2 * k, g * hpg + 2 * k + 1):
                seg = cs[:, h:h + 1] - cs_t[h:h + 1, :]
                m = cb * jnp.exp2(jnp.where(causal, seg, -jnp.inf)) * dt_t[h:h + 1, :]
                yd.append(_dot(m.astype(BF16), pair))
            lo = g * hpg * hd + k * LANES
            y_pairs.append(jnp.where(even, yd[0], yd[1]) + yoff[:, k * LANES:(k + 1) * LANES] * ecs_x[:, lo:lo + LANES])
        xs_gt = xs_g.T
        xw = [xs_gt[r * hd:(r + 1) * hd, :] * w_t[g * hpg + r:g * hpg + r + 1, :] for r in range(hpg)]
        s_g = _dot(jnp.concatenate(xw, axis=0).astype(BF16), bg)
        for r in range(hpg):
            h = g * hpg + r
            h_ref[h] = h_ref[h] * e_last[:, h:h + 1] + s_g[r * hd:(r + 1) * hd, :]
    y = jnp.concatenate(y_pairs, axis=1) + dsk_ref[...] * xs
    y_ref[...] = y[0:lt, :].astype(y_ref.dtype)

    @pl.when(c == nc - 1)
    def _():
        for k in range(n_prev):
            hfin_ref[k, 0] = hprev_ref[k, 0]
        hfin_ref[n_prev, 0] = h_ref[...]


def _ssd_scan(xact, dt, h0, layer, alog, dsk, h_prev, *, nb, seqlen, groups):
    _, _, nh, hd, ds = h0.shape
    n_prev = 0 if h_prev is None else h_prev.shape[0]
    state_blk = lambda n: pl.BlockSpec((n, 1, nh, hd, ds), lambda b, c: (0, b, 0, 0, 0))
    piece_row = jnp.arange(LANES)[:, None]
    expand = ((jnp.arange(nh * hd)[None, :] // hd == piece_row % nh) & (piece_row < 3 * nh)).astype(BF16)
    cd = xact.shape[1]
    hpg = nh // groups
    di = nh * hd
    if seqlen >= 256:
        lt = q = 256
    else:
        lt, q = seqlen, SHORT_SEQ_CHUNK
    assert seqlen % lt == 0 and lt % SUBLANES == 0
    nc = seqlen // lt
    pad_rows = q if lt < q else SUBLANES
    kern = functools.partial(_ssd_scan_kernel, lt=lt, q=q, groups=groups, hpg=hpg, hd=hd, ds=ds, nc=nc,
                             n_prev=n_prev)
    row = lambda n: pl.BlockSpec((lt, n), lambda b, c: (b * nc + c, 0))
    par = lambda a: pl.BlockSpec(a.shape, lambda b, c: (0,) * a.ndim)
    return pl.pallas_call(
        kern,
        grid=(nb, nc),
        in_specs=[row(cd), row(dt.shape[1]),
                  pl.BlockSpec((1, 1, nh, hd, ds), lambda b, c: (layer, b, 0, 0, 0)),
                  par(alog), par(dsk), par(expand)] + ([state_blk(n_prev)] if n_prev else []),
        out_specs=[row(di), state_blk(n_prev + 1)],
        out_shape=[jax.ShapeDtypeStruct((nb * seqlen, di), _stream_dtype(seqlen)),
                   jax.ShapeDtypeStruct((n_prev + 1, nb, nh, hd, ds), F32)],
        scratch_shapes=[pltpu.VMEM((pad_rows, cd), F32), pltpu.VMEM((pad_rows, dt.shape[1]), F32),
                        pltpu.VMEM((nh, hd, ds), F32)],
        compiler_params=_cparams(2),
        name="ssd_scan",
    )(xact, dt, h0, alog, dsk, expand, *([h_prev] if n_prev else []))


def _ssd_out_kernel(y_ref, z_ref, x_ref, gw_ref, wo_ref, pw_ref, o_ref, *, groups):
    yg = y_ref[...].astype(F32) * _silu(z_ref[...].astype(F32))
    gs = yg.shape[1] // groups
    parts = []
    for g in range(groups):
        blk = yg[:, g * gs:(g + 1) * gs]
        parts.append(blk * lax.rsqrt(jnp.mean(blk * blk, axis=-1, keepdims=True) + RMS_EPS))
    yn = (jnp.concatenate(parts, axis=1) * gw_ref[...]).astype(BF16)
    o_ref[...] = x_ref[...] + _rms(_dot(yn, wo_ref[...]), pw_ref[...])


def _ssd_out(y, z, x, gw, wo, pw, *, groups):
    t, d = x.shape
    tm = _row_tile(t, 512)
    row = lambda n: pl.BlockSpec((tm, n), lambda i: (i, 0))
    return pl.pallas_call(
        functools.partial(_ssd_out_kernel, groups=groups),
        grid=(t // tm,),
        in_specs=[row(y.shape[1]), row(z.shape[1]), row(d), _full(gw.shape), _full(wo.shape), _full(pw.shape)],
        out_specs=row(d),
        out_shape=jax.ShapeDtypeStruct((t, d), F32),
        compiler_params=_cparams(),
        name="ssd_out",
    )(y, z, x, gw, wo, pw)


def _ffn_kernel(x_ref, prev_ref, nw_ref, wg_ref, wu_ref, cw_ref, cb_ref, wd_ref, pw_ref,
                o_ref, convnew_ref, hist_ref, *, seg, tiles_per_seq):
    x = x_ref[...]
    tm = x.shape[0]
    u = _rms(x, nw_ref[...]).astype(BF16)
    if seg == tm:
        @pl.when(pl.program_id(0) % tiles_per_seq == 0)
        def _():
            hist_ref[...] = prev_ref[...]

    out = None
    for cols in _col_chunks(wg_ref.shape[1], CONV_CHUNK):
        g = _dot(u, wg_ref[:, cols])
        up = _dot(u, wu_ref[:, cols])
        if seg == tm:
            gc = _carried_conv(hist_ref, cols, g, cw_ref[:, cols], cb_ref[:, cols])
            convnew_ref[:, cols] = g[tm - SUBLANES:, :]
        else:
            gc = _causal_conv(g, prev_ref[:, cols], cw_ref[:, cols], cb_ref[:, cols], seg)
            convnew_ref[:, cols] = g
        part = _dot((_silu(gc) * up).astype(BF16), wd_ref[cols, :])
        out = part if out is None else out + part
    o_ref[...] = x + _rms(out, pw_ref[...])


def _ffn(x, prev8, nw, wg, wu, cw, cb, wd, pw, layer, *, nb, seqlen):
    t, d = x.shape
    f = wg.shape[2]
    if seqlen == SUBLANES:
        tm = _row_tile(t, 256)
        seg, tps = SUBLANES, 1
        prev_spec = pl.BlockSpec((tm, f), lambda i: (i, 0))
        new_spec = pl.BlockSpec((tm, f), lambda i: (i, 0))
    else:
        tm = _row_tile(seqlen, 512)
        seg, tps = tm, seqlen // tm
        prev_spec = pl.BlockSpec((SUBLANES, f), lambda i: (i // tps, 0))
        new_spec = pl.BlockSpec((SUBLANES, f), lambda i: (i // tps, 0))
    row = pl.BlockSpec((tm, d), lambda i: (i, 0))
    return pl.pallas_call(
        functools.partial(_ffn_kernel, seg=seg, tiles_per_seq=tps),
        grid=(t // tm,),
        in_specs=[row, prev_spec, _full(nw.shape), _layer_of(wg, layer), _layer_of(wu, layer), _full(cw.shape),
                  _full(cb.shape), _layer_of(wd, layer), _full(pw.shape)],
        out_specs=[row, new_spec],
        out_shape=[jax.ShapeDtypeStruct((t, d), F32), jax.ShapeDtypeStruct((nb * SUBLANES, f), F32)],
        scratch_shapes=[pltpu.VMEM((SUBLANES, f), F32)],
        compiler_params=_cparams(),
        name="conv_ffn",
    )(x, prev8, nw, wg, wu, cw, cb, wd, pw)


def _gelu(x):
    return 0.5 * x * (1.0 + lax.erf(x * (1.0 / math.sqrt(2.0))))


def _s5_token_pitch(nst):
    groups = -(-nst // SUBLANES)
    return SUBLANES * (groups + 1 - groups % 2)


def _s5_kernel(x_ref, h0r_ref, h0i_ref, nw_ref, wbr_ref, wbi_ref, lamr_ref, lami_ref, wcr_ref, wci_ref,
               dsk_ref, wv_ref, wg_ref, pw_ref, o_ref, hr_out_ref, hi_out_ref, sr_ref, si_ref, cr_ref, ci_ref,
               *, long_seq, tiles_per_seq, kb, nst):
    x = x_ref[...]
    tm, d = x.shape
    u = _rms(x, nw_ref[...])
    ub = u.astype(BF16)
    kw = d // kb
    st_per_kb = nst // kb
    if long_seq:
        i = pl.program_id(0)

        @pl.when(i % tiles_per_seq == 0)
        def _():
            cr_ref[...] = h0r_ref[0]
            ci_ref[...] = h0i_ref[0]

        pitch = _s5_token_pitch(nst)
        for b in range(kb):
            blk = ub[:, b * kw:(b + 1) * kw]
            pr = _dot(blk, wbr_ref[b])
            pi = _dot(blk, wbi_ref[b])
            for j in range(st_per_kb):
                st = b * st_per_kb + j
                sr_ref[pl.ds(st, tm, stride=pitch), :] = pr[:, j * LANES:(j + 1) * LANES]
                si_ref[pl.ds(st, tm, stride=pitch), :] = pi[:, j * LANES:(j + 1) * LANES]
        lr = lamr_ref[...]
        li = lami_ref[...]

        def step(t, carry):
            hr, hi = carry
            off = pl.multiple_of(t * pitch, SUBLANES)
            br = sr_ref[pl.ds(off, nst), :]
            bi = si_ref[pl.ds(off, nst), :]
            nr = lr * hr - li * hi + br
            ni = lr * hi + li * hr + bi
            sr_ref[pl.ds(off, nst), :] = nr
            si_ref[pl.ds(off, nst), :] = ni
            return nr, ni

        hr, hi = lax.fori_loop(0, tm, step, (cr_ref[...], ci_ref[...]), unroll=8)
        cr_ref[...] = hr
        ci_ref[...] = hi
        hr_out_ref[0] = hr
        hi_out_ref[0] = hi
        load_r = lambda st: sr_ref[pl.ds(st, tm, stride=pitch), :]
        load_i = lambda st: si_ref[pl.ds(st, tm, stride=pitch), :]
    else:
        nseq = tm // SUBLANES
        for b in range(kb):
            blk = ub[:, b * kw:(b + 1) * kw]
            pr = _dot(blk, wbr_ref[b])
            pi = _dot(blk, wbi_ref[b])
            for j in range(st_per_kb):
                sr_ref[b * st_per_kb + j] = pr[:, j * LANES:(j + 1) * LANES]
                si_ref[b * st_per_kb + j] = pi[:, j * LANES:(j + 1) * LANES]
        for st in range(nst):
            lanes = slice(st * LANES, (st + 1) * LANES)
            lr = lamr_ref[:, lanes]
            li = lami_ref[:, lanes]
            hr = h0r_ref[:, lanes]
            hi = h0i_ref[:, lanes]
            for t in range(SUBLANES):
                rows_t = pl.ds(t, nseq, stride=SUBLANES)
                br = sr_ref[st, rows_t, :]
                bi = si_ref[st, rows_t, :]
                hr, hi = lr * hr - li * hi + br, lr * hi + li * hr + bi
                sr_ref[st, rows_t, :] = hr
                si_ref[st, rows_t, :] = hi
            hr_out_ref[:, lanes] = hr
            hi_out_ref[:, lanes] = hi
        load_r = lambda st: sr_ref[st]
        load_i = lambda st: si_ref[st]

    ys = []
    for b in range(kb):
        hr_b = jnp.concatenate([load_r(b * st_per_kb + j) for j in range(st_per_kb)], axis=1).astype(BF16)
        hi_b = jnp.concatenate([load_i(b * st_per_kb + j) for j in range(st_per_kb)], axis=1).astype(BF16)
        ys.append(_dot(hr_b, wcr_ref[b]) - _dot(hi_b, wci_ref[b]))
    y = jnp.concatenate(ys, axis=1) + dsk_ref[...] * u
    gl = _gelu(y).astype(BF16)
    val = _dot(gl, wv_ref[...])
    gate = _dot(gl, wg_ref[...])
    out = val * _sigmoid(gate)
    o_ref[...] = x + _rms(out, pw_ref[...])


def _s5(x, h0r, h0i, nw, wbr, wbi, lamr, lami, wcr, wci, dsk, wv, wg, pw, *, nb, seqlen):
    t, d = x.shape
    kb = wbr.shape[0]
    nst = kb * wbr.shape[2] // LANES
    long_seq = seqlen != SUBLANES
    if long_seq:
        tm = _row_tile(seqlen, 256)
        tps = seqlen // tm
        st_spec = pl.BlockSpec((1, nst, LANES), lambda i: (i // tps, 0, 0))
        st_shape = jax.ShapeDtypeStruct((nb, nst, LANES), F32)
        rows = tm * _s5_token_pitch(nst)
        scratch = [pltpu.VMEM((rows, LANES), F32), pltpu.VMEM((rows, LANES), F32),
                   pltpu.VMEM((nst, LANES), F32), pltpu.VMEM((nst, LANES), F32)]
    else:
        tm = _row_tile(t, 256)
        tps = 1
        ns = tm // SUBLANES
        st_spec = pl.BlockSpec((ns, nst * LANES), lambda i: (i, 0))
        st_shape = jax.ShapeDtypeStruct((nb, nst * LANES), F32)
        scratch = [pltpu.VMEM((nst, tm, LANES), F32), pltpu.VMEM((nst, tm, LANES), F32),
                   pltpu.VMEM((SUBLANES, LANES), F32), pltpu.VMEM((SUBLANES, LANES), F32)]
    row = pl.BlockSpec((tm, d), lambda i: (i, 0))
    consts = [nw, wbr, wbi, lamr, lami, wcr, wci, dsk, wv, wg, pw]
    return pl.pallas_call(
        functools.partial(_s5_kernel, long_seq=long_seq, tiles_per_seq=tps, kb=kb, nst=nst),
        grid=(t // tm,),
        in_specs=[row, st_spec, st_spec] + [_full(a.shape) for a in consts],
        out_specs=[row, st_spec, st_spec],
        out_shape=[jax.ShapeDtypeStruct((t, d), F32), st_shape, st_shape],
        scratch_shapes=scratch,
        compiler_params=_cparams(),
        name="s5",
    )(x, h0r, h0i, *consts)


def _mla_proj_kernel(x_ref, cos_ref, sin_ref, nw_ref, wdq_ref, qn_ref, wqn_ref,
                     wqr_ref, wqrs_ref, wukt_ref, wkv_ref, wkr_ref, wkrs_ref, kvn_ref,
                     qlat_ref, qrope_ref, ckv_ref, krope_ref, *key_refs, heads, nope, rope, qscale):
    u = _rms(x_ref[...], nw_ref[...]).astype(BF16)
    cq = _rms(_dot(u, wdq_ref[...]), qn_ref[...]).astype(BF16)
    qn = _dot(cq, wqn_ref[...])
    reps = heads * rope // LANES
    qr = (_dot(cq, wqr_ref[...]) * jnp.tile(cos_ref[...], (1, reps))
          + _dot(cq, wqrs_ref[...]) * jnp.tile(sin_ref[...], (1, reps)))
    ckv = _rms(_dot(u, wkv_ref[...]), kvn_ref[...])
    kr = _dot(u, wkr_ref[...]) * cos_ref[:, :rope] + _dot(u, wkrs_ref[...]) * sin_ref[:, :rope]
    ckv_ref[...] = ckv
    krope_ref[...] = kr
    for h in range(heads):
        ql = _dot(qn[:, h * nope:(h + 1) * nope].astype(BF16), wukt_ref[h])
        qlat_ref[h] = (ql * qscale).astype(qlat_ref.dtype)
        qrope_ref[h] = (qr[:, h * rope:(h + 1) * rope] * qscale).astype(qrope_ref.dtype)
    if key_refs:
        ckvb_ref, kropeb_ref, ckvt_ref = key_refs
        ckvb_ref[...] = ckv.astype(BF16)
        kropeb_ref[...] = kr.astype(BF16)
        ckvt_ref[0] = ckv.T.astype(BF16)


def _mla_proj(x, tabs, nw, wdq, qn, wqn, wqr, wqrs, wukt, wkv, wkr, wkrs, kvn, *, pos_tiles, tm, for_prompt,
              qscale):
    t, d = x.shape
    heads, nope, lora = wukt.shape
    rope = wkr.shape[1]
    cos_tab, sin_tab = tabs
    assert (heads * rope) % LANES == 0 and LANES % rope == 0
    row = lambda n: pl.BlockSpec((tm, n), lambda i: (i, 0))
    tab = lambda n: pl.BlockSpec((tm, n), lambda i: (i % pos_tiles, 0))
    consts = [nw, wdq, qn, wqn, wqr, wqrs, wukt, wkv, wkr, wkrs, kvn]
    qdtype = BF16 if for_prompt else F32
    out_specs = [pl.BlockSpec((heads, tm, lora), lambda i: (0, i, 0)),
                 pl.BlockSpec((heads, tm, rope), lambda i: (0, i, 0)), row(lora), row(rope)]
    out_shape = [jax.ShapeDtypeStruct((heads, t, lora), qdtype), jax.ShapeDtypeStruct((heads, t, rope), qdtype),
                 jax.ShapeDtypeStruct((t, lora), F32), jax.ShapeDtypeStruct((t, rope), F32)]
    if for_prompt:
        out_specs += [row(lora), row(rope), pl.BlockSpec((1, lora, tm), lambda i: (i, 0, 0))]
        out_shape += [jax.ShapeDtypeStruct((t, lora), BF16), jax.ShapeDtypeStruct((t, rope), BF16),
                      jax.ShapeDtypeStruct((t // tm, lora, tm), BF16)]
    return pl.pallas_call(
        functools.partial(_mla_proj_kernel, heads=heads, nope=nope, rope=rope, qscale=qscale),
        grid=(t // tm,),
        in_specs=[row(d), tab(LANES), tab(LANES)]
        + [_full(a.shape) for a in consts],
        out_specs=out_specs,
        out_shape=out_shape,
        compiler_params=_cparams(),
        name="mla_proj",
    )(x, cos_tab, sin_tab, *consts)


def _mla_prompt_kernel(ql_ref, qr_ref, kc_ref, kr_ref, vt_ref, o_ref, m_ref, l_ref, acc_ref,
                       *, heads, tq, tk):
    qi = pl.program_id(1)
    tile = vt_ref.shape[2]
    per_block = tk // tile
    first_own = qi * (tq // tile)
    n_full = first_own // per_block
    m_ref[...] = jnp.full_like(m_ref, -jnp.inf)
    l_ref[...] = jnp.zeros_like(l_ref)
    acc_ref[...] = jnp.zeros_like(acc_ref)

    nqh = heads * tq
    ql = ql_ref[...].reshape(nqh, ql_ref.shape[2])
    qr = qr_ref[...].reshape(nqh, qr_ref.shape[2])

    def block(first_tile, ntiles, masked):
        nk = ntiles * tile
        off = pl.multiple_of(first_tile * tile, tile)
        kc = kc_ref[pl.ds(off, nk), :]
        kr = kr_ref[pl.ds(off, nk), :]
        s = _dot_nt(kc, ql) + _dot_nt(kr, qr)
        if masked:
            key = off + lax.broadcasted_iota(jnp.int32, (nk, nqh), 0)
            qry = qi * tq + lax.broadcasted_iota(jnp.int32, (nk, nqh), 1) % tq
            s = jnp.where(key <= qry, s, -jnp.inf)
        m_prev = m_ref[...]
        m_new = jnp.maximum(m_prev, jnp.max(s, axis=0, keepdims=True))
        alpha = jnp.exp2(m_prev - m_new)
        p = jnp.exp2(s - m_new)
        l_ref[...] = alpha * l_ref[...] + jnp.sum(p, axis=0, keepdims=True)
        vt = jnp.concatenate([vt_ref[first_tile + j] for j in range(ntiles)], axis=1)
        acc_ref[...] = alpha * acc_ref[...] + _dot(vt, p.astype(BF16))
        m_ref[...] = m_new

    def body(ki, carry):
        block(ki * per_block, per_block, False)
        return carry

    lax.fori_loop(0, n_full, body, 0)
    for rem in range(1, per_block):
        @pl.when(first_own - n_full * per_block == rem)
        def _():
            block(n_full * per_block, rem, False)
    block(first_own, tq // tile, True)
    o_t = acc_ref[...] / l_ref[...]
    for h in range(heads):
        o_ref[h] = o_t[:, h * tq:(h + 1) * tq].T.astype(BF16)


def _mla_prompt_attn(qlat, qrope, ckvb, kropeb, ckv_t, *, nb, seqlen):
    heads, t, lora = qlat.shape
    rope = qrope.shape[2]
    tile = ckv_t.shape[2]
    tq = ATTN_QUERY_TILE if seqlen % ATTN_QUERY_TILE == 0 else tile
    tk = ATTN_KEY_BLOCK if seqlen % ATTN_KEY_BLOCK == 0 else tile
    assert tq % tile == 0 and tk % tile == 0 and (tq % tk == 0 or tk % tq == 0)
    nq = seqlen // tq
    return pl.pallas_call(
        functools.partial(_mla_prompt_kernel, heads=heads, tq=tq, tk=tk),
        grid=(nb, nq),
        in_specs=[pl.BlockSpec((heads, tq, lora), lambda b, i: (0, b * nq + i, 0)),
                  pl.BlockSpec((heads, tq, rope), lambda b, i: (0, b * nq + i, 0)),
                  pl.BlockSpec((seqlen, lora), lambda b, i: (b, 0)),
                  pl.BlockSpec((seqlen, rope), lambda b, i: (b, 0)),
                  pl.BlockSpec((seqlen // tile, lora, tile), lambda b, i: (b, 0, 0))],
        out_specs=pl.BlockSpec((heads, tq, lora), lambda b, i: (0, b * nq + i, 0)),
        out_shape=jax.ShapeDtypeStruct((heads, t, lora), BF16),
        scratch_shapes=[pltpu.VMEM((1, heads * tq), F32), pltpu.VMEM((1, heads * tq), F32),
                        pltpu.VMEM((lora, heads * tq), F32)],
        compiler_params=_cparams(2),
        name="mla_prompt_attn",
    )(qlat, qrope, ckvb, kropeb, ckv_t)


def _mla_sample_kernel(pt_ref, ql_ref, qr_ref, kc_ref, kr_ref, lat_hbm, rope_hbm, o_ref,
                       lat_buf, rope_buf, kcb_ref, sem_lat, sem_rope, *, layer, npages, chunk, scale):
    b = pl.program_id(0)
    slot = b % 2

    def page_copies(seq, dst_slot, p):
        page = pt_ref[seq, p]
        return (pltpu.make_async_copy(lat_hbm.at[layer, page], lat_buf.at[dst_slot, p], sem_lat.at[dst_slot]),
                pltpu.make_async_copy(rope_hbm.at[layer, page], rope_buf.at[dst_slot, p], sem_rope.at[dst_slot]))

    def start_all(seq, dst_slot):
        def body(p, carry):
            for cp in page_copies(seq, dst_slot, p):
                cp.start()
            return carry
        lax.fori_loop(0, npages, body, 0, unroll=math.gcd(npages, 8))

    @pl.when(b == 0)
    def _():
        start_all(0, 0)

    @pl.when(b + 1 < pl.num_programs(0))
    def _():
        start_all(b + 1, 1 - slot)

    def wait_body(p, carry):
        for cp in page_copies(b, slot, p):
            cp.wait()
        return carry

    lax.fori_loop(0, npages, wait_body, 0, unroll=math.gcd(npages, 8))

    rows = ql_ref.shape[0] * ql_ref.shape[1]
    lora = ql_ref.shape[2]
    ql = ql_ref[...].reshape(rows, lora).astype(BF16)
    qr = qr_ref[...].reshape(rows, qr_ref.shape[2]).astype(BF16)

    ntok = kc_ref.shape[0]
    pad = PAGE_SIZE - ntok
    kc_own = jnp.concatenate([kc_ref[...], jnp.zeros((pad, lora), F32)], axis=0).astype(BF16)
    kr_own = jnp.concatenate([kr_ref[...], jnp.zeros((pad, kr_ref.shape[1]), F32)], axis=0).astype(BF16)
    s_own = (_dot_nt(ql, kc_own) + _dot_nt(qr, kr_own)) * scale
    q_tok = lax.broadcasted_iota(jnp.int32, s_own.shape, 0) % ntok
    k_tok = lax.broadcasted_iota(jnp.int32, s_own.shape, 1)
    scores = [jnp.where(k_tok <= q_tok, s_own, -jnp.inf)]
    ckeys = chunk * PAGE_SIZE
    for c in range(npages // chunk):
        keys = slice(c * ckeys, (c + 1) * ckeys)
        kcb_ref[keys, :] = lat_buf[slot, c * chunk:(c + 1) * chunk].reshape(ckeys, lora).astype(BF16)
        kr_t = jnp.concatenate([rope_buf[slot, c * chunk + k] for k in range(chunk)], axis=1).astype(BF16)
        scores.append((_dot_nt(ql, kcb_ref[keys, :]) + _dot(qr, kr_t)) * scale)
    m = functools.reduce(jnp.maximum, [jnp.max(s, axis=-1, keepdims=True) for s in scores])
    p = jnp.exp(scores[0] - m)
    l = jnp.sum(p, axis=-1, keepdims=True)
    acc = _dot(p.astype(BF16), kc_own)
    for c in range(npages // chunk):
        p = jnp.exp(scores[c + 1] - m)
        l = l + jnp.sum(p, axis=-1, keepdims=True)
        acc = acc + _dot(p.astype(BF16), kcb_ref[c * ckeys:(c + 1) * ckeys, :])
    o_ref[...] = (acc / l).reshape(o_ref.shape)


def _mla_sample_attn(qlat, qrope, ckv, krope, pool_lat, pool_rope_t, layer, page_table, *, seqlen, scale):
    heads, t, lora = qlat.shape
    rope = qrope.shape[2]
    nb, npages = page_table.shape
    chunk = math.gcd(npages, SAMPLE_KEY_PAGES)
    assert seqlen == SUBLANES
    grid_spec = pltpu.PrefetchScalarGridSpec(
        num_scalar_prefetch=1,
        grid=(nb,),
        in_specs=[pl.BlockSpec((heads, seqlen, lora), lambda b, pt: (0, b, 0)),
                  pl.BlockSpec((heads, seqlen, rope), lambda b, pt: (0, b, 0)),
                  pl.BlockSpec((seqlen, lora), lambda b, pt: (b, 0)),
                  pl.BlockSpec((seqlen, rope), lambda b, pt: (b, 0)),
                  pl.BlockSpec(memory_space=pl.ANY), pl.BlockSpec(memory_space=pl.ANY)],
        out_specs=pl.BlockSpec((heads, seqlen, lora), lambda b, pt: (0, b, 0)),
        scratch_shapes=[pltpu.VMEM((2, npages, PAGE_SIZE, lora), F32), pltpu.VMEM((2, npages, rope, PAGE_SIZE), F32),
                        pltpu.VMEM((npages * PAGE_SIZE, lora), BF16),
                        pltpu.SemaphoreType.DMA((2,)), pltpu.SemaphoreType.DMA((2,))],
    )
    return pl.pallas_call(
        functools.partial(_mla_sample_kernel, layer=layer, npages=npages, chunk=chunk, scale=scale),
        grid_spec=grid_spec,
        out_shape=jax.ShapeDtypeStruct((heads, t, lora), F32),
        compiler_params=_cparams(1),
        name="mla_sample_attn",
    )(page_table, qlat, qrope, ckv, krope, pool_lat, pool_rope_t)


def _mla_out_kernel(o_ref, x_ref, wuv_ref, wo_ref, pw_ref, out_ref, *, heads):
    parts = [_dot(o_ref[h].astype(BF16), wuv_ref[h]) for h in range(heads)]
    o = jnp.concatenate(parts, axis=1).astype(BF16)
    out_ref[...] = x_ref[...] + _rms(_dot(o, wo_ref[...]), pw_ref[...])


def _mla_out(o, x, wuv, wo, pw):
    heads, t, lora = o.shape
    d = x.shape[1]
    tm = _row_tile(t, 512)
    row = pl.BlockSpec((tm, d), lambda i: (i, 0))
    return pl.pallas_call(
        functools.partial(_mla_out_kernel, heads=heads),
        grid=(t // tm,),
        in_specs=[pl.BlockSpec((heads, tm, lora), lambda i: (0, i, 0)), row,
                  _full(wuv.shape), _full(wo.shape), _full(pw.shape)],
        out_specs=row,
        out_shape=jax.ShapeDtypeStruct((t, d), F32),
        compiler_params=_cparams(),
        name="mla_out",
    )(o, x, wuv, wo, pw)


def _pad_lanes(a, width=LANES):
    return jnp.pad(a, [(0, 0)] * (a.ndim - 1) + [(0, width - a.shape[-1])])


def _hist8(prev):
    return jnp.pad(prev, ((0, 0), (SUBLANES - prev.shape[1], 0), (0, 0)))


def _rope_tables(pos, rope, heads):
    half = rope // 2
    inv = ROPE_THETA ** (-jnp.arange(half, dtype=F32) / half)

    col = jnp.arange(LANES)
    ang = pos.astype(F32)[:, None] * inv[col % half][None, :]
    sign = jnp.where(col % rope < half, -1.0, 1.0).astype(F32)
    return jnp.cos(ang), jnp.sin(ang) * sign[None, :]


def _swap_halves(w, rope):
    k, n = w.shape
    w = w.reshape(k, n // rope, 2, rope // 2)
    return w[:, :, ::-1, :].reshape(k, n)


def _s5_params(a_re, a_im, log_step, b_re, b_im, c_re, c_im, d_model):
    g, s = a_re.shape
    grp = d_model // g
    delta = jnp.exp(log_step)[:, None]
    mag = jnp.exp(a_re * delta)
    ang = a_im * delta
    lam_re, lam_im = mag * jnp.cos(ang), mag * jnp.sin(ang)
    den = a_re * a_re + a_im * a_im
    nr, ni = lam_re - 1.0, lam_im
    f_re = (nr * a_re + ni * a_im) / den
    f_im = (ni * a_re - nr * a_im) / den
    bb_re = f_re[..., None] * b_re - f_im[..., None] * b_im
    bb_im = f_re[..., None] * b_im + f_im[..., None] * b_re
    kw = 256 if d_model % 256 == 0 else d_model
    kb = d_model // kw
    gpb = kw // grp

    def block_diag_in(bb):
        bb = bb.reshape(kb, gpb, s, grp)
        eye = jnp.eye(gpb, dtype=F32)
        w = jnp.einsum("kgsc,gh->kgchs", bb, eye)
        return w.reshape(kb, gpb * grp, gpb * s).astype(BF16)

    def block_diag_out(cc):
        cc = cc.reshape(kb, gpb, grp, s)
        eye = jnp.eye(gpb, dtype=F32)
        w = jnp.einsum("kgcs,gh->kgshc", cc, eye)
        return w.reshape(kb, gpb * s, gpb * grp).astype(BF16)

    nst = g * s // LANES
    return dict(wbr=block_diag_in(bb_re), wbi=block_diag_in(bb_im),
                wcr=block_diag_out(c_re), wci=block_diag_out(c_im),
                lam_re=lam_re.reshape(nst, LANES), lam_im=lam_im.reshape(nst, LANES))


def kernel(x_prompt, x_sample, state_ssd, state_ssd_conv, state_s5_re, state_s5_im, cache_mla_latent, cache_mla_krope, page_table, state_ffn_conv, norm_mix_pre, norm_mix_post, norm_ffn_pre, norm_ffn_post, ssd_w_in, ssd_conv_w, ssd_conv_b, ssd_dt_bias, ssd_a_log, ssd_d, ssd_norm, ssd_w_out, s5_a_re, s5_a_im, s5_log_step, s5_b_re, s5_b_im, s5_c_re, s5_c_im, s5_d, s5_w_val, s5_w_gate, mla_w_dq, mla_q_norm, mla_w_uq, mla_w_dkv, mla_kv_norm, mla_w_uk, mla_w_uv, mla_w_o, ffn_w_gate, ffn_w_up, ffn_conv_w, ffn_conv_b, ffn_w_down):
    b_p, l_p, d = x_prompt.shape
    b_s, l_s, _ = x_sample.shape
    depth = norm_mix_pre.shape[0]
    past_len = page_table.shape[1] * PAGE_SIZE
    groups_of = {"p": (b_p, l_p), "s": (b_s, l_s)}
    hid = {"p": x_prompt.reshape(b_p * l_p, d), "s": x_sample.reshape(b_s * l_s, d)}
    outs = {k: {n: [] for n in ("ssd_c", "s5r", "s5i", "lat", "kr", "ffc")} for k in ("p", "s")}
    ssd_h = {"p": None, "s": None}

    _, _, nh, hd, ds = state_ssd.shape
    di = nh * hd
    cd = state_ssd_conv.shape[-1]
    ssd_groups = (cd - di) // (2 * ds)
    s5_g, s5_s = s5_a_re.shape[1:]
    nst = s5_g * s5_s // LANES
    lora, heads, nope = mla_w_uk.shape[1:]
    rope = cache_mla_krope.shape[-1]
    mla_scale = 1.0 / math.sqrt(nope + rope)
    row1 = lambda a: a.reshape(1, -1)
    ffn_wg, ffn_wu, ffn_wd = ffn_w_gate.astype(BF16), ffn_w_up.astype(BF16), ffn_w_down.astype(BF16)

    for i in range(depth):
        kind, j = i % N_MIXERS, i // N_MIXERS
        if kind == 0:
            wz, wx = ssd_w_in[j, :, :di].astype(BF16), ssd_w_in[j, :, di:di + cd].astype(BF16)
            wdt = _pad_lanes(ssd_w_in[j, :, di + cd:]).astype(BF16)
            dtb, alog = (_pad_lanes(row1(a[j])) for a in (ssd_dt_bias, ssd_a_log))
            dsk = jnp.repeat(ssd_d[j], hd).reshape(1, di)
            wo = ssd_w_out[j].astype(BF16)
            for k, (nb, sl) in groups_of.items():
                if k == "p":
                    prev8 = jnp.zeros((nb * SUBLANES, cd), F32)
                    h0, h0_layer = jnp.zeros((1, nb, nh, hd, ds), F32), 0
                else:
                    prev8 = _hist8(state_ssd_conv[j]).reshape(nb * SUBLANES, cd)
                    h0, h0_layer = state_ssd, j
                z, xact, dt, cnew = _ssd_in(hid[k], prev8, row1(norm_mix_pre[i]), wz, wx, wdt, ssd_conv_w[j],
                                            row1(ssd_conv_b[j]), dtb, nb=nb, seqlen=sl)
                y, ssd_h[k] = _ssd_scan(xact, dt, h0, h0_layer, alog, dsk, ssd_h[k], nb=nb, seqlen=sl,
                                        groups=ssd_groups)
                hid[k] = _ssd_out(y, z, hid[k], row1(ssd_norm[j]), wo, row1(norm_mix_post[i]), groups=ssd_groups)
                outs[k]["ssd_c"].append(cnew.reshape(nb, SUBLANES, cd)[:, SUBLANES - (SSD_CONV - 1):, :])
        elif kind == 1:
            sp = _s5_params(s5_a_re[j], s5_a_im[j], s5_log_step[j], s5_b_re[j], s5_b_im[j], s5_c_re[j], s5_c_im[j], d)
            wv, wg = s5_w_val[j].astype(BF16), s5_w_gate[j].astype(BF16)
            for k, (nb, sl) in groups_of.items():
                if k == "p":
                    h0r = h0i = jnp.zeros((nb, nst, LANES), F32)
                    lamr, lami = sp["lam_re"], sp["lam_im"]
                else:
                    h0r, h0i = state_s5_re[j].reshape(nb, nst * LANES), state_s5_im[j].reshape(nb, nst * LANES)
                    lamr, lami = sp["lam_re"].reshape(1, -1), sp["lam_im"].reshape(1, -1)
                hid[k], hr, hi = _s5(hid[k], h0r, h0i, row1(norm_mix_pre[i]), sp["wbr"], sp["wbi"], lamr, lami,
                                     sp["wcr"], sp["wci"], row1(s5_d[j]), wv, wg, row1(norm_mix_post[i]),
                                     nb=nb, seqlen=sl)
                outs[k]["s5r"].append(hr.reshape(nb, s5_g, s5_s))
                outs[k]["s5i"].append(hi.reshape(nb, s5_g, s5_s))
        else:
            w_uq = mla_w_uq[j].reshape(-1, heads, nope + rope)
            wqn = w_uq[:, :, :nope].reshape(-1, heads * nope).astype(BF16)
            wqr = w_uq[:, :, nope:].reshape(-1, heads * rope).astype(BF16)
            wqrs = _swap_halves(wqr, rope)
            wkv = mla_w_dkv[j][:, :lora].astype(BF16)
            wkr = mla_w_dkv[j][:, lora:].astype(BF16)
            wkrs = _swap_halves(wkr, rope)
            wukt = jnp.transpose(mla_w_uk[j], (1, 2, 0)).astype(BF16)
            wuv = jnp.transpose(mla_w_uv[j], (1, 0, 2)).astype(BF16)
            wdq, wo = mla_w_dq[j].astype(BF16), mla_w_o[j].astype(BF16)
            for k, (nb, sl) in groups_of.items():
                t = nb * sl
                if k == "p":
                    tm = _row_tile(sl, 256)
                    tabs = _rope_tables(jnp.arange(sl), rope, heads)
                    pos_tiles = sl // tm
                else:
                    tm = _row_tile(t, 256)
                    tabs = _rope_tables(past_len + jnp.arange(tm) % sl, rope, heads)
                    pos_tiles = 1
                proj = _mla_proj(
                    hid[k], tabs, row1(norm_mix_pre[i]), wdq, row1(mla_q_norm[j]), wqn, wqr, wqrs, wukt, wkv, wkr, wkrs,
                    row1(mla_kv_norm[j]), pos_tiles=pos_tiles, tm=tm, for_prompt=k == "p",
                    qscale=mla_scale * LOG2E if k == "p" else 1.0)
                qlat, qrope, ckv, kr = proj[:4]
                if k == "p":
                    o = _mla_prompt_attn(qlat, qrope, *proj[4:], nb=nb, seqlen=sl)
                else:
                    o = _mla_sample_attn(qlat, qrope, ckv, kr, cache_mla_latent, jnp.swapaxes(cache_mla_krope, 2, 3),
                                         j, page_table, seqlen=sl, scale=mla_scale)
                hid[k] = _mla_out(o, hid[k], wuv, wo, row1(norm_mix_post[i]))
                outs[k]["lat"].append(ckv.reshape(nb, sl, lora))
                outs[k]["kr"].append(kr.reshape(nb, sl, rope))
        f = ffn_wg.shape[2]
        for k, (nb, sl) in groups_of.items():
            if k == "p":
                prev8 = jnp.zeros((nb * SUBLANES, f), F32)
            else:
                prev8 = _hist8(state_ffn_conv[i]).reshape(nb * SUBLANES, f)
            hid[k], cnew = _ffn(hid[k], prev8, row1(norm_ffn_pre[i]), ffn_wg, ffn_wu, ffn_conv_w[i],
                                row1(ffn_conv_b[i]), ffn_wd, row1(norm_ffn_post[i]), i, nb=nb, seqlen=sl)
            outs[k]["ffc"].append(cnew.reshape(nb, SUBLANES, f)[:, SUBLANES - (FFN_CONV - 1):, :])

    res = [hid["p"].reshape(b_p, l_p, d), hid["s"].reshape(b_s, l_s, d)]
    for k in ("p", "s"):
        res += [ssd_h[k]] + [jnp.stack(outs[k][n]) for n in ("ssd_c", "s5r", "s5i", "lat", "kr", "ffc")]
    return tuple(res)
```

```python
import functools
import math

import jax
import jax.numpy as jnp
from jax import lax
from jax.experimental import pallas as pl
from jax.experimental.pallas import tpu as pltpu

F32 = jnp.float32
BF16 = jnp.bfloat16
RMS_EPS = 1e-6
ROPE_THETA = 10000.0
LOG2E = math.log2(math.e)
PAGE_SIZE = 128
N_MIXERS = 3

SUBLANES = 8
LANES = 128
VMEM_LIMIT_BYTES = 56 * 1024 * 1024

SSD_CONV = 4
FFN_CONV = 3
CONV_CHUNK = 1024
ATTN_QUERY_TILE = 256
ATTN_KEY_BLOCK = 512
SAMPLE_KEY_PAGES = 16
SHORT_SEQ_CHUNK = 16
RESIDENT = pl.Buffered(1)


def _cparams(n_axes=1):
    return pltpu.CompilerParams(dimension_semantics=("arbitrary",) * n_axes,
                                vmem_limit_bytes=VMEM_LIMIT_BYTES)


def _full(shape):
    nd = len(shape)
    return pl.BlockSpec(shape, lambda i: (0,) * nd, pipeline_mode=RESIDENT)


def _layer_of(stacked, layer):
    nd = stacked.ndim - 1
    return pl.BlockSpec((None,) + stacked.shape[1:], lambda i: (layer,) + (0,) * nd, pipeline_mode=RESIDENT)


def _row_tile(n, pref):
    t = min(n, pref)
    while n % t or t % SUBLANES:
        t -= SUBLANES
    assert t > 0
    return t


def _stream_dtype(seqlen):
    return F32 if seqlen == SUBLANES else BF16


def _rms(x, w):
    return x * lax.rsqrt(jnp.mean(x * x, axis=-1, keepdims=True) + RMS_EPS) * w


def _sigmoid(x):
    return 0.5 * jnp.tanh(0.5 * x) + 0.5


def _silu(x):
    h = 0.5 * x
    return h * jnp.tanh(h) + h


def _softplus(x):
    return jnp.maximum(x, 0.0) + jnp.log1p(jnp.exp(-jnp.abs(x)))


def _dot(a, b):
    return jnp.dot(a, b, preferred_element_type=F32)


def _dot_nt(a, b):
    return lax.dot_general(a, b, (((1,), (1,)), ((), ())), preferred_element_type=F32)


def _dot_tn(a, b):
    return lax.dot_general(a, b, (((0,), (0,)), ((), ())), preferred_element_type=F32)


def _shifted_rows(x, prev8, s, seg):
    if s == 0:
        return x
    rows = x.shape[0]
    rolled = pltpu.roll(x, s, axis=0)
    if seg == rows:
        head = jnp.where(lax.broadcasted_iota(jnp.int32, (SUBLANES, x.shape[1]), 0) < s,
                         pltpu.roll(prev8, s, axis=0), rolled[:SUBLANES])
        if rows == SUBLANES:
            return head
        return jnp.concatenate([head, rolled[SUBLANES:]], axis=0)
    assert seg == SUBLANES
    hist = pltpu.roll(prev8, (rows + s - SUBLANES) % rows, axis=0) if rows > SUBLANES else pltpu.roll(prev8, s, axis=0)
    t = lax.broadcasted_iota(jnp.int32, x.shape, 0) % SUBLANES
    return jnp.where(t < s, hist, rolled)


def _causal_conv(x, prev8, w, b, seg):
    k = w.shape[0]
    acc = b
    for j in range(k):
        acc = acc + _shifted_rows(x, prev8, k - 1 - j, seg) * w[j:j + 1, :]
    return acc


def _carried_conv(hist_ref, cols, x, w, b):
    acc = _causal_conv(x, hist_ref[:, cols], w, b, x.shape[0])
    hist_ref[:, cols] = x[x.shape[0] - SUBLANES:, :]
    return acc


def _col_chunks(n, width):
    return [slice(lo, min(lo + width, n)) for lo in range(0, n, width)]


def _ssd_in_kernel(x_ref, prev_ref, nw_ref, wz_ref, wx_ref, wdt_ref, cw_ref, cb_ref, dtb_ref,
                   z_ref, xact_ref, dt_ref, convnew_ref, hist_ref, *, seg, tiles_per_seq):
    tm = x_ref.shape[0]
    u = _rms(x_ref[...], nw_ref[...]).astype(BF16)
    if seg == tm:
        @pl.when(pl.program_id(0) % tiles_per_seq == 0)
        def _():
            hist_ref[...] = prev_ref[...]

    for cols in _col_chunks(wx_ref.shape[1], CONV_CHUNK):
        xbc = _dot(u, wx_ref[:, cols])
        if seg == tm:
            conv = _carried_conv(hist_ref, cols, xbc, cw_ref[:, cols], cb_ref[:, cols])
            convnew_ref[:, cols] = xbc[tm - SUBLANES:, :]
        else:
            conv = _causal_conv(xbc, prev_ref[:, cols], cw_ref[:, cols], cb_ref[:, cols], seg)
            convnew_ref[:, cols] = xbc
        xact_ref[:, cols] = _silu(conv)
    z_ref[...] = _dot(u, wz_ref[...]).astype(z_ref.dtype)
    dt_ref[...] = _softplus(_dot(u, wdt_ref[...]) + dtb_ref[...])


def _ssd_in(x, prev8, nw, wz, wx, wdt, cw, cb, dtb, *, nb, seqlen):
    t, d = x.shape
    di, cd, dp = wz.shape[1], wx.shape[1], wdt.shape[1]
    if seqlen == SUBLANES:
        tm = _row_tile(t, 256)
        seg, tps = SUBLANES, 1
        hist_spec = pl.BlockSpec((tm, cd), lambda i: (i, 0))
    else:
        tm = _row_tile(seqlen, 256)
        seg, tps = tm, seqlen // tm
        hist_spec = pl.BlockSpec((SUBLANES, cd), lambda i: (i // tps, 0))
    row = lambda n: pl.BlockSpec((tm, n), lambda i: (i, 0))
    consts = [nw, wz, wx, wdt, cw, cb, dtb]
    return pl.pallas_call(
        functools.partial(_ssd_in_kernel, seg=seg, tiles_per_seq=tps),
        grid=(t // tm,),
        in_specs=[row(d), hist_spec] + [_full(a.shape) for a in consts],
        out_specs=[row(di), row(cd), row(dp), hist_spec],
        out_shape=[jax.ShapeDtypeStruct((t, di), _stream_dtype(seqlen)), jax.ShapeDtypeStruct((t, cd), F32),
                   jax.ShapeDtypeStruct((t, dp), F32), jax.ShapeDtypeStruct((nb * SUBLANES, cd), F32)],
        scratch_shapes=[pltpu.VMEM((SUBLANES, cd), F32)],
        compiler_params=_cparams(),
        name="ssd_in",
    )(x, prev8, *consts)


def _cumsum_rows(x):
    rows = x.shape[0]
    row = lax.broadcasted_iota(jnp.int32, x.shape, 0)
    k = 1
    while k < rows:
        x = x + jnp.where(row >= k, pltpu.roll(x, k, axis=0), 0.0)
        k *= 2
    return x


def _ssd_scan_kernel(xact_ref, dt_ref, h0_ref, alog_ref, dsk_ref, expand_ref, *rest,
                     lt, q, groups, hpg, hd, ds, nc, n_prev):
    hprev_ref = rest[0] if n_prev else None
    y_ref, hfin_ref, xpad_ref, dtpad_ref, h_ref = rest[1 if n_prev else 0:]
    c = pl.program_id(1)
    di = groups * hpg * hd
    gn = groups * ds

    @pl.when(c == 0)
    def _():
        h_ref[...] = h0_ref[0, 0]

    if lt < q:
        xpad_ref[...] = jnp.zeros_like(xpad_ref)
        xpad_ref[0:lt, :] = xact_ref[...]
        dtpad_ref[...] = jnp.zeros_like(dtpad_ref)
        dtpad_ref[0:lt, :] = dt_ref[...]
        xact = xpad_ref[...]
        dt = dtpad_ref[...]
    else:
        xact = xact_ref[...]
        dt = dt_ref[...]
    xs = xact[:, :di]
    bm = xact[:, di:di + gn]
    cm = xact[:, di + gn:]
    a = -jnp.exp(alog_ref[...])
    cs = _cumsum_rows(dt * a) * LOG2E
    cs_last = cs[q - 1:q, :]
    nh = groups * hpg
    assert 3 * nh <= LANES
    ecs = jnp.exp2(cs)
    hi = ecs.astype(BF16).astype(F32)
    mid = (ecs - hi).astype(BF16).astype(F32)
    lo = ecs - hi - mid
    lane = lax.broadcasted_iota(jnp.int32, ecs.shape, 1)
    packed = jnp.where(lane < nh, hi, jnp.where(lane < 2 * nh, pltpu.roll(mid, nh, axis=1), pltpu.roll(lo, 2 * nh, axis=1)))
    ecs_x = _dot(packed.astype(BF16), expand_ref[...])
    e_last = jnp.exp2(cs_last)
    cs_t = cs.T
    dt_t = dt.T
    w_t = (jnp.exp2(cs_last - cs) * dt).T
    causal = (lax.broadcasted_iota(jnp.int32, (q, q), 0) >= lax.broadcasted_iota(jnp.int32, (q, q), 1))
    even = lax.broadcasted_iota(jnp.int32, (q, 2 * hd), 1) < hd
    assert 2 * hd == LANES and hpg % 2 == 0
    y_pairs = []
    for g in range(groups):
        bg = bm[:, g * ds:(g + 1) * ds].astype(BF16)
        cg = cm[:, g * ds:(g + 1) * ds].astype(BF16)
        cb = _dot_nt(cg, bg)
        hg = h_ref[g * hpg:(g + 1) * hpg].reshape(hpg * hd, ds)
        yoff = _dot_nt(cg, hg.astype(BF16))
        xs_g = xs[:, g * hpg * hd:(g + 1) * hpg * hd]
        for k in range(hpg // 2):
            pair = xs_g[:, k * LANES:(k + 1) * LANES].astype(BF16)
            yd = []
            for h in (g * hpg + 2 * k, g * hpg + 2 * k + 1):
                seg = cs[:, h:h + 1] - cs_t[h:h + 1, :]
                m = cb * jnp.exp2(jnp.where(causal, seg, -jnp.inf)) * dt_t[h:h + 1, :]
                yd.append(_dot(m.astype(BF16), pair))
            lo = g * hpg * hd + k * LANES
            y_pairs.append(jnp.where(even, yd[0], yd[1]) + yoff[:, k * LANES:(k + 1) * LANES] * ecs_x[:, lo:lo + LANES])
        xs_gt = xs_g.T
        xw = [xs_gt[r * hd:(r + 1) * hd, :] * w_t[g * hpg + r:g * hpg + r + 1, :] for r in range(hpg)]
        s_g = _dot(jnp.concatenate(xw, axis=0).astype(BF16), bg)
        for r in range(hpg):
            h = g * hpg + r
            h_ref[h] = h_ref[h] * e_last[:, h:h + 1] + s_g[r * hd:(r + 1) * hd, :]
    y = jnp.concatenate(y_pairs, axis=1) + dsk_ref[...] * xs
    y_ref[...] = y[0:lt, :].astype(y_ref.dtype)

    @pl.when(c == nc - 1)
    def _():
        for k in range(n_prev):
            hfin_ref[k, 0] = hprev_ref[k, 0]
        hfin_ref[n_prev, 0] = h_ref[...]


def _ssd_scan(xact, dt, h0, layer, alog, dsk, h_prev, *, nb, seqlen, groups):
    _, _, nh, hd, ds = h0.shape
    n_prev = 0 if h_prev is None else h_prev.shape[0]
    state_blk = lambda n: pl.BlockSpec((n, 1, nh, hd, ds), lambda b, c: (0, b, 0, 0, 0))
    state_depth = None
    piece_row = jnp.arange(LANES)[:, None]
    expand = ((jnp.arange(nh * hd)[None, :] // hd == piece_row % nh) & (piece_row < 3 * nh)).astype(BF16)
    cd = xact.shape[1]
    hpg = nh // groups
    di = nh * hd
    if seqlen >= 256:
        lt = q = 256
    else:
        lt, q = seqlen, SHORT_SEQ_CHUNK
    assert seqlen % lt == 0 and lt % SUBLANES == 0
    nc = seqlen // lt
    pad_rows = q if lt < q else SUBLANES
    kern = functools.partial(_ssd_scan_kernel, lt=lt, q=q, groups=groups, hpg=hpg, hd=hd, ds=ds, nc=nc,
                             n_prev=n_prev)
    row = lambda n: pl.BlockSpec((lt, n), lambda b, c: (b * nc + c, 0))
    par = lambda a: pl.BlockSpec(a.shape, lambda b, c: (0,) * a.ndim)
    return pl.pallas_call(
        kern,
        grid=(nb, nc),
        in_specs=[row(cd), row(dt.shape[1]),
                  pl.BlockSpec((1, 1, nh, hd, ds), lambda b, c: (layer, b, 0, 0, 0), pipeline_mode=state_depth),
                  par(alog), par(dsk), par(expand)]
        + ([pl.BlockSpec((n_prev, 1, nh, hd, ds), lambda b, c: (0, b, 0, 0, 0), pipeline_mode=state_depth)]
           if n_prev else []),
        out_specs=[row(di), state_blk(n_prev + 1)],
        out_shape=[jax.ShapeDtypeStruct((nb * seqlen, di), _stream_dtype(seqlen)),
                   jax.ShapeDtypeStruct((n_prev + 1, nb, nh, hd, ds), F32)],
        scratch_shapes=[pltpu.VMEM((pad_rows, cd), F32), pltpu.VMEM((pad_rows, dt.shape[1]), F32),
                        pltpu.VMEM((nh, hd, ds), F32)],
        compiler_params=_cparams(2),
        name="ssd_scan",
    )(xact, dt, h0, alog, dsk, expand, *([h_prev] if n_prev else []))


def _ssd_out_kernel(y_ref, z_ref, x_ref, gw_ref, wo_ref, pw_ref, o_ref, *, groups):
    yg = y_ref[...].astype(F32) * _silu(z_ref[...].astype(F32))
    gs = yg.shape[1] // groups
    parts = []
    for g in range(groups):
        blk = yg[:, g * gs:(g + 1) * gs]
        parts.append(blk * lax.rsqrt(jnp.mean(blk * blk, axis=-1, keepdims=True) + RMS_EPS))
    yn = (jnp.concatenate(parts, axis=1) * gw_ref[...]).astype(BF16)
    o_ref[...] = x_ref[...] + _rms(_dot(yn, wo_ref[...]), pw_ref[...])


def _ssd_out(y, z, x, gw, wo, pw, *, groups):
    t, d = x.shape
    tm = _row_tile(t, 512)
    row = lambda n: pl.BlockSpec((tm, n), lambda i: (i, 0))
    return pl.pallas_call(
        functools.partial(_ssd_out_kernel, groups=groups),
        grid=(t // tm,),
        in_specs=[row(y.shape[1]), row(z.shape[1]), row(d), _full(gw.shape), _full(wo.shape), _full(pw.shape)],
        out_specs=row(d),
        out_shape=jax.ShapeDtypeStruct((t, d), F32),
        compiler_params=_cparams(),
        name="ssd_out",
    )(y, z, x, gw, wo, pw)


def _ffn_kernel(x_ref, prev_ref, nw_ref, wg_ref, wu_ref, cw_ref, cb_ref, wd_ref, pw_ref,
                o_ref, convnew_ref, hist_ref, *, seg, tiles_per_seq):
    x = x_ref[...]
    tm = x.shape[0]
    u = _rms(x, nw_ref[...]).astype(BF16)
    if seg == tm:
        @pl.when(pl.program_id(0) % tiles_per_seq == 0)
        def _():
            hist_ref[...] = prev_ref[...]

    out = None
    for cols in _col_chunks(wg_ref.shape[1], CONV_CHUNK):
        g = _dot(u, wg_ref[:, cols])
        up = _dot(u, wu_ref[:, cols])
        if seg == tm:
            gc = _carried_conv(hist_ref, cols, g, cw_ref[:, cols], cb_ref[:, cols])
            convnew_ref[:, cols] = g[tm - SUBLANES:, :]
        else:
            gc = _causal_conv(g, prev_ref[:, cols], cw_ref[:, cols], cb_ref[:, cols], seg)
            convnew_ref[:, cols] = g
        part = _dot((_silu(gc) * up).astype(BF16), wd_ref[cols, :])
        out = part if out is None else out + part
    o_ref[...] = x + _rms(out, pw_ref[...])


def _ffn(x, prev8, nw, wg, wu, cw, cb, wd, pw, layer, *, nb, seqlen):
    t, d = x.shape
    f = wg.shape[2]
    if seqlen == SUBLANES:
        tm = _row_tile(t, 256)
        seg, tps = SUBLANES, 1
        prev_spec = pl.BlockSpec((tm, f), lambda i: (i, 0))
        new_spec = pl.BlockSpec((tm, f), lambda i: (i, 0))
    else:
        tm = _row_tile(seqlen, 1024)
        seg, tps = tm, seqlen // tm
        prev_spec = pl.BlockSpec((SUBLANES, f), lambda i: (i // tps, 0))
        new_spec = pl.BlockSpec((SUBLANES, f), lambda i: (i // tps, 0))
    row = pl.BlockSpec((tm, d), lambda i: (i, 0))
    return pl.pallas_call(
        functools.partial(_ffn_kernel, seg=seg, tiles_per_seq=tps),
        grid=(t // tm,),
        in_specs=[row, prev_spec, _full(nw.shape), _layer_of(wg, layer), _layer_of(wu, layer), _full(cw.shape),
                  _full(cb.shape), _layer_of(wd, layer), _full(pw.shape)],
        out_specs=[row, new_spec],
        out_shape=[jax.ShapeDtypeStruct((t, d), F32), jax.ShapeDtypeStruct((nb * SUBLANES, f), F32)],
        scratch_shapes=[pltpu.VMEM((SUBLANES, f), F32)],
        compiler_params=_cparams(),
        name="conv_ffn",
    )(x, prev8, nw, wg, wu, cw, cb, wd, pw)


def _gelu(x):
    return 0.5 * x * (1.0 + lax.erf(x * (1.0 / math.sqrt(2.0))))


def _s5_token_pitch(nst):
    groups = -(-nst // SUBLANES)
    return SUBLANES * (groups + 1 - groups % 2)


def _s5_kernel(x_ref, h0r_ref, h0i_ref, nw_ref, wbr_ref, wbi_ref, lamr_ref, lami_ref, wcr_ref, wci_ref,
               dsk_ref, wv_ref, wg_ref, pw_ref, o_ref, hr_out_ref, hi_out_ref, sr_ref, si_ref, cr_ref, ci_ref,
               *, long_seq, tiles_per_seq, kb, nst):
    x = x_ref[...]
    tm, d = x.shape
    u = _rms(x, nw_ref[...])
    ub = u.astype(BF16)
    kw = d // kb
    st_per_kb = nst // kb
    if long_seq:
        i = pl.program_id(0)

        @pl.when(i % tiles_per_seq == 0)
        def _():
            cr_ref[...] = h0r_ref[0]
            ci_ref[...] = h0i_ref[0]

        pitch = _s5_token_pitch(nst)
        for b in range(kb):
            blk = ub[:, b * kw:(b + 1) * kw]
            pr = _dot(blk, wbr_ref[b])
            pi = _dot(blk, wbi_ref[b])
            for j in range(st_per_kb):
                st = b * st_per_kb + j
                sr_ref[pl.ds(st, tm, stride=pitch), :] = pr[:, j * LANES:(j + 1) * LANES]
                si_ref[pl.ds(st, tm, stride=pitch), :] = pi[:, j * LANES:(j + 1) * LANES]
        lr = lamr_ref[...]
        li = lami_ref[...]

        def step(t, carry):
            hr, hi = carry
            off = pl.multiple_of(t * pitch, SUBLANES)
            br = sr_ref[pl.ds(off, nst), :]
            bi = si_ref[pl.ds(off, nst), :]
            nr = lr * hr - li * hi + br
            ni = lr * hi + li * hr + bi
            sr_ref[pl.ds(off, nst), :] = nr
            si_ref[pl.ds(off, nst), :] = ni
            return nr, ni

        hr, hi = lax.fori_loop(0, tm, step, (cr_ref[...], ci_ref[...]), unroll=8)
        cr_ref[...] = hr
        ci_ref[...] = hi
        hr_out_ref[0] = hr
        hi_out_ref[0] = hi
        load_r = lambda st: sr_ref[pl.ds(st, tm, stride=pitch), :]
        load_i = lambda st: si_ref[pl.ds(st, tm, stride=pitch), :]
    else:
        nseq = tm // SUBLANES
        for b in range(kb):
            blk = ub[:, b * kw:(b + 1) * kw]
            pr = _dot(blk, wbr_ref[b])
            pi = _dot(blk, wbi_ref[b])
            for j in range(st_per_kb):
                sr_ref[b * st_per_kb + j] = pr[:, j * LANES:(j + 1) * LANES]
                si_ref[b * st_per_kb + j] = pi[:, j * LANES:(j + 1) * LANES]
        for st in range(nst):
            lanes = slice(st * LANES, (st + 1) * LANES)
            lr = lamr_ref[:, lanes]
            li = lami_ref[:, lanes]
            hr = h0r_ref[:, lanes]
            hi = h0i_ref[:, lanes]
            for t in range(SUBLANES):
                rows_t = pl.ds(t, nseq, stride=SUBLANES)
                br = sr_ref[st, rows_t, :]
                bi = si_ref[st, rows_t, :]
                hr, hi = lr * hr - li * hi + br, lr * hi + li * hr + bi
                sr_ref[st, rows_t, :] = hr
                si_ref[st, rows_t, :] = hi
            hr_out_ref[:, lanes] = hr
            hi_out_ref[:, lanes] = hi
        load_r = lambda st: sr_ref[st]
        load_i = lambda st: si_ref[st]

    ys = []
    for b in range(kb):
        hr_b = jnp.concatenate([load_r(b * st_per_kb + j) for j in range(st_per_kb)], axis=1).astype(BF16)
        hi_b = jnp.concatenate([load_i(b * st_per_kb + j) for j in range(st_per_kb)], axis=1).astype(BF16)
        ys.append(_dot(hr_b, wcr_ref[b]) - _dot(hi_b, wci_ref[b]))
    y = jnp.concatenate(ys, axis=1) + dsk_ref[...] * u
    gl = _gelu(y).astype(BF16)
    val = _dot(gl, wv_ref[...])
    gate = _dot(gl, wg_ref[...])
    out = val * _sigmoid(gate)
    o_ref[...] = x + _rms(out, pw_ref[...])


def _s5(x, h0r, h0i, nw, wbr, wbi, lamr, lami, wcr, wci, dsk, wv, wg, pw, *, nb, seqlen):
    t, d = x.shape
    kb = wbr.shape[0]
    nst = kb * wbr.shape[2] // LANES
    long_seq = seqlen != SUBLANES
    if long_seq:
        tm = _row_tile(seqlen, 256)
        tps = seqlen // tm
        st_spec = pl.BlockSpec((1, nst, LANES), lambda i: (i // tps, 0, 0))
        st_shape = jax.ShapeDtypeStruct((nb, nst, LANES), F32)
        rows = tm * _s5_token_pitch(nst)
        scratch = [pltpu.VMEM((rows, LANES), F32), pltpu.VMEM((rows, LANES), F32),
                   pltpu.VMEM((nst, LANES), F32), pltpu.VMEM((nst, LANES), F32)]
    else:
        tm = _row_tile(t, 256)
        tps = 1
        ns = tm // SUBLANES
        st_spec = pl.BlockSpec((ns, nst * LANES), lambda i: (i, 0))
        st_shape = jax.ShapeDtypeStruct((nb, nst * LANES), F32)
        scratch = [pltpu.VMEM((nst, tm, LANES), F32), pltpu.VMEM((nst, tm, LANES), F32),
                   pltpu.VMEM((SUBLANES, LANES), F32), pltpu.VMEM((SUBLANES, LANES), F32)]
    row = pl.BlockSpec((tm, d), lambda i: (i, 0))
    consts = [nw, wbr, wbi, lamr, lami, wcr, wci, dsk, wv, wg, pw]
    return pl.pallas_call(
        functools.partial(_s5_kernel, long_seq=long_seq, tiles_per_seq=tps, kb=kb, nst=nst),
        grid=(t // tm,),
        in_specs=[row, st_spec, st_spec] + [_full(a.shape) for a in consts],
        out_specs=[row, st_spec, st_spec],
        out_shape=[jax.ShapeDtypeStruct((t, d), F32), st_shape, st_shape],
        scratch_shapes=scratch,
        compiler_params=_cparams(),
        name="s5",
    )(x, h0r, h0i, *consts)


def _mla_proj_kernel(x_ref, cos_ref, sin_ref, nw_ref, wdq_ref, qn_ref, wqn_ref,
                     wqr_ref, wqrs_ref, wukt_ref, wkv_ref, wkr_ref, wkrs_ref, kvn_ref,
                     qlat_ref, qrope_ref, ckv_ref, krope_ref, *key_refs, heads, nope, rope, qscale):
    u = _rms(x_ref[...], nw_ref[...]).astype(BF16)
    cq = _rms(_dot(u, wdq_ref[...]), qn_ref[...]).astype(BF16)
    qn = _dot(cq, wqn_ref[...])
    reps = heads * rope // LANES
    qr = (_dot(cq, wqr_ref[...]) * jnp.tile(cos_ref[...], (1, reps))
          + _dot(cq, wqrs_ref[...]) * jnp.tile(sin_ref[...], (1, reps)))
    ckv = _rms(_dot(u, wkv_ref[...]), kvn_ref[...])
    kr = _dot(u, wkr_ref[...]) * cos_ref[:, :rope] + _dot(u, wkrs_ref[...]) * sin_ref[:, :rope]
    ckv_ref[...] = ckv
    krope_ref[...] = kr
    for h in range(heads):
        ql = _dot(qn[:, h * nope:(h + 1) * nope].astype(BF16), wukt_ref[h])
        qlat_ref[h] = (ql * qscale).astype(qlat_ref.dtype)
        qrope_ref[h] = (qr[:, h * rope:(h + 1) * rope] * qscale).astype(qrope_ref.dtype)
    if key_refs:
        ckvb_ref, kropeb_ref, ckvt_ref = key_refs
        ckvb_ref[...] = ckv.astype(BF16)
        kropeb_ref[...] = kr.astype(BF16)
        ckvt_ref[0] = ckv.T.astype(BF16)


def _mla_proj(x, tabs, nw, wdq, qn, wqn, wqr, wqrs, wukt, wkv, wkr, wkrs, kvn, *, pos_tiles, tm, for_prompt,
              qscale):
    t, d = x.shape
    heads, nope, lora = wukt.shape
    rope = wkr.shape[1]
    cos_tab, sin_tab = tabs
    assert (heads * rope) % LANES == 0 and LANES % rope == 0
    row = lambda n: pl.BlockSpec((tm, n), lambda i: (i, 0))
    tab = lambda n: pl.BlockSpec((tm, n), lambda i: (i % pos_tiles, 0))
    consts = [nw, wdq, qn, wqn, wqr, wqrs, wukt, wkv, wkr, wkrs, kvn]
    qdtype = BF16 if for_prompt else F32
    out_specs = [pl.BlockSpec((heads, tm, lora), lambda i: (0, i, 0)),
                 pl.BlockSpec((heads, tm, rope), lambda i: (0, i, 0)), row(lora), row(rope)]
    out_shape = [jax.ShapeDtypeStruct((heads, t, lora), qdtype), jax.ShapeDtypeStruct((heads, t, rope), qdtype),
                 jax.ShapeDtypeStruct((t, lora), F32), jax.ShapeDtypeStruct((t, rope), F32)]
    if for_prompt:
        out_specs += [row(lora), row(rope), pl.BlockSpec((1, lora, tm), lambda i: (i, 0, 0))]
        out_shape += [jax.ShapeDtypeStruct((t, lora), BF16), jax.ShapeDtypeStruct((t, rope), BF16),
                      jax.ShapeDtypeStruct((t // tm, lora, tm), BF16)]
    return pl.pallas_call(
        functools.partial(_mla_proj_kernel, heads=heads, nope=nope, rope=rope, qscale=qscale),
        grid=(t // tm,),
        in_specs=[row(d), tab(LANES), tab(LANES)]
        + [_full(a.shape) for a in consts],
        out_specs=out_specs,
        out_shape=out_shape,
        compiler_params=_cparams(),
        name="mla_proj",
    )(x, cos_tab, sin_tab, *consts)


def _mla_prompt_kernel(ql_ref, qr_ref, kc_ref, kr_ref, vt_ref, o_ref, m_ref, l_ref, acc_ref,
                       *, heads, tq, tk):
    qi = pl.program_id(1)
    tile = vt_ref.shape[2]
    per_block = tk // tile
    first_own = qi * (tq // tile)
    n_full = first_own // per_block
    m_ref[...] = jnp.full_like(m_ref, -jnp.inf)
    l_ref[...] = jnp.zeros_like(l_ref)
    acc_ref[...] = jnp.zeros_like(acc_ref)

    nqh = heads * tq
    ql = ql_ref[...].reshape(nqh, ql_ref.shape[2])
    qr = qr_ref[...].reshape(nqh, qr_ref.shape[2])

    def block(first_tile, ntiles, masked):
        nk = ntiles * tile
        off = pl.multiple_of(first_tile * tile, tile)
        kc = kc_ref[pl.ds(off, nk), :]
        kr = kr_ref[pl.ds(off, nk), :]
        s = _dot_nt(kc, ql) + _dot_nt(kr, qr)
        if masked:
            key = off + lax.broadcasted_iota(jnp.int32, (nk, nqh), 0)
            qry = qi * tq + lax.broadcasted_iota(jnp.int32, (nk, nqh), 1) % tq
            s = jnp.where(key <= qry, s, -jnp.inf)
        m_prev = m_ref[...]
        m_new = jnp.maximum(m_prev, jnp.max(s, axis=0, keepdims=True))
        alpha = jnp.exp2(m_prev - m_new)
        p = jnp.exp2(s - m_new)
        l_ref[...] = alpha * l_ref[...] + jnp.sum(p, axis=0, keepdims=True)
        vt = jnp.concatenate([vt_ref[first_tile + j] for j in range(ntiles)], axis=1)
        acc_ref[...] = alpha * acc_ref[...] + _dot(vt, p.astype(BF16))
        m_ref[...] = m_new

    def body(ki, carry):
        block(ki * per_block, per_block, False)
        return carry

    lax.fori_loop(0, n_full, body, 0)
    for rem in range(1, per_block):
        @pl.when(first_own - n_full * per_block == rem)
        def _():
            block(n_full * per_block, rem, False)
    block(first_own, tq // tile, True)
    o_t = acc_ref[...] / l_ref[...]
    for h in range(heads):
        o_ref[h] = o_t[:, h * tq:(h + 1) * tq].T.astype(BF16)


def _mla_prompt_attn(qlat, qrope, ckvb, kropeb, ckv_t, *, nb, seqlen):
    heads, t, lora = qlat.shape
    rope = qrope.shape[2]
    tile = ckv_t.shape[2]
    tq = ATTN_QUERY_TILE if seqlen % ATTN_QUERY_TILE == 0 else tile
    tk = ATTN_KEY_BLOCK if seqlen % ATTN_KEY_BLOCK == 0 else tile
    assert tq % tile == 0 and tk % tile == 0 and (tq % tk == 0 or tk % tq == 0)
    nq = seqlen // tq
    return pl.pallas_call(
        functools.partial(_mla_prompt_kernel, heads=heads, tq=tq, tk=tk),
        grid=(nb, nq),
        in_specs=[pl.BlockSpec((heads, tq, lora), lambda b, i: (0, b * nq + i, 0)),
                  pl.BlockSpec((heads, tq, rope), lambda b, i: (0, b * nq + i, 0)),
                  pl.BlockSpec((seqlen, lora), lambda b, i: (b, 0)),
                  pl.BlockSpec((seqlen, rope), lambda b, i: (b, 0)),
                  pl.BlockSpec((seqlen // tile, lora, tile), lambda b, i: (b, 0, 0))],
        out_specs=pl.BlockSpec((heads, tq, lora), lambda b, i: (0, b * nq + i, 0)),
        out_shape=jax.ShapeDtypeStruct((heads, t, lora), BF16),
        scratch_shapes=[pltpu.VMEM((1, heads * tq), F32), pltpu.VMEM((1, heads * tq), F32),
                        pltpu.VMEM((lora, heads * tq), F32)],
        compiler_params=_cparams(2),
        name="mla_prompt_attn",
    )(qlat, qrope, ckvb, kropeb, ckv_t)


def _mla_sample_kernel(pt_ref, ql_ref, qr_ref, kc_ref, kr_ref, lat_hbm, rope_hbm, o_ref,
                       lat_buf, rope_buf, kcb_ref, sem_lat, sem_rope, *, layer, npages, chunk, scale):
    b = pl.program_id(0)
    slot = b % 2

    def page_copies(seq, dst_slot, p):
        page = pt_ref[seq, p]
        return (pltpu.make_async_copy(lat_hbm.at[layer, page], lat_buf.at[dst_slot, p], sem_lat.at[dst_slot]),
                pltpu.make_async_copy(rope_hbm.at[layer, page], rope_buf.at[dst_slot, p], sem_rope.at[dst_slot]))

    def start_all(seq, dst_slot):
        def body(p, carry):
            for cp in page_copies(seq, dst_slot, p):
                cp.start()
            return carry
        lax.fori_loop(0, npages, body, 0, unroll=math.gcd(npages, 8))

    @pl.when(b == 0)
    def _():
        start_all(0, 0)

    @pl.when(b + 1 < pl.num_programs(0))
    def _():
        start_all(b + 1, 1 - slot)

    def wait_body(p, carry):
        for cp in page_copies(b, slot, p):
            cp.wait()
        return carry

    lax.fori_loop(0, npages, wait_body, 0, unroll=math.gcd(npages, 8))

    rows = ql_ref.shape[0] * ql_ref.shape[1]
    lora = ql_ref.shape[2]
    ql = ql_ref[...].reshape(rows, lora).astype(BF16)
    qr = qr_ref[...].reshape(rows, qr_ref.shape[2]).astype(BF16)

    ntok = kc_ref.shape[0]
    pad = PAGE_SIZE - ntok
    kc_own = jnp.concatenate([kc_ref[...], jnp.zeros((pad, lora), F32)], axis=0).astype(BF16)
    kr_own = jnp.concatenate([kr_ref[...], jnp.zeros((pad, kr_ref.shape[1]), F32)], axis=0).astype(BF16)
    s_own = (_dot_nt(ql, kc_own) + _dot_nt(qr, kr_own)) * scale
    q_tok = lax.broadcasted_iota(jnp.int32, s_own.shape, 0) % ntok
    k_tok = lax.broadcasted_iota(jnp.int32, s_own.shape, 1)
    scores = [jnp.where(k_tok <= q_tok, s_own, -jnp.inf)]
    ckeys = chunk * PAGE_SIZE
    for c in range(npages // chunk):
        keys = slice(c * ckeys, (c + 1) * ckeys)
        kcb_ref[keys, :] = lat_buf[slot, c * chunk:(c + 1) * chunk].reshape(ckeys, lora).astype(BF16)
        kr_t = jnp.concatenate([rope_buf[slot, c * chunk + k] for k in range(chunk)], axis=1).astype(BF16)
        scores.append((_dot_nt(ql, kcb_ref[keys, :]) + _dot(qr, kr_t)) * scale)
    m = functools.reduce(jnp.maximum, [jnp.max(s, axis=-1, keepdims=True) for s in scores])
    p = jnp.exp(scores[0] - m)
    l = jnp.sum(p, axis=-1, keepdims=True)
    acc = _dot(p.astype(BF16), kc_own)
    for c in range(npages // chunk):
        p = jnp.exp(scores[c + 1] - m)
        l = l + jnp.sum(p, axis=-1, keepdims=True)
        acc = acc + _dot(p.astype(BF16), kcb_ref[c * ckeys:(c + 1) * ckeys, :])
    o_ref[...] = (acc / l).reshape(o_ref.shape)


def _mla_sample_attn(qlat, qrope, ckv, krope, pool_lat, pool_rope_t, layer, page_table, *, seqlen, scale):
    heads, t, lora = qlat.shape
    rope = qrope.shape[2]
    nb, npages = page_table.shape
    chunk = math.gcd(npages, SAMPLE_KEY_PAGES)
    assert seqlen == SUBLANES
    grid_spec = pltpu.PrefetchScalarGridSpec(
        num_scalar_prefetch=1,
        grid=(nb,),
        in_specs=[pl.BlockSpec((heads, seqlen, lora), lambda b, pt: (0, b, 0)),
                  pl.BlockSpec((heads, seqlen, rope), lambda b, pt: (0, b, 0)),
                  pl.BlockSpec((seqlen, lora), lambda b, pt: (b, 0)),
                  pl.BlockSpec((seqlen, rope), lambda b, pt: (b, 0)),
                  pl.BlockSpec(memory_space=pl.ANY), pl.BlockSpec(memory_space=pl.ANY)],
        out_specs=pl.BlockSpec((heads, seqlen, lora), lambda b, pt: (0, b, 0)),
        scratch_shapes=[pltpu.VMEM((2, npages, PAGE_SIZE, lora), F32), pltpu.VMEM((2, npages, rope, PAGE_SIZE), F32),
                        pltpu.VMEM((npages * PAGE_SIZE, lora), BF16),
                        pltpu.SemaphoreType.DMA((2,)), pltpu.SemaphoreType.DMA((2,))],
    )
    return pl.pallas_call(
        functools.partial(_mla_sample_kernel, layer=layer, npages=npages, chunk=chunk, scale=scale),
        grid_spec=grid_spec,
        out_shape=jax.ShapeDtypeStruct((heads, t, lora), F32),
        compiler_params=_cparams(1),
        name="mla_sample_attn",
    )(page_table, qlat, qrope, ckv, krope, pool_lat, pool_rope_t)


def _mla_out_kernel(o_ref, x_ref, wuv_ref, wo_ref, pw_ref, out_ref, *, heads):
    parts = [_dot(o_ref[h].astype(BF16), wuv_ref[h]) for h in range(heads)]
    o = jnp.concatenate(parts, axis=1).astype(BF16)
    out_ref[...] = x_ref[...] + _rms(_dot(o, wo_ref[...]), pw_ref[...])


def _mla_out(o, x, wuv, wo, pw):
    heads, t, lora = o.shape
    d = x.shape[1]
    tm = _row_tile(t, 512)
    row = pl.BlockSpec((tm, d), lambda i: (i, 0))
    return pl.pallas_call(
        functools.partial(_mla_out_kernel, heads=heads),
        grid=(t // tm,),
        in_specs=[pl.BlockSpec((heads, tm, lora), lambda i: (0, i, 0)), row,
                  _full(wuv.shape), _full(wo.shape), _full(pw.shape)],
        out_specs=row,
        out_shape=jax.ShapeDtypeStruct((t, d), F32),
        compiler_params=_cparams(),
        name="mla_out",
    )(o, x, wuv, wo, pw)


def _pad_lanes(a, width=LANES):
    return jnp.pad(a, [(0, 0)] * (a.ndim - 1) + [(0, width - a.shape[-1])])


def _hist8(prev):
    return jnp.pad(prev, ((0, 0), (SUBLANES - prev.shape[1], 0), (0, 0)))


def _rope_tables(pos, rope, heads):
    half = rope // 2
    inv = ROPE_THETA ** (-jnp.arange(half, dtype=F32) / half)

    col = jnp.arange(LANES)
    ang = pos.astype(F32)[:, None] * inv[col % half][None, :]
    sign = jnp.where(col % rope < half, -1.0, 1.0).astype(F32)
    return jnp.cos(ang), jnp.sin(ang) * sign[None, :]


def _swap_halves(w, rope):
    k, n = w.shape
    w = w.reshape(k, n // rope, 2, rope // 2)
    return w[:, :, ::-1, :].reshape(k, n)


def _s5_params(a_re, a_im, log_step, b_re, b_im, c_re, c_im, d_model):
    g, s = a_re.shape
    grp = d_model // g
    delta = jnp.exp(log_step)[:, None]
    mag = jnp.exp(a_re * delta)
    ang = a_im * delta
    lam_re, lam_im = mag * jnp.cos(ang), mag * jnp.sin(ang)
    den = a_re * a_re + a_im * a_im
    nr, ni = lam_re - 1.0, lam_im
    f_re = (nr * a_re + ni * a_im) / den
    f_im = (ni * a_re - nr * a_im) / den
    bb_re = f_re[..., None] * b_re - f_im[..., None] * b_im
    bb_im = f_re[..., None] * b_im + f_im[..., None] * b_re
    kw = 256 if d_model % 256 == 0 else d_model
    kb = d_model // kw
    gpb = kw // grp

    def block_diag_in(bb):
        bb = bb.reshape(kb, gpb, s, grp)
        eye = jnp.eye(gpb, dtype=F32)
        w = jnp.einsum("kgsc,gh->kgchs", bb, eye)
        return w.reshape(kb, gpb * grp, gpb * s).astype(BF16)

    def block_diag_out(cc):
        cc = cc.reshape(kb, gpb, grp, s)
        eye = jnp.eye(gpb, dtype=F32)
        w = jnp.einsum("kgcs,gh->kgshc", cc, eye)
        return w.reshape(kb, gpb * s, gpb * grp).astype(BF16)

    nst = g * s // LANES
    return dict(wbr=block_diag_in(bb_re), wbi=block_diag_in(bb_im),
                wcr=block_diag_out(c_re), wci=block_diag_out(c_im),
                lam_re=lam_re.reshape(nst, LANES), lam_im=lam_im.reshape(nst, LANES))


def kernel(x_prompt, x_sample, state_ssd, state_ssd_conv, state_s5_re, state_s5_im, cache_mla_latent, cache_mla_krope, page_table, state_ffn_conv, norm_mix_pre, norm_mix_post, norm_ffn_pre, norm_ffn_post, ssd_w_in, ssd_conv_w, ssd_conv_b, ssd_dt_bias, ssd_a_log, ssd_d, ssd_norm, ssd_w_out, s5_a_re, s5_a_im, s5_log_step, s5_b_re, s5_b_im, s5_c_re, s5_c_im, s5_d, s5_w_val, s5_w_gate, mla_w_dq, mla_q_norm, mla_w_uq, mla_w_dkv, mla_kv_norm, mla_w_uk, mla_w_uv, mla_w_o, ffn_w_gate, ffn_w_up, ffn_conv_w, ffn_conv_b, ffn_w_down):
    b_p, l_p, d = x_prompt.shape
    b_s, l_s, _ = x_sample.shape
    depth = norm_mix_pre.shape[0]
    past_len = page_table.shape[1] * PAGE_SIZE
    groups_of = {"p": (b_p, l_p), "s": (b_s, l_s)}
    hid = {"p": x_prompt.reshape(b_p * l_p, d), "s": x_sample.reshape(b_s * l_s, d)}
    outs = {k: {n: [] for n in ("ssd_c", "s5r", "s5i", "lat", "kr", "ffc")} for k in ("p", "s")}
    ssd_h = {"p": None, "s": None}

    _, _, nh, hd, ds = state_ssd.shape
    di = nh * hd
    cd = state_ssd_conv.shape[-1]
    ssd_groups = (cd - di) // (2 * ds)
    s5_g, s5_s = s5_a_re.shape[1:]
    nst = s5_g * s5_s // LANES
    lora, heads, nope = mla_w_uk.shape[1:]
    rope = cache_mla_krope.shape[-1]
    mla_scale = 1.0 / math.sqrt(nope + rope)
    row1 = lambda a: a.reshape(1, -1)
    ffn_wg, ffn_wu, ffn_wd = ffn_w_gate.astype(BF16), ffn_w_up.astype(BF16), ffn_w_down.astype(BF16)

    for i in range(depth):
        kind, j = i % N_MIXERS, i // N_MIXERS
        if kind == 0:
            wz, wx = ssd_w_in[j, :, :di].astype(BF16), ssd_w_in[j, :, di:di + cd].astype(BF16)
            wdt = _pad_lanes(ssd_w_in[j, :, di + cd:]).astype(BF16)
            dtb, alog = (_pad_lanes(row1(a[j])) for a in (ssd_dt_bias, ssd_a_log))
            dsk = jnp.repeat(ssd_d[j], hd).reshape(1, di)
            wo = ssd_w_out[j].astype(BF16)
            for k, (nb, sl) in groups_of.items():
                if k == "p":
                    prev8 = jnp.zeros((nb * SUBLANES, cd), F32)
                    h0, h0_layer = jnp.zeros((1, nb, nh, hd, ds), F32), 0
                else:
                    prev8 = _hist8(state_ssd_conv[j]).reshape(nb * SUBLANES, cd)
                    h0, h0_layer = state_ssd, j
                z, xact, dt, cnew = _ssd_in(hid[k], prev8, row1(norm_mix_pre[i]), wz, wx, wdt, ssd_conv_w[j],
                                            row1(ssd_conv_b[j]), dtb, nb=nb, seqlen=sl)
                y, ssd_h[k] = _ssd_scan(xact, dt, h0, h0_layer, alog, dsk, ssd_h[k], nb=nb, seqlen=sl,
                                        groups=ssd_groups)
                hid[k] = _ssd_out(y, z, hid[k], row1(ssd_norm[j]), wo, row1(norm_mix_post[i]), groups=ssd_groups)
                outs[k]["ssd_c"].append(cnew.reshape(nb, SUBLANES, cd)[:, SUBLANES - (SSD_CONV - 1):, :])
        elif kind == 1:
            sp = _s5_params(s5_a_re[j], s5_a_im[j], s5_log_step[j], s5_b_re[j], s5_b_im[j], s5_c_re[j], s5_c_im[j], d)
            wv, wg = s5_w_val[j].astype(BF16), s5_w_gate[j].astype(BF16)
            for k, (nb, sl) in groups_of.items():
                if k == "p":
                    h0r = h0i = jnp.zeros((nb, nst, LANES), F32)
                    lamr, lami = sp["lam_re"], sp["lam_im"]
                else:
                    h0r, h0i = state_s5_re[j].reshape(nb, nst * LANES), state_s5_im[j].reshape(nb, nst * LANES)
                    lamr, lami = sp["lam_re"].reshape(1, -1), sp["lam_im"].reshape(1, -1)
                hid[k], hr, hi = _s5(hid[k], h0r, h0i, row1(norm_mix_pre[i]), sp["wbr"], sp["wbi"], lamr, lami,
                                     sp["wcr"], sp["wci"], row1(s5_d[j]), wv, wg, row1(norm_mix_post[i]),
                                     nb=nb, seqlen=sl)
                outs[k]["s5r"].append(hr.reshape(nb, s5_g, s5_s))
                outs[k]["s5i"].append(hi.reshape(nb, s5_g, s5_s))
        else:
            w_uq = mla_w_uq[j].reshape(-1, heads, nope + rope)
            wqn = w_uq[:, :, :nope].reshape(-1, heads * nope).astype(BF16)
            wqr = w_uq[:, :, nope:].reshape(-1, heads * rope).astype(BF16)
            wqrs = _swap_halves(wqr, rope)
            wkv = mla_w_dkv[j][:, :lora].astype(BF16)
            wkr = mla_w_dkv[j][:, lora:].astype(BF16)
            wkrs = _swap_halves(wkr, rope)
            wukt = jnp.transpose(mla_w_uk[j], (1, 2, 0)).astype(BF16)
            wuv = jnp.transpose(mla_w_uv[j], (1, 0, 2)).astype(BF16)
            wdq, wo = mla_w_dq[j].astype(BF16), mla_w_o[j].astype(BF16)
            for k, (nb, sl) in groups_of.items():
                t = nb * sl
                if k == "p":
                    tm = _row_tile(sl, 256)
                    tabs = _rope_tables(jnp.arange(sl), rope, heads)
                    pos_tiles = sl // tm
                else:
                    tm = _row_tile(t, 256)
                    tabs = _rope_tables(past_len + jnp.arange(tm) % sl, rope, heads)
                    pos_tiles = 1
                proj = _mla_proj(
                    hid[k], tabs, row1(norm_mix_pre[i]), wdq, row1(mla_q_norm[j]), wqn, wqr, wqrs, wukt, wkv, wkr, wkrs,
                    row1(mla_kv_norm[j]), pos_tiles=pos_tiles, tm=tm, for_prompt=k == "p",
                    qscale=mla_scale * LOG2E if k == "p" else 1.0)
                qlat, qrope, ckv, kr = proj[:4]
                if k == "p":
                    o = _mla_prompt_attn(qlat, qrope, *proj[4:], nb=nb, seqlen=sl)
                else:
                    o = _mla_sample_attn(qlat, qrope, ckv, kr, cache_mla_latent, jnp.swapaxes(cache_mla_krope, 2, 3),
                                         j, page_table, seqlen=sl, scale=mla_scale)
                hid[k] = _mla_out(o, hid[k], wuv, wo, row1(norm_mix_post[i]))
                outs[k]["lat"].append(ckv.reshape(nb, sl, lora))
                outs[k]["kr"].append(kr.reshape(nb, sl, rope))
        f = ffn_wg.shape[2]
        for k, (nb, sl) in groups_of.items():
            if k == "p":
                prev8 = jnp.zeros((nb * SUBLANES, f), F32)
            else:
                prev8 = _hist8(state_ffn_conv[i]).reshape(nb * SUBLANES, f)
            hid[k], cnew = _ffn(hid[k], prev8, row1(norm_ffn_pre[i]), ffn_wg, ffn_wu, ffn_conv_w[i],
                                row1(ffn_conv_b[i]), ffn_wd, row1(norm_ffn_post[i]), i, nb=nb, seqlen=sl)
            outs[k]["ffc"].append(cnew.reshape(nb, SUBLANES, f)[:, SUBLANES - (FFN_CONV - 1):, :])

    res = [hid["p"].reshape(b_p, l_p, d), hid["s"].reshape(b_s, l_s, d)]
    for k in ("p", "s"):
        res += [ssd_h[k]] + [jnp.stack(outs[k][n]) for n in ("ssd_c", "s5r", "s5i", "lat", "kr", "ffc")]
    return tuple(res)
```
